```python
import jax, jax.numpy as jnp
from jax import lax
import numpy as np

D_MODEL = 1024
BATCH = 2
SEQ = 8192
DEPTH = 2
DEC_BATCH = 32
DEC_SEQ = 8
PAST_LEN = 16384
PAGE_SIZE = 128

N_BRANCH = 4
MIX = D_MODEL // N_BRANCH
N_HEADS = 4
HEAD_DIM = MIX // N_HEADS
ROT_DIMS = HEAD_DIM // 4
ROPE_THETA = 500000.0
RET_THETA = 10000.0
MOBA_BLOCK = 256
MOBA_TOPK = 3
Q_BLOCK = 128
GMLP_CHUNK = 128
RET_CHUNK = 128
D_FF = -(-8 * D_MODEL // (3 * 256)) * 256
D_PLE = 256
EPS = 1e-6
W_IN = 12 * MIX + N_HEADS
SPLITS = (MIX, 2 * MIX, 3 * MIX, 4 * MIX, 5 * MIX, 6 * MIX, 6 * MIX + N_HEADS, 7 * MIX + N_HEADS,
          8 * MIX + N_HEADS, 9 * MIX + N_HEADS, 10 * MIX + N_HEADS, 11 * MIX + N_HEADS)

kernel_name = 'hybrid_moba_fox_gmlp_retention_step'

F32 = jnp.float32


def rmsnorm(x, g):
    xf = x.astype(F32)
    y = xf * lax.rsqrt(jnp.mean(xf * xf, axis=-1, keepdims=True) + EPS)
    return (y * g.astype(F32)).astype(x.dtype)


def layernorm(x, g, b):
    xf = x.astype(F32)
    mu = jnp.mean(xf, axis=-1, keepdims=True)
    var = jnp.mean(jnp.square(xf - mu), axis=-1, keepdims=True)
    return ((xf - mu) * lax.rsqrt(var + EPS) * g.astype(F32) + b.astype(F32)).astype(x.dtype)


def rope(x, pos, n_rot, theta):
    half = n_rot // 2
    inv = theta ** (-jnp.arange(half, dtype=F32) / half)
    ang = pos.astype(F32)[:, None] * inv[None, :]
    cos = jnp.cos(ang)[:, None, :]
    sin = jnp.sin(ang)[:, None, :]
    xr = x[..., :n_rot].astype(F32)
    x1, x2 = xr[..., :half], xr[..., half:]
    rot = jnp.concatenate([x1 * cos - x2 * sin, x2 * cos + x1 * sin], axis=-1).astype(x.dtype)
    return jnp.concatenate([rot, x[..., n_rot:]], axis=-1)


def q_block_size(L):
    return Q_BLOCK if L % Q_BLOCK == 0 else L


def moba_block(q, pos, kb, vb, kmean):
    B, L, H, Dh = q.shape
    nb = kb.shape[2]
    n_top = min(MOBA_TOPK, nb)
    qf = q.astype(F32)
    own = pos // MOBA_BLOCK
    gate = jnp.einsum('blhd,bhnd->bhln', qf, kmean)
    past = jnp.arange(nb)[None, :] < own[:, None]
    gate = jnp.where(past[None, None], gate, -jnp.inf)
    _, top = lax.top_k(gate, n_top)
    top_ok = jnp.arange(n_top)[None, :] < own[:, None]
    own_b = jnp.broadcast_to(own[None, None, :, None], (B, H, L, 1)).astype(top.dtype)
    sel = jnp.concatenate([top, own_b], axis=-1)
    ok = jnp.concatenate([top_ok, jnp.ones((L, 1), bool)], axis=-1)
    bi = jnp.arange(B)[:, None, None, None]
    hi = jnp.arange(H)[None, :, None, None]
    ks = kb[bi, hi, sel].astype(F32)
    vs = vb[bi, hi, sel].astype(F32)
    kpos = sel[..., None] * MOBA_BLOCK + jnp.arange(MOBA_BLOCK)
    mask = ok[None, None, :, :, None] & (kpos <= pos[None, None, :, None, None])
    s = jnp.einsum('blhd,bhlkjd->bhlkj', qf, ks) * (HEAD_DIM ** -0.5)
    s = jnp.where(mask, s, -jnp.inf)
    p = jax.nn.softmax(s.reshape(B, H, L, -1), axis=-1).reshape(s.shape)
    return jnp.einsum('bhlkj,bhlkjd->blhd', p, vs)


def moba_attention(q, k, v, pos):
    B, T, H, Dh = k.shape
    L = q.shape[1]
    nb = -(-T // MOBA_BLOCK)
    pad = nb * MOBA_BLOCK - T
    k = jnp.pad(k, ((0, 0), (0, pad), (0, 0), (0, 0)))
    v = jnp.pad(v, ((0, 0), (0, pad), (0, 0), (0, 0)))
    kb = k.reshape(B, nb, MOBA_BLOCK, H, Dh).transpose(0, 3, 1, 2, 4)
    vb = v.reshape(B, nb, MOBA_BLOCK, H, Dh).transpose(0, 3, 1, 2, 4)
    kmean = jnp.mean(kb.astype(F32), axis=3)
    qb = q_block_size(L)
    nq = L // qb
    qs = q.reshape(B, nq, qb, H, Dh).transpose(1, 0, 2, 3, 4)
    ps = pos.reshape(nq, qb)
    o = lax.map(lambda a: moba_block(a[0], a[1], kb, vb, kmean), (qs, ps))
    return o.transpose(1, 0, 2, 3, 4).reshape(B, L, H, Dh)


def fox_attention(q, k, v, c, pos):
    B, L, H, Dh = q.shape
    T = k.shape[1]
    kf = k.astype(F32)
    vf = v.astype(F32)
    ct = c.transpose(0, 2, 1)
    cq = ct[:, :, T - L:]
    kidx = jnp.arange(T)
    qb = q_block_size(L)
    nq = L // qb
    qs = q.astype(F32).reshape(B, nq, qb, H, Dh).transpose(1, 0, 2, 3, 4)
    cqs = cq.reshape(B, H, nq, qb).transpose(2, 0, 1, 3)
    ps = pos.reshape(nq, qb)

    def one_block(a):
        qq, cc, pp = a
        s = jnp.einsum('blhd,bthd->bhlt', qq, kf) * (HEAD_DIM ** -0.5) + cc[..., None] - ct[:, :, None, :]
        s = jnp.where(kidx[None, :] <= pp[:, None], s, -jnp.inf)
        p = jax.nn.softmax(s, axis=-1)
        return jnp.einsum('bhlt,bthd->blhd', p, vf)

    o = lax.map(one_block, (qs, cqs, ps))
    return o.transpose(1, 0, 2, 3, 4).reshape(B, L, H, Dh)


def gmlp_spatial(u, v, ws, bs):
    B, L, _ = v.shape
    nc = -(-L // GMLP_CHUNK)
    pad = nc * GMLP_CHUNK - L
    vp = jnp.pad(v, ((0, 0), (0, pad), (0, 0))).reshape(B, nc, GMLP_CHUNK, N_HEADS, HEAD_DIM)
    tril = jnp.tril(jnp.ones((GMLP_CHUNK, GMLP_CHUNK), bool))
    wm = jnp.where(tril[None], ws, 0.0).astype(v.dtype)
    mixed = jnp.einsum('gts,bcsgd->bctgd', wm, vp) + bs.T.astype(v.dtype)[None, None, :, :, None]
    mixed = mixed.reshape(B, nc * GMLP_CHUNK, MIX)[:, :L]
    return u * mixed


def retention_chunked(q, k, v, s0):
    B, L, H, Dh = q.shape
    c = RET_CHUNK if L % RET_CHUNK == 0 else L
    n = L // c
    lg = jnp.log(1.0 - 2.0 ** (-5.0 - jnp.arange(H, dtype=F32)))
    j = jnp.arange(c, dtype=F32)
    diff = j[:, None] - j[None, :]
    dmat = jnp.where(diff[None] >= 0, jnp.exp(jnp.maximum(diff, 0.0)[None] * lg[:, None, None]), 0.0)
    q_dec = jnp.exp((j[:, None] + 1.0) * lg[None, :])[None, :, :, None]
    k_dec = jnp.exp((c - 1.0 - j)[:, None] * lg[None, :])[None, :, :, None]
    chunk_dec = jnp.exp(c * lg)[None, :, None, None]
    split = lambda t: t.reshape(B, n, c, H, Dh).transpose(1, 0, 2, 3, 4)

    def step(S, inp):
        qc, kc, vc = inp
        att = jnp.einsum('bihd,bjhd->bhij', qc, kc) * dmat[None]
        o = jnp.einsum('bhij,bjhd->bihd', att, vc) + jnp.einsum('bihd,bhde->bihe', qc, S) * q_dec
        S = S * chunk_dec + jnp.einsum('bjhd,bjhe->bhde', kc * k_dec, vc)
        return S, o

    S, o = lax.scan(step, s0, (split(q), split(k), split(v)))
    return o.transpose(1, 0, 2, 3, 4).reshape(B, L, H, Dh), S


def head_groupnorm(o, g):
    mu = jnp.mean(o, axis=-1, keepdims=True)
    var = jnp.mean(jnp.square(o - mu), axis=-1, keepdims=True)
    return (o - mu) * lax.rsqrt(var + EPS) * g.astype(F32).reshape(N_HEADS, HEAD_DIM)


def token_mixers(xn, pos, past, w_in, b_forget, gmlp_ln_g, gmlp_ln_b, gmlp_ws, gmlp_bs,
                 ret_gn_g, w_branch, w_merge, w_out):
    B, L, _ = xn.shape
    dt = xn.dtype
    proj = xn @ w_in
    qa, ka, va, qb, kb, vb, fb, uc, vc, qd, kd, vd, gd = jnp.split(proj, SPLITS, axis=-1)
    heads = lambda t: t.reshape(B, L, N_HEADS, HEAD_DIM)
    qa = rope(heads(qa), pos, ROT_DIMS, ROPE_THETA)
    ka = rope(heads(ka), pos, ROT_DIMS, ROPE_THETA)
    va = heads(va)
    qb, kb, vb = heads(qb), heads(kb), heads(vb)
    logf = jax.nn.log_sigmoid(fb.astype(F32) + b_forget.astype(F32))
    qd = rope(heads(qd), pos, HEAD_DIM, RET_THETA)
    kd = rope(heads(kd), pos, HEAD_DIM, RET_THETA) * (HEAD_DIM ** -0.5)
    if past is None:
        ka_all, va_all, kb_all, vb_all, logf_all = ka, va, kb, vb, logf
        s0 = jnp.zeros((B, N_HEADS, HEAD_DIM, HEAD_DIM), F32)
    else:
        pk_a, pv_a, pk_b, pv_b, plogf, ps0 = past
        ka_all = jnp.concatenate([pk_a.astype(dt), ka], axis=1)
        va_all = jnp.concatenate([pv_a.astype(dt), va], axis=1)
        kb_all = jnp.concatenate([pk_b.astype(dt), kb], axis=1)
        vb_all = jnp.concatenate([pv_b.astype(dt), vb], axis=1)
        logf_all = jnp.concatenate([plogf.astype(F32), logf], axis=1)
        s0 = ps0.astype(F32)
    oa = moba_attention(qa, ka_all, va_all, pos).astype(dt)
    ob = fox_attention(qb, kb_all, vb_all, jnp.cumsum(logf_all, axis=1), pos).astype(dt)
    u = jax.nn.gelu(uc)
    vn = layernorm(jax.nn.gelu(vc), gmlp_ln_g, gmlp_ln_b)
    oc = gmlp_spatial(u, vn, gmlp_ws, gmlp_bs)
    od, s_new = retention_chunked(qd.astype(F32), kd.astype(F32), heads(vd).astype(F32), s0)
    od = (jax.nn.silu(gd.astype(F32)) * head_groupnorm(od, ret_gn_g).reshape(B, L, MIX)).astype(dt)
    branches = jnp.stack([oa.reshape(B, L, MIX), ob.reshape(B, L, MIX), oc, od], axis=2)
    br = jnp.einsum('blnm,nmd->blnd', branches, w_branch)
    gates = jax.nn.sigmoid(xn @ w_merge).reshape(B, L, N_BRANCH, D_MODEL)
    y = jnp.sum(gates * br, axis=2) @ w_out
    return y, (ka, va, kb, vb, logf, s_new, vn)


def decoder_layer(h, p_i, pos, past, prm):
    (norm_mix, w_in, b_forget, gmlp_ln_g, gmlp_ln_b, gmlp_ws, gmlp_bs, ret_gn_g, w_branch, w_merge,
     w_out, norm_ffn, w_ffn_in, w_ffn_out, norm_ple, w_ple_gate, w_ple_proj) = prm
    y, rows = token_mixers(rmsnorm(h, norm_mix), pos, past, w_in, b_forget, gmlp_ln_g, gmlp_ln_b,
                           gmlp_ws, gmlp_bs, ret_gn_g, w_branch, w_merge, w_out)
    h = h + y
    g, u = jnp.split(rmsnorm(h, norm_ffn) @ w_ffn_in, 2, axis=-1)
    h = h + (jax.nn.silu(g) * u) @ w_ffn_out
    h = h + jax.nn.sigmoid(rmsnorm(h, norm_ple) @ w_ple_gate) * (p_i @ w_ple_proj)
    return h, rows


def stack_layers(rows, j):
    return jnp.stack([r[j] for r in rows], axis=0)


def setup_inputs(seed: int = 0) -> dict:
    key = jax.random.key(seed)
    ks = iter(jax.random.split(key, 48))

    def nrm(shape, scale=1.0):
        return jax.random.normal(next(ks), shape, F32) * scale

    n_pages = PAST_LEN // PAGE_SIZE
    n_used = DEC_BATCH * n_pages
    n_pool = n_used + -(-n_used // 4)
    page_table = jax.random.permutation(next(ks), n_pool)[:n_used].reshape(DEC_BATCH, n_pages).astype(jnp.int32)
    kv_shape = (DEPTH, n_pool, PAGE_SIZE, N_HEADS, HEAD_DIM)
    return {
        'x_prompt': nrm((BATCH, SEQ, D_MODEL)),
        'x_sample': nrm((DEC_BATCH, DEC_SEQ, D_MODEL)),
        'cache_k_a': nrm(kv_shape),
        'cache_v_a': nrm(kv_shape),
        'cache_k_b': nrm(kv_shape),
        'cache_v_b': nrm(kv_shape),
        'cache_logf_b': jax.nn.log_sigmoid(3.0 + nrm((DEPTH, n_pool, PAGE_SIZE, N_HEADS))),
        'state_ret': nrm((DEPTH, DEC_BATCH, N_HEADS, HEAD_DIM, HEAD_DIM), 0.5),
        'page_table': page_table,
        'p_prompt': nrm((DEPTH, BATCH, SEQ, D_PLE)),
        'p_sample': nrm((DEPTH, DEC_BATCH, DEC_SEQ, D_PLE)),
        'norm_mix': 1.0 + nrm((DEPTH, D_MODEL), 0.05),
        'w_in': nrm((DEPTH, D_MODEL, W_IN), D_MODEL ** -0.5),
        'b_forget': 3.0 + nrm((DEPTH, N_HEADS), 0.5),
        'gmlp_ln_g': 1.0 + nrm((DEPTH, MIX), 0.05),
        'gmlp_ln_b': nrm((DEPTH, MIX), 0.02),
        'gmlp_ws': nrm((DEPTH, N_HEADS, GMLP_CHUNK, GMLP_CHUNK), GMLP_CHUNK ** -0.5),
        'gmlp_bs': 1.0 + nrm((DEPTH, N_HEADS, GMLP_CHUNK), 0.1),
        'ret_gn_g': 1.0 + nrm((DEPTH, MIX), 0.05),
        'w_branch': nrm((DEPTH, N_BRANCH, MIX, D_MODEL), MIX ** -0.5),
        'w_merge': nrm((DEPTH, D_MODEL, N_BRANCH * D_MODEL), D_MODEL ** -0.5),
        'w_out': nrm((DEPTH, D_MODEL, D_MODEL), D_MODEL ** -0.5),
        'norm_ffn': 1.0 + nrm((DEPTH, D_MODEL), 0.05),
        'w_ffn_in': nrm((DEPTH, D_MODEL, 2 * D_FF), D_MODEL ** -0.5),
        'w_ffn_out': nrm((DEPTH, D_FF, D_MODEL), D_FF ** -0.5),
        'norm_ple': 1.0 + nrm((DEPTH, D_MODEL), 0.05),
        'w_ple_gate': nrm((DEPTH, D_MODEL, D_MODEL), D_MODEL ** -0.5),
        'w_ple_proj': nrm((DEPTH, D_PLE, D_MODEL), D_PLE ** -0.5),
        'norm_final': 1.0 + nrm((D_MODEL,), 0.05),
    }


def reference(x_prompt, x_sample, cache_k_a, cache_v_a, cache_k_b, cache_v_b, cache_logf_b, state_ret,
              page_table, p_prompt, p_sample, norm_mix, w_in, b_forget, gmlp_ln_g, gmlp_ln_b, gmlp_ws,
              gmlp_bs, ret_gn_g, w_branch, w_merge, w_out, norm_ffn, w_ffn_in, w_ffn_out, norm_ple,
              w_ple_gate, w_ple_proj, norm_final):
    pos_p = jnp.arange(x_prompt.shape[1], dtype=jnp.int32)
    pos_s = PAST_LEN + jnp.arange(x_sample.shape[1], dtype=jnp.int32)
    dbs = page_table.shape[0]
    hp, hs = x_prompt, x_sample
    rows_p, rows_s = [], []
    for i in range(DEPTH):
        prm = (norm_mix[i], w_in[i], b_forget[i], gmlp_ln_g[i], gmlp_ln_b[i], gmlp_ws[i], gmlp_bs[i],
               ret_gn_g[i], w_branch[i], w_merge[i], w_out[i], norm_ffn[i], w_ffn_in[i], w_ffn_out[i],
               norm_ple[i], w_ple_gate[i], w_ple_proj[i])
        hp, rp = decoder_layer(hp, p_prompt[i], pos_p, None, prm)
        past = (cache_k_a[i, page_table].reshape(dbs, -1, N_HEADS, HEAD_DIM),
                cache_v_a[i, page_table].reshape(dbs, -1, N_HEADS, HEAD_DIM),
                cache_k_b[i, page_table].reshape(dbs, -1, N_HEADS, HEAD_DIM),
                cache_v_b[i, page_table].reshape(dbs, -1, N_HEADS, HEAD_DIM),
                cache_logf_b[i, page_table].reshape(dbs, -1, N_HEADS),
                state_ret[i])
        hs, rs = decoder_layer(hs, p_sample[i], pos_s, past, prm)
        rows_p.append(rp)
        rows_s.append(rs)
    y_prompt = rmsnorm(hp, norm_final)
    y_sample = rmsnorm(hs, norm_final)
    return (y_prompt, y_sample,
            stack_layers(rows_p, 0), stack_layers(rows_p, 1), stack_layers(rows_p, 2), stack_layers(rows_p, 3),
            stack_layers(rows_p, 4), stack_layers(rows_p, 5),
            stack_layers(rows_s, 0), stack_layers(rows_s, 1), stack_layers(rows_s, 2), stack_layers(rows_s, 3),
            stack_layers(rows_s, 4), stack_layers(rows_s, 5), stack_layers(rows_s, 6))
```

```python
import functools

import jax
import jax.numpy as jnp
from jax import lax
from jax.experimental import pallas as pl
from jax.experimental.pallas import tpu as pltpu

F32 = jnp.float32
BF16 = jnp.bfloat16

N_HEADS = 4
HEAD_DIM = 64
MIX = N_HEADS * HEAD_DIM
ROT_DIMS = HEAD_DIM // 4
ROPE_THETA = 500000.0
RET_THETA = 10000.0
MOBA_BLOCK = 256
MOBA_TOPK = 3
GMLP_CHUNK = 128
RET_CHUNK = 128
PAGE_SIZE = 128
EPS = 1e-6
SCALE = HEAD_DIM ** -0.5
NEG_INF = float("-inf")

LANES = 128
VMEM_LIMIT = 56 * 1024 * 1024
PAGES_PER_STEP = 16
N_PROJ_GROUPS = 12
W_IN_PAD = N_PROJ_GROUPS * MIX + LANES


def _params(*sem):
    return pltpu.CompilerParams(dimension_semantics=sem, vmem_limit_bytes=VMEM_LIMIT)


def _rms(x, g):
    return x * lax.rsqrt(jnp.mean(x * x, axis=-1, keepdims=True) + EPS) * g


def _nt_dot(a, b, precision=None):
    return lax.dot_general(a, b, (((1,), (1,)), ((), ())), precision=precision,
                           preferred_element_type=F32)


def _log_sigmoid(x):
    return jnp.minimum(x, 0.0) - jnp.log(1.0 + jnp.exp(-jnp.abs(x)))


def _proj_kernel(h_ref, g_ref, w_ref, bf_ref, lng_ref, lnb_ref, ca_ref, sa_ref, cd_ref, sd_ref, o_ref):
    xn = _rms(h_ref[...], g_ref[...]).astype(BF16)
    lane = lax.broadcasted_iota(jnp.int32, (1, MIX), 1) & (HEAD_DIM - 1)

    def rope(y, c_ref, s_ref, half):
        c = jnp.concatenate([c_ref[...], c_ref[...]], axis=1)
        s = jnp.concatenate([s_ref[...], s_ref[...]], axis=1)
        rot = jnp.where(lane < half, pltpu.roll(y, MIX - half, 1), pltpu.roll(y, half, 1))
        return y * c + rot * s

    for grp in range(N_PROJ_GROUPS):
        y = jnp.dot(xn, w_ref[:, grp * MIX:(grp + 1) * MIX], preferred_element_type=F32)
        if grp in (0, 1):
            y = rope(y, ca_ref, sa_ref, ROT_DIMS // 2)
        elif grp == 6:
            y = jax.nn.gelu(y)
        elif grp == 7:
            y = jax.nn.gelu(y)
            mu = jnp.mean(y, axis=-1, keepdims=True)
            yc = y - mu
            var = jnp.mean(yc * yc, axis=-1, keepdims=True)
            y = yc * lax.rsqrt(var + EPS) * lng_ref[...] + lnb_ref[...]
        elif grp == 8:
            y = rope(y, cd_ref, sd_ref, HEAD_DIM // 2)
        elif grp == 9:
            y = rope(y, cd_ref, sd_ref, HEAD_DIM // 2) * SCALE
        o_ref[:, grp * MIX:(grp + 1) * MIX] = y
    f = jnp.dot(xn, w_ref[:, N_PROJ_GROUPS * MIX:], preferred_element_type=F32) + bf_ref[...]
    o_ref[:, N_PROJ_GROUPS * MIX:] = _log_sigmoid(f)


def _proj(h, g, w_pad, bf_pad, ln_g, ln_b, tabs, tm):
    n, d = h.shape
    ca, sa, cd, sd = tabs
    tab_blocks = ca.shape[0] // tm
    full = lambda shape: pl.BlockSpec(shape, lambda i: (0, 0))
    tab = pl.BlockSpec((tm, LANES), lambda i: (i % tab_blocks, 0))
    return pl.pallas_call(
        _proj_kernel,
        out_shape=jax.ShapeDtypeStruct((n, W_IN_PAD), F32),
        grid=(n // tm,),
        in_specs=[pl.BlockSpec((tm, d), lambda i: (i, 0)), full((1, d)), full((d, W_IN_PAD)),
                  full((1, LANES)), full((1, MIX)), full((1, MIX)), tab, tab, tab, tab],
        out_specs=pl.BlockSpec((tm, W_IN_PAD), lambda i: (i, 0)),
        compiler_params=_params("parallel"),
        name="proj",
    )(h, g, w_pad, bf_pad, ln_g, ln_b, ca, sa, cd, sd)


def _rope_tables(pos, n_rot, theta):
    half = n_rot // 2
    inv = theta ** (-jnp.arange(half, dtype=F32) / half)
    ang = pos.astype(F32)[:, None] * inv[None, :]
    cos, sin = jnp.cos(ang), jnp.sin(ang)
    rest = HEAD_DIM - n_rot
    ones = jnp.ones((pos.shape[0], rest), F32)
    zeros = jnp.zeros((pos.shape[0], rest), F32)
    c = jnp.concatenate([cos, cos, ones], axis=1)
    s = jnp.concatenate([-sin, sin, zeros], axis=1)
    reps = LANES // HEAD_DIM
    return jnp.tile(c, (1, reps)), jnp.tile(s, (1, reps))


def _kmean_kernel(k_ref, o_ref, *, nblk):
    x = k_ref[...].reshape(nblk, MOBA_BLOCK, MIX)
    o_ref[...] = jnp.sum(x, axis=1) * (1.0 / MOBA_BLOCK)


def _kmean_prompt(ka, b, l):
    nblk = 8
    assert l % (nblk * MOBA_BLOCK) == 0
    return pl.pallas_call(
        functools.partial(_kmean_kernel, nblk=nblk),
        out_shape=jax.ShapeDtypeStruct((b, l // MOBA_BLOCK, MIX), F32),
        grid=(b, l // (nblk * MOBA_BLOCK)),
        in_specs=[pl.BlockSpec((None, nblk * MOBA_BLOCK, MIX), lambda i, j: (i, j, 0))],
        out_specs=pl.BlockSpec((None, nblk, MIX), lambda i, j: (i, j, 0)),
        compiler_params=_params("parallel", "parallel"),
        name="kmean_prompt",
    )(ka.reshape(b, l, MIX))


def _lane_cumsum(x):
    n = x.shape[-1]
    lane = lax.broadcasted_iota(jnp.int32, x.shape, x.ndim - 1)
    s = 1
    while s < n:
        x = x + jnp.where(lane >= s, pltpu.roll(x, s, x.ndim - 1), 0.0)
        s *= 2
    return x


def _cumsum_kernel(x_ref, o_ref):
    o_ref[...] = _lane_cumsum(x_ref[...])


def _cumsum_prompt(logf_t):
    r, l = logf_t.shape
    return pl.pallas_call(
        _cumsum_kernel,
        out_shape=jax.ShapeDtypeStruct((r, l), F32),
        grid=(1,),
        in_specs=[pl.BlockSpec((r, l), lambda i: (0, 0))],
        out_specs=pl.BlockSpec((r, l), lambda i: (0, 0)),
        compiler_params=_params("arbitrary"),
        name="cumsum_prompt",
    )(logf_t)


def _softmax_first(s, v):
    m = jnp.max(s, axis=1, keepdims=True)
    p = jnp.exp(s - m)
    l = jnp.sum(p, axis=1, keepdims=True)
    acc = jnp.dot(p.astype(BF16), v, preferred_element_type=F32)
    return m, l, acc


def _softmax_update(carry, s, v):
    m, l, acc = carry
    m_new = jnp.maximum(m, jnp.max(s, axis=1, keepdims=True))
    alpha = jnp.exp(m - m_new)
    p = jnp.exp(s - m_new)
    l = alpha * l + jnp.sum(p, axis=1, keepdims=True)
    acc = alpha * acc + jnp.dot(p.astype(BF16), v, preferred_element_type=F32)
    return m_new, l, acc


def _topk_select(gate, n_valid_lanes, n_top):
    lane = lax.broadcasted_iota(jnp.int32, gate.shape, 1)
    sel = jnp.zeros(gate.shape, F32)
    g = gate
    for _ in range(n_top):
        mx = jnp.max(g, axis=1, keepdims=True)
        idx = jnp.min(jnp.where(g == mx, lane, n_valid_lanes), axis=1, keepdims=True)
        hit = lane == idx
        sel = jnp.maximum(sel, jnp.where(hit, jnp.where(mx > NEG_INF, 1.0, 0.0), 0.0))
        g = jnp.where(hit, NEG_INF, g)
    return sel


def _moba_prompt_kernel(q_ref, k_ref, v_ref, km_ref, o_ref, *, nb):
    t = MOBA_BLOCK
    i = pl.program_id(2)
    q = q_ref[...]
    qb = (q * SCALE).astype(BF16)
    gate = _nt_dot(q, km_ref[...], precision=lax.Precision.HIGHEST)
    blk = lax.broadcasted_iota(jnp.int32, (t, nb), 1)
    sel = _topk_select(jnp.where(blk < i, gate, NEG_INF), nb, min(MOBA_TOPK, nb))

    row = lax.broadcasted_iota(jnp.int32, (t, t), 0)
    col = lax.broadcasted_iota(jnp.int32, (t, t), 1)
    own = pl.multiple_of(i * t, t)
    s = jnp.where(col <= row, _nt_dot(qb, k_ref[pl.ds(own, t), :]), NEG_INF)
    carry = _softmax_first(s, v_ref[pl.ds(own, t), :])

    def body(j, carry):
        start = pl.multiple_of(j * t, t)
        s = _nt_dot(qb, k_ref[pl.ds(start, t), :])
        chosen = jnp.sum(jnp.where(blk == j, sel, 0.0), axis=1, keepdims=True)
        s = jnp.where(chosen > 0.5, s, NEG_INF)
        return _softmax_update(carry, s, v_ref[pl.ds(start, t), :])

    m, l, acc = lax.fori_loop(0, i, body, carry)
    o_ref[...] = acc / l


def _fox_prompt_kernel(q_ref, k_ref, v_ref, cq_ref, ck_ref, o_ref):
    t = MOBA_BLOCK
    i = pl.program_id(2)
    qb = (q_ref[...] * SCALE).astype(BF16)
    cq = cq_ref[...]
    row = lax.broadcasted_iota(jnp.int32, (t, t), 0)
    col = lax.broadcasted_iota(jnp.int32, (t, t), 1)
    own = pl.multiple_of(i * t, t)
    s = _nt_dot(qb, k_ref[pl.ds(own, t), :]) + cq - ck_ref[:, pl.ds(own, t)]
    carry = _softmax_first(jnp.where(col <= row, s, NEG_INF), v_ref[pl.ds(own, t), :])

    def body(j, carry):
        start = pl.multiple_of(j * t, t)
        s = _nt_dot(qb, k_ref[pl.ds(start, t), :]) + cq - ck_ref[:, pl.ds(start, t)]
        return _softmax_update(carry, s, v_ref[pl.ds(start, t), :])

    m, l, acc = lax.fori_loop(0, i, body, carry)
    o_ref[...] = acc / l


def _heads_spec(rows, index):
    return pl.BlockSpec((None, None, rows, HEAD_DIM), index)


def _moba_prompt(q, k_bf, v_bf, kmean):
    b, h, l, _ = q.shape
    nb = l // MOBA_BLOCK
    tile = lambda bi, hi, i: (bi, hi, i, 0)
    whole = lambda bi, hi, i: (bi, hi, 0, 0)
    return pl.pallas_call(
        functools.partial(_moba_prompt_kernel, nb=nb),
        out_shape=jax.ShapeDtypeStruct((b, h, l, HEAD_DIM), F32),
        grid=(b, h, nb),
        in_specs=[_heads_spec(MOBA_BLOCK, tile), _heads_spec(l, whole), _heads_spec(l, whole),
                  _heads_spec(nb, whole)],
        out_specs=_heads_spec(MOBA_BLOCK, tile),
        compiler_params=_params("parallel", "parallel", "arbitrary"),
        name="moba_prompt",
    )(q, k_bf, v_bf, kmean)


def _fox_prompt(q, k_bf, v_bf, c):
    b, h, l, _ = q.shape
    nb = l // MOBA_BLOCK
    tile = lambda bi, hi, i: (bi, hi, i, 0)
    whole = lambda bi, hi, i: (bi, hi, 0, 0)
    return pl.pallas_call(
        _fox_prompt_kernel,
        out_shape=jax.ShapeDtypeStruct((b, h, l, HEAD_DIM), F32),
        grid=(b, h, nb),
        in_specs=[_heads_spec(MOBA_BLOCK, tile), _heads_spec(l, whole), _heads_spec(l, whole),
                  pl.BlockSpec((None, None, MOBA_BLOCK, 1), tile),
                  pl.BlockSpec((None, None, 1, l), whole)],
        out_specs=_heads_spec(MOBA_BLOCK, tile),
        compiler_params=_params("parallel", "parallel", "arbitrary"),
        name="fox_prompt",
    )(q, k_bf, v_bf, c.reshape(b, h, l, 1), c.reshape(b, h, 1, l))


def _gmlp_kernel(u_ref, v_ref, w_ref, b_ref, o_ref, *, c):
    v = v_ref[...]
    grp = lax.broadcasted_iota(jnp.int32, v.shape, 1) // HEAD_DIM
    row = lax.broadcasted_iota(jnp.int32, (c, c), 0)
    col = lax.broadcasted_iota(jnp.int32, (c, c), 1)
    w_cat = jnp.concatenate([jnp.where(col <= row, w_ref[g], 0.0) for g in range(N_HEADS)], axis=1)
    v_cat = jnp.concatenate([jnp.where(grp == g, v, 0.0) for g in range(N_HEADS)], axis=0)
    mixed = jnp.dot(w_cat.astype(BF16), v_cat.astype(BF16), preferred_element_type=F32)
    o_ref[...] = u_ref[...] * (mixed + b_ref[...])


def _gmlp(u, v, ws, bs, c):
    n = u.shape[0]
    bias = jnp.repeat(bs.T, HEAD_DIM, axis=1)
    blk = pl.BlockSpec((c, MIX), lambda i: (i, 0))
    return pl.pallas_call(
        functools.partial(_gmlp_kernel, c=c),
        out_shape=jax.ShapeDtypeStruct((n, MIX), F32),
        grid=(n // c,),
        in_specs=[blk, blk, pl.BlockSpec((N_HEADS, c, c), lambda i: (0, 0, 0)),
                  pl.BlockSpec((c, MIX), lambda i: (0, 0))],
        out_specs=blk,
        compiler_params=_params("parallel"),
        name="gmlp",
    )(u, v, ws, bias)


def _retention_kernel(q_ref, kt_ref, v_ref, g_ref, s0_ref, dm_ref, qd_ref, kd_ref, cd_ref, gn_ref,
                      o_ref, s_ref, state, *, c, cps):
    step = pl.program_id(2)

    @pl.when(step == 0)
    def _():
        state[...] = s0_ref[...]

    dmat = dm_ref[...]
    q_dec = qd_ref[...]
    k_dec = kd_ref[...]
    chunk_dec = cd_ref[...]
    gn = gn_ref[...]
    s = state[...]
    for ci in range(cps):
        rows = slice(ci * c, (ci + 1) * c)
        qc = q_ref[rows, :].astype(BF16)
        vc = v_ref[rows, :].astype(BF16)
        kt = kt_ref[:, rows]
        att = jnp.dot(qc, kt.astype(BF16), preferred_element_type=F32) * dmat
        o = (jnp.dot(att.astype(BF16), vc, preferred_element_type=F32)
             + jnp.dot(qc, s.astype(BF16), preferred_element_type=F32) * q_dec)
        s = s * chunk_dec + jnp.dot((kt * k_dec).astype(BF16), vc, preferred_element_type=F32)
        mu = jnp.mean(o, axis=-1, keepdims=True)
        oc = o - mu
        var = jnp.mean(oc * oc, axis=-1, keepdims=True)
        o_ref[rows, :] = jax.nn.silu(g_ref[rows, :]) * (oc * lax.rsqrt(var + EPS) * gn)
    state[...] = s

    @pl.when(step == pl.num_programs(2) - 1)
    def _():
        s_ref[...] = s


def _retention(q, kt, v, g, s0, gn_g, c, cps):
    b, h, l, dh = q.shape
    lg = jnp.log(1.0 - 2.0 ** (-5.0 - jnp.arange(h, dtype=F32)))
    j = jnp.arange(c, dtype=F32)
    diff = j[:, None] - j[None, :]
    dmat = jnp.where(diff[None] >= 0, jnp.exp(jnp.maximum(diff, 0.0)[None] * lg[:, None, None]), 0.0)
    q_dec = jnp.broadcast_to(jnp.exp((j[None, :] + 1.0) * lg[:, None])[:, :, None], (h, c, dh))
    k_dec = jnp.broadcast_to(jnp.exp((c - 1.0 - j)[None, :] * lg[:, None])[:, None, :], (h, dh, c))
    chunk_dec = jnp.broadcast_to(jnp.exp(c * lg)[:, None, None], (h, dh, dh))
    r = c * cps
    rows = pl.BlockSpec((None, None, r, dh), lambda bi, hi, i: (bi, hi, i, 0))
    per_bh = pl.BlockSpec((None, None, dh, dh), lambda bi, hi, i: (bi, hi, 0, 0))
    per_h = lambda s1, s2: pl.BlockSpec((None, s1, s2), lambda bi, hi, i: (hi, 0, 0))
    return pl.pallas_call(
        functools.partial(_retention_kernel, c=c, cps=cps),
        out_shape=(jax.ShapeDtypeStruct((b, h, l, dh), F32), jax.ShapeDtypeStruct((b, h, dh, dh), F32)),
        grid=(b, h, l // r),
        in_specs=[rows, pl.BlockSpec((None, None, dh, r), lambda bi, hi, i: (bi, hi, 0, i)), rows, rows,
                  per_bh, per_h(c, c), per_h(c, dh), per_h(dh, c), per_h(dh, dh), per_h(1, dh)],
        out_specs=(rows, per_bh),
        scratch_shapes=[pltpu.VMEM((dh, dh), F32)],
        compiler_params=_params("parallel", "parallel", "arbitrary"),
        name="retention",
    )(q, kt, v, g, s0, dmat, q_dec, k_dec, chunk_dec, gn_g.reshape(h, 1, dh))


def _merge_kernel(h_ref, g_ref, oa_ref, ob_ref, oc_ref, od_ref, wm_ref, wb_ref, wo_ref, o_ref):
    h = h_ref[...]
    d = h.shape[1]
    xn = _rms(h, g_ref[...]).astype(BF16)
    z = None
    for n, br_ref in enumerate((oa_ref, ob_ref, oc_ref, od_ref)):
        gate = jax.nn.sigmoid(jnp.dot(xn, wm_ref[:, n * d:(n + 1) * d], preferred_element_type=F32))
        br = jnp.dot(br_ref[...].astype(BF16), wb_ref[n], preferred_element_type=F32)
        z = gate * br if z is None else z + gate * br
    o_ref[...] = h + jnp.dot(z.astype(BF16), wo_ref[...], preferred_element_type=F32)


def _merge(h, g, branches, w_merge, w_branch, w_out, tm):
    n, d = h.shape
    rows = lambda w: pl.BlockSpec((tm, w), lambda i: (i, 0))
    return pl.pallas_call(
        _merge_kernel,
        out_shape=jax.ShapeDtypeStruct((n, d), F32),
        grid=(n // tm,),
        in_specs=[rows(d), pl.BlockSpec((1, d), lambda i: (0, 0)), rows(MIX), rows(MIX), rows(MIX), rows(MIX),
                  pl.BlockSpec(w_merge.shape, lambda i: (0, 0)),
                  pl.BlockSpec(w_branch.shape, lambda i: (0, 0, 0)),
                  pl.BlockSpec(w_out.shape, lambda i: (0, 0))],
        out_specs=rows(d),
        compiler_params=_params("parallel"),
        name="merge",
    )(h, g, *branches, w_merge, w_branch, w_out)


def _ffn_kernel(h_ref, g_ref, wg_ref, wu_ref, wo_ref, o_ref, xn_ref, acc_ref):
    k = pl.program_id(1)

    @pl.when(k == 0)
    def _():
        xn_ref[...] = _rms(h_ref[...], g_ref[...]).astype(BF16)
        acc_ref[...] = h_ref[...]

    xn = xn_ref[...]
    gate = jnp.dot(xn, wg_ref[...], preferred_element_type=F32)
    up = jnp.dot(xn, wu_ref[...], preferred_element_type=F32)
    act = (jax.nn.silu(gate) * up).astype(BF16)
    acc_ref[...] += jnp.dot(act, wo_ref[...], preferred_element_type=F32)

    @pl.when(k == pl.num_programs(1) - 1)
    def _():
        o_ref[...] = acc_ref[...]


def _ffn(h, g, w_in, w_out, tm, tf):
    n, d = h.shape
    d_ff = w_out.shape[0]
    nk = d_ff // tf
    return pl.pallas_call(
        _ffn_kernel,
        out_shape=jax.ShapeDtypeStruct((n, d), F32),
        grid=(n // tm, nk),
        in_specs=[pl.BlockSpec((tm, d), lambda i, k: (i, 0)), pl.BlockSpec((1, d), lambda i, k: (0, 0)),
                  pl.BlockSpec((d, tf), lambda i, k: (0, k)), pl.BlockSpec((d, tf), lambda i, k: (0, nk + k)),
                  pl.BlockSpec((tf, d), lambda i, k: (k, 0))],
        out_specs=pl.BlockSpec((tm, d), lambda i, k: (i, 0)),
        scratch_shapes=[pltpu.VMEM((tm, d), BF16), pltpu.VMEM((tm, d), F32)],
        compiler_params=_params("parallel", "arbitrary"),
        name="ffn",
    )(h, g, w_in, w_in, w_out)


def _ple_kernel(h_ref, g_ref, p_ref, wg_ref, wp_ref, gf_ref, o_ref, *, final):
    h = h_ref[...]
    xn = _rms(h, g_ref[...]).astype(BF16)
    gate = jax.nn.sigmoid(jnp.dot(xn, wg_ref[...], preferred_element_type=F32))
    emb = jnp.dot(p_ref[...].astype(BF16), wp_ref[...], preferred_element_type=F32)
    out = h + gate * emb
    o_ref[...] = _rms(out, gf_ref[...]) if final else out


def _ple(h, g, p, w_gate, w_proj, g_final, tm, final):
    n, d = h.shape
    dp = p.shape[1]
    vec = pl.BlockSpec((1, d), lambda i: (0, 0))
    return pl.pallas_call(
        functools.partial(_ple_kernel, final=final),
        out_shape=jax.ShapeDtypeStruct((n, d), F32),
        grid=(n // tm,),
        in_specs=[pl.BlockSpec((tm, d), lambda i: (i, 0)), vec, pl.BlockSpec((tm, dp), lambda i: (i, 0)),
                  pl.BlockSpec((d, d), lambda i: (0, 0)), pl.BlockSpec((dp, d), lambda i: (0, 0)), vec],
        out_specs=pl.BlockSpec((tm, d), lambda i: (i, 0)),
        compiler_params=_params("parallel"),
        name="ple",
    )(h, g, p, w_gate, w_proj, g_final)


def _page_specs(layer, block, pp):
    return [pl.BlockSpec((None, None) + block, lambda b, c, pt, k=k: (layer, pt[b, c * pp + k], 0, 0))
            for k in range(pp)]


def _kmean_sample_kernel(pt_ref, *refs, pp):
    pages, o_ref = refs[:pp], refs[pp]
    per_block = MOBA_BLOCK // PAGE_SIZE
    rows = []
    for blk in range(pp // per_block):
        tot = None
        for k in range(per_block):
            s = jnp.sum(pages[blk * per_block + k][...], axis=0, keepdims=True)
            tot = s if tot is None else tot + s
        rows.append(tot * (1.0 / MOBA_BLOCK))
    o_ref[...] = jnp.concatenate(rows, axis=0)


def _kmean_sample(cache_k, layer, page_table):
    db, n_pages = page_table.shape
    pp = PAGES_PER_STEP
    per_block = MOBA_BLOCK // PAGE_SIZE
    grid_spec = pltpu.PrefetchScalarGridSpec(
        num_scalar_prefetch=1, grid=(db, n_pages // pp),
        in_specs=_page_specs(layer, (PAGE_SIZE, MIX), pp),
        out_specs=pl.BlockSpec((None, pp // per_block, MIX), lambda b, c, pt: (b, c, 0)))
    return pl.pallas_call(
        functools.partial(_kmean_sample_kernel, pp=pp),
        out_shape=jax.ShapeDtypeStruct((db, n_pages // per_block, MIX), F32),
        grid_spec=grid_spec,
        compiler_params=_params("parallel", "arbitrary"),
        name="kmean_sample",
    )(page_table, *([cache_k] * pp))


def _cumsum_sample_kernel(pt_ref, *refs, pp):
    pages, new_ref, o_ref, onew_ref, carry = refs[:pp], refs[pp], refs[pp + 1], refs[pp + 2], refs[pp + 3]
    c = pl.program_id(1)

    @pl.when(c == 0)
    def _():
        carry[...] = jnp.zeros(carry.shape, F32)

    x = jnp.concatenate([p[...] for p in pages], axis=1)
    cs = _lane_cumsum(x) + carry[:, :1]
    o_ref[...] = cs
    total = cs[:, pp * PAGE_SIZE - 1:]
    carry[...] = jnp.broadcast_to(total, carry.shape)

    @pl.when(c == pl.num_programs(1) - 1)
    def _():
        onew_ref[...] = _lane_cumsum(new_ref[...]) + total


def _cumsum_sample(cache_logf_t, layer, page_table, logf_new):
    db, n_pages = page_table.shape
    pp = PAGES_PER_STEP
    h = N_HEADS
    grid_spec = pltpu.PrefetchScalarGridSpec(
        num_scalar_prefetch=1, grid=(db, n_pages // pp),
        in_specs=_page_specs(layer, (h, PAGE_SIZE), pp)
        + [pl.BlockSpec((None, h, LANES), lambda b, c, pt: (b, 0, 0))],
        out_specs=(pl.BlockSpec((None, h, pp * PAGE_SIZE), lambda b, c, pt: (b, 0, c)),
                   pl.BlockSpec((None, h, LANES), lambda b, c, pt: (b, 0, 0))),
        scratch_shapes=[pltpu.VMEM((h, LANES), F32)])
    return pl.pallas_call(
        functools.partial(_cumsum_sample_kernel, pp=pp),
        out_shape=(jax.ShapeDtypeStruct((db, h, n_pages * PAGE_SIZE), F32),
                   jax.ShapeDtypeStruct((db, h, LANES), F32)),
        grid_spec=grid_spec,
        compiler_params=_params("parallel", "arbitrary"),
        name="cumsum_sample",
    )(page_table, *([cache_logf_t] * pp), logf_new)


def _expand_heads(x, t):
    return jnp.concatenate([jnp.broadcast_to(x[h:h + 1, :], (t, x.shape[1])) for h in range(N_HEADS)], axis=0)


def _decode_attn_kernel(pt_ref, *refs, pp, t, fox):
    kp, vp = refs[:pp], refs[pp:2 * pp]
    q_ref, kn_ref, vn_ref = refs[2 * pp:2 * pp + 3]
    if fox:
        cq_ref, cn_ref, ck_ref = refs[2 * pp + 3:2 * pp + 6]
        o_ref, s_ref, m_ref, l_ref, acc_ref = refs[2 * pp + 6:]
    else:
        km_ref = refs[2 * pp + 3]
        o_ref, s_ref, m_ref, l_ref, acc_ref, sel_ref = refs[2 * pp + 4:]
    c = pl.program_id(1)
    r = N_HEADS * t
    q = q_ref[...]
    qb = (q * SCALE).astype(BF16)

    @pl.when(c == 0)
    def _():
        s = _nt_dot(qb, kn_ref[...].astype(BF16))
        if fox:
            s = s + cq_ref[...] - _expand_heads(cn_ref[...], t)
        key = lax.broadcasted_iota(jnp.int32, (r, LANES), 1)
        qi = lax.broadcasted_iota(jnp.int32, (r, LANES), 0) % t
        m, l, acc = _softmax_first(jnp.where(key <= qi, s, NEG_INF), vn_ref[...].astype(BF16))
        m_ref[...] = jnp.broadcast_to(m, m_ref.shape)
        l_ref[...] = jnp.broadcast_to(l, l_ref.shape)
        acc_ref[...] = acc
        if not fox:
            nb = km_ref.shape[0]
            gate = _nt_dot(q, km_ref[...], precision=lax.Precision.HIGHEST)
            sel_ref[...] = _topk_select(gate, nb, min(MOBA_TOPK, nb))

    for k in range(pp):
        s_ref[:, k * PAGE_SIZE:(k + 1) * PAGE_SIZE] = _nt_dot(qb, kp[k][...].astype(BF16))
    s = s_ref[...]
    if fox:
        cq = jnp.concatenate([cq_ref[...]] * pp, axis=1)
        s = s + cq - _expand_heads(ck_ref[...], t)
    else:
        per_block = MOBA_BLOCK // PAGE_SIZE
        nblk = pp // per_block
        sel = sel_ref[...]
        blk = lax.broadcasted_iota(jnp.int32, sel.shape, 1)
        cols = []
        for b in range(nblk):
            chosen = jnp.sum(jnp.where(blk == c * nblk + b, sel, 0.0), axis=1, keepdims=True)
            cols.append(jnp.broadcast_to(chosen, (r, MOBA_BLOCK)))
        s = jnp.where(jnp.concatenate(cols, axis=1) > 0.5, s, NEG_INF)
    m = m_ref[:, :1]
    m_new = jnp.maximum(m, jnp.max(s, axis=1, keepdims=True))
    alpha = jnp.exp(m - m_new)
    p = jnp.exp(s - m_new).astype(BF16)
    l_new = alpha * l_ref[:, :1] + jnp.sum(p.astype(F32), axis=1, keepdims=True)
    acc = alpha * acc_ref[...]
    for k in range(pp):
        acc = acc + jnp.dot(p[:, k * PAGE_SIZE:(k + 1) * PAGE_SIZE], vp[k][...].astype(BF16),
                            preferred_element_type=F32)
    m_ref[...] = jnp.broadcast_to(m_new, m_ref.shape)
    l_ref[...] = jnp.broadcast_to(l_new, l_ref.shape)
    acc_ref[...] = acc

    @pl.when(c == pl.num_programs(1) - 1)
    def _():
        o = acc / l_new
        grp = lax.broadcasted_iota(jnp.int32, (t, MIX), 1) // HEAD_DIM
        out = jnp.zeros((t, MIX), F32)
        for h in range(N_HEADS):
            out = out + jnp.where(grp == h, o[h * t:(h + 1) * t, :], 0.0)
        o_ref[...] = out


def _decode_attn(cache_k, cache_v, layer, page_table, q_blk, k_new, v_new, extra, fox):
    db, n_pages = page_table.shape
    pp = PAGES_PER_STEP
    r = q_blk.shape[1]
    t = r // N_HEADS
    per_b = lambda s1, s2: pl.BlockSpec((None, s1, s2), lambda b, c, pt: (b, 0, 0))
    in_specs = (_page_specs(layer, (PAGE_SIZE, MIX), pp) + _page_specs(layer, (PAGE_SIZE, MIX), pp)
                + [per_b(r, MIX), per_b(LANES, MIX), per_b(LANES, MIX)])
    scratch = [pltpu.VMEM((r, pp * PAGE_SIZE), F32), pltpu.VMEM((r, LANES), F32), pltpu.VMEM((r, LANES), F32),
               pltpu.VMEM((r, MIX), F32)]
    if fox:
        cq, c_new, c_past = extra
        in_specs += [per_b(r, LANES), per_b(N_HEADS, LANES),
                     pl.BlockSpec((None, N_HEADS, pp * PAGE_SIZE), lambda b, c, pt: (b, 0, c))]
    else:
        (kmean,) = extra
        in_specs += [per_b(kmean.shape[1], MIX)]
        scratch += [pltpu.VMEM((r, kmean.shape[1]), F32)]
    grid_spec = pltpu.PrefetchScalarGridSpec(
        num_scalar_prefetch=1, grid=(db, n_pages // pp), in_specs=in_specs,
        out_specs=pl.BlockSpec((None, t, MIX), lambda b, c, pt: (b, 0, 0)),
        scratch_shapes=scratch)
    return pl.pallas_call(
        functools.partial(_decode_attn_kernel, pp=pp, t=t, fox=fox),
        out_shape=jax.ShapeDtypeStruct((db, t, MIX), F32),
        grid_spec=grid_spec,
        compiler_params=_params("parallel", "arbitrary"),
        name="fox_sample" if fox else "moba_sample",
    )(page_table, *([cache_k] * pp), *([cache_v] * pp), q_blk, k_new, v_new, *extra)


def _to_heads(x, b, l):
    return x.reshape(b, l, N_HEADS, HEAD_DIM).transpose(0, 2, 1, 3)


def _from_heads(x):
    b, h, l, dh = x.shape
    return x.transpose(0, 2, 1, 3).reshape(b * l, h * dh)


def _split_proj(proj):
    cols = [proj[:, g * MIX:(g + 1) * MIX] for g in range(N_PROJ_GROUPS)]
    logf = proj[:, N_PROJ_GROUPS * MIX:N_PROJ_GROUPS * MIX + N_HEADS]
    return cols, logf


def _tail(h, p, prm, branches, final, tm):
    h = _merge(h, prm["norm_mix"], branches, prm["w_merge"], prm["w_branch"], prm["w_out"], tm)
    d_ff = prm["w_ffn_out"].shape[0]
    tf = d_ff // 2 if (d_ff // 2) % LANES == 0 else d_ff
    h = _ffn(h, prm["norm_ffn"], prm["w_ffn_in"], prm["w_ffn_out"], min(512, h.shape[0]), tf)
    return _ple(h, prm["norm_ple"], p, prm["w_ple_gate"], prm["w_ple_proj"], prm["norm_final"], tm, final)


def _prompt_layer(h, p, prm, tabs, b, l, final):
    tm = 256
    proj = _proj(h, prm["norm_mix"], prm["w_in"], prm["b_forget"], prm["ln_g"], prm["ln_b"], tabs, tm)
    (qa, ka, va, qb, kb, vb, u, vn, qd, kd, vd, gd), logf = _split_proj(proj)
    hd = lambda x: _to_heads(x, b, l)
    kmean = _kmean_prompt(ka, b, l)
    kmean = kmean.reshape(b, -1, N_HEADS, HEAD_DIM).transpose(0, 2, 1, 3)
    oa = _from_heads(_moba_prompt(hd(qa), hd(ka).astype(BF16), hd(va).astype(BF16), kmean))
    c = _cumsum_prompt(logf.reshape(b, l, N_HEADS).transpose(0, 2, 1).reshape(b * N_HEADS, l))
    ob = _from_heads(_fox_prompt(hd(qb), hd(kb).astype(BF16), hd(vb).astype(BF16),
                                 c.reshape(b, N_HEADS, l)))
    oc = _gmlp(u, vn, prm["gmlp_ws"], prm["gmlp_bs"], GMLP_CHUNK)
    s0 = jnp.zeros((b, N_HEADS, HEAD_DIM, HEAD_DIM), F32)
    cps = 8 if l % (8 * RET_CHUNK) == 0 else 1
    od, s_new = _retention(hd(qd), hd(kd).transpose(0, 1, 3, 2), hd(vd), hd(gd), s0, prm["ret_gn_g"],
                           RET_CHUNK, cps)
    h = _tail(h, p, prm, (oa, ob, oc, _from_heads(od)), final, tm)
    rows = (ka.reshape(b, l, N_HEADS, HEAD_DIM), va.reshape(b, l, N_HEADS, HEAD_DIM),
            kb.reshape(b, l, N_HEADS, HEAD_DIM), vb.reshape(b, l, N_HEADS, HEAD_DIM),
            logf.reshape(b, l, N_HEADS), s_new)
    return h, rows


def _block_diag_q(q, db, t):
    grp = jnp.arange(MIX) // HEAD_DIM
    mask = (grp[None, None, :] == jnp.arange(N_HEADS)[:, None, None]).astype(F32)
    return (q.reshape(db, 1, t, MIX) * mask[None]).reshape(db, N_HEADS * t, MIX)


def _pad_rows(x, db, t, rows):
    return jnp.pad(x.reshape(db, t, -1), ((0, 0), (0, rows - t), (0, 0)))


def _sample_layer(h, p, prm, tabs, caches, layer, page_table, db, t, final):
    cache_k_a, cache_v_a, cache_k_b, cache_v_b, cache_logf_t, state = caches
    n = db * t
    proj = _proj(h, prm["norm_mix"], prm["w_in"], prm["b_forget"], prm["ln_g"], prm["ln_b"], tabs, n)
    (qa, ka, va, qb, kb, vb, u, vn, qd, kd, vd, gd), logf = _split_proj(proj)
    kmean = _kmean_sample(cache_k_a, layer, page_table)
    oa = _decode_attn(cache_k_a, cache_v_a, layer, page_table, _block_diag_q(qa, db, t),
                      _pad_rows(ka, db, t, LANES), _pad_rows(va, db, t, LANES), (kmean,), fox=False)
    logf_new = jnp.pad(logf.reshape(db, t, N_HEADS).transpose(0, 2, 1), ((0, 0), (0, 0), (0, LANES - t)))
    c_past, c_new = _cumsum_sample(cache_logf_t, layer, page_table, logf_new)
    cq = jnp.broadcast_to(c_new[:, :, :t].reshape(db, N_HEADS * t, 1), (db, N_HEADS * t, LANES))
    ob = _decode_attn(cache_k_b, cache_v_b, layer, page_table, _block_diag_q(qb, db, t),
                      _pad_rows(kb, db, t, LANES), _pad_rows(vb, db, t, LANES), (cq, c_new, c_past), fox=True)
    oc = _gmlp(u, vn, prm["gmlp_ws"][:, :t, :t], prm["gmlp_bs"][:, :t], t)
    hd = lambda x: _to_heads(x, db, t)
    od, s_new = _retention(hd(qd), hd(kd).transpose(0, 1, 3, 2), hd(vd), hd(gd), state[layer],
                           prm["ret_gn_g"], t, 1)
    h = _tail(h, p, prm, (oa.reshape(n, MIX), ob.reshape(n, MIX), oc, _from_heads(od)), final, n)
    rows = (ka.reshape(db, t, N_HEADS, HEAD_DIM), va.reshape(db, t, N_HEADS, HEAD_DIM),
            kb.reshape(db, t, N_HEADS, HEAD_DIM), vb.reshape(db, t, N_HEADS, HEAD_DIM),
            logf.reshape(db, t, N_HEADS), s_new, vn.reshape(db, t, MIX))
    return h, rows


def kernel(x_prompt, x_sample, cache_k_a, cache_v_a, cache_k_b, cache_v_b, cache_logf_b, state_ret, page_table, p_prompt, p_sample, norm_mix, w_in, b_forget, gmlp_ln_g, gmlp_ln_b, gmlp_ws, gmlp_bs, ret_gn_g, w_branch, w_merge, w_out, norm_ffn, w_ffn_in, w_ffn_out, norm_ple, w_ple_gate, w_ple_proj, norm_final):
    b, l, d = x_prompt.shape
    db, t, _ = x_sample.shape
    depth = w_in.shape[0]
    n_pages = page_table.shape[1]
    past_len = n_pages * PAGE_SIZE
    assert d == N_HEADS * MIX and l % MOBA_BLOCK == 0
    assert past_len % MOBA_BLOCK == 0 and t <= LANES and t % 8 == 0 and t < MOBA_BLOCK
    assert n_pages % PAGES_PER_STEP == 0

    pos_p = jnp.arange(l, dtype=jnp.int32)
    pos_s = jnp.tile(past_len + jnp.arange(t, dtype=jnp.int32), db)
    tabs_p = _rope_tables(pos_p, ROT_DIMS, ROPE_THETA) + _rope_tables(pos_p, HEAD_DIM, RET_THETA)
    tabs_s = _rope_tables(pos_s, ROT_DIMS, ROPE_THETA) + _rope_tables(pos_s, HEAD_DIM, RET_THETA)

    pool = cache_k_a.shape[1]
    pages = lambda x: x.reshape(depth, pool, PAGE_SIZE, MIX)
    caches = (pages(cache_k_a), pages(cache_v_a), pages(cache_k_b), pages(cache_v_b),
              cache_logf_b.transpose(0, 1, 3, 2), state_ret)

    hp = x_prompt.reshape(b * l, d)
    hs = x_sample.reshape(db * t, d)
    rows_p, rows_s = [], []
    for i in range(depth):
        w_split = w_in[i]
        w_main = jnp.concatenate([w_split[:, :6 * MIX], w_split[:, 6 * MIX + N_HEADS:]], axis=1)
        w_f = jnp.pad(w_split[:, 6 * MIX:6 * MIX + N_HEADS], ((0, 0), (0, LANES - N_HEADS)))
        prm = dict(
            norm_mix=norm_mix[i][None], w_in=jnp.concatenate([w_main, w_f], axis=1).astype(BF16),
            b_forget=jnp.pad(b_forget[i], (0, LANES - N_HEADS))[None],
            ln_g=gmlp_ln_g[i][None], ln_b=gmlp_ln_b[i][None], gmlp_ws=gmlp_ws[i], gmlp_bs=gmlp_bs[i],
            ret_gn_g=ret_gn_g[i], w_branch=w_branch[i].astype(BF16), w_merge=w_merge[i].astype(BF16),
            w_out=w_out[i].astype(BF16), norm_ffn=norm_ffn[i][None], w_ffn_in=w_ffn_in[i].astype(BF16),
            w_ffn_out=w_ffn_out[i].astype(BF16), norm_ple=norm_ple[i][None],
            w_ple_gate=w_ple_gate[i].astype(BF16), w_ple_proj=w_ple_proj[i].astype(BF16),
            norm_final=norm_final[None])
        final = i == depth - 1
        hp, rp = _prompt_layer(hp, p_prompt[i].reshape(b * l, -1), prm, tabs_p, b, l, final)
        hs, rs = _sample_layer(hs, p_sample[i].reshape(db * t, -1), prm, tabs_s, caches, i, page_table,
                               db, t, final)
        rows_p.append(rp)
        rows_s.append(rs)
    stack = lambda rows, j: jnp.stack([r[j] for r in rows], axis=0)
    return ((hp.reshape(b, l, d), hs.reshape(db, t, d))
            + tuple(stack(rows_p, j) for j in range(6)) + tuple(stack(rows_s, j) for j in range(7)))
```

```python
import functools

import jax
import jax.numpy as jnp
from jax import lax
from jax.experimental import pallas as pl
from jax.experimental.pallas import tpu as pltpu

F32 = jnp.float32
BF16 = jnp.bfloat16

N_HEADS = 4
HEAD_DIM = 64
MIX = N_HEADS * HEAD_DIM
ROT_DIMS = HEAD_DIM // 4
ROPE_THETA = 500000.0
RET_THETA = 10000.0
MOBA_BLOCK = 256
MOBA_TOPK = 3
GMLP_CHUNK = 128
RET_CHUNK = 128
PAGE_SIZE = 128
EPS = 1e-6
SCALE = HEAD_DIM ** -0.5
NEG_INF = float("-inf")

LANES = 128
VMEM_LIMIT = 56 * 1024 * 1024
PAGES_PER_STEP = 16
N_PROJ_GROUPS = 12
N_ATTN_GROUPS = 6
W_IN_PAD = N_PROJ_GROUPS * MIX + LANES
LOG2E = 1.4426950408889634
AUG = 8


def _params(*sem):
    return pltpu.CompilerParams(dimension_semantics=sem, vmem_limit_bytes=VMEM_LIMIT)


def _rms(x, g):
    return x * lax.rsqrt(jnp.mean(x * x, axis=-1, keepdims=True) + EPS) * g


def _nt_dot(a, b, precision=None):
    return lax.dot_general(a, b, (((1,), (1,)), ((), ())), precision=precision,
                           preferred_element_type=F32)


def _log_sigmoid(x):
    return jnp.minimum(x, 0.0) - jnp.log(1.0 + jnp.exp(-jnp.abs(x)))


def _proj_kernel(h_ref, g_ref, w_ref, bf_ref, lng_ref, lnb_ref, ca_ref, sa_ref, cd_ref, sd_ref, o_ref,
                 *t_refs):
    xn = _rms(h_ref[...], g_ref[...]).astype(BF16)
    lane = lax.broadcasted_iota(jnp.int32, (1, MIX), 1) & (HEAD_DIM - 1)

    def rope(y, c_ref, s_ref, half):
        c = jnp.concatenate([c_ref[...], c_ref[...]], axis=1)
        s = jnp.concatenate([s_ref[...], s_ref[...]], axis=1)
        rot = jnp.where(lane < half, pltpu.roll(y, MIX - half, 1), pltpu.roll(y, half, 1))
        return y * c + rot * s

    for grp in range(N_PROJ_GROUPS):
        y = jnp.dot(xn, w_ref[:, grp * MIX:(grp + 1) * MIX], preferred_element_type=F32)
        if grp in (0, 1):
            y = rope(y, ca_ref, sa_ref, ROT_DIMS // 2)
        elif grp == 6:
            y = jax.nn.gelu(y)
        elif grp == 7:
            y = jax.nn.gelu(y)
            mu = jnp.mean(y, axis=-1, keepdims=True)
            yc = y - mu
            var = jnp.mean(yc * yc, axis=-1, keepdims=True)
            y = yc * lax.rsqrt(var + EPS) * lng_ref[...] + lnb_ref[...]
        elif grp == 8:
            y = rope(y, cd_ref, sd_ref, HEAD_DIM // 2)
        elif grp == 9:
            y = rope(y, cd_ref, sd_ref, HEAD_DIM // 2) * SCALE
        o_ref[:, grp * MIX:(grp + 1) * MIX] = y
        if t_refs and grp < N_ATTN_GROUPS:
            yt = y.T
            t_refs[0][grp] = yt
            if grp % 3 == 2:
                t_refs[1][grp // 3] = yt.astype(BF16)
    f = jnp.dot(xn, w_ref[:, N_PROJ_GROUPS * MIX:], preferred_element_type=F32) + bf_ref[...]
    o_ref[:, N_PROJ_GROUPS * MIX:] = _log_sigmoid(f)


def _proj(h, g, w_pad, bf_pad, ln_g, ln_b, tabs, tm, batch=None):
    n, d = h.shape
    ca, sa, cd, sd = tabs
    tab_blocks = ca.shape[0] // tm
    full = lambda shape: pl.BlockSpec(shape, lambda i: (0, 0))
    tab = pl.BlockSpec((tm, LANES), lambda i: (i % tab_blocks, 0))
    out_shape = [jax.ShapeDtypeStruct((n, W_IN_PAD), F32)]
    out_specs = [pl.BlockSpec((tm, W_IN_PAD), lambda i: (i, 0))]
    if batch is not None:
        l = n // batch
        per_seq = l // tm
        t_map = lambda i: (i // per_seq, 0, 0, i % per_seq)
        out_shape += [jax.ShapeDtypeStruct((batch, N_ATTN_GROUPS, MIX, l), F32),
                      jax.ShapeDtypeStruct((batch, N_ATTN_GROUPS // 3, MIX, l), BF16)]
        out_specs += [pl.BlockSpec((None, N_ATTN_GROUPS, MIX, tm), t_map),
                      pl.BlockSpec((None, N_ATTN_GROUPS // 3, MIX, tm), t_map)]
    return pl.pallas_call(
        _proj_kernel,
        out_shape=out_shape,
        grid=(n // tm,),
        in_specs=[pl.BlockSpec((tm, d), lambda i: (i, 0)), full((1, d)), full((d, W_IN_PAD)),
                  full((1, LANES)), full((1, MIX)), full((1, MIX)), tab, tab, tab, tab],
        out_specs=out_specs,
        compiler_params=_params("parallel"),
        name="proj",
    )(h, g, w_pad, bf_pad, ln_g, ln_b, ca, sa, cd, sd)


def _rope_tables(pos, n_rot, theta):
    half = n_rot // 2
    inv = theta ** (-jnp.arange(half, dtype=F32) / half)
    ang = pos.astype(F32)[:, None] * inv[None, :]
    cos, sin = jnp.cos(ang), jnp.sin(ang)
    rest = HEAD_DIM - n_rot
    ones = jnp.ones((pos.shape[0], rest), F32)
    zeros = jnp.zeros((pos.shape[0], rest), F32)
    c = jnp.concatenate([cos, cos, ones], axis=1)
    s = jnp.concatenate([-sin, sin, zeros], axis=1)
    reps = LANES // HEAD_DIM
    return jnp.tile(c, (1, reps)), jnp.tile(s, (1, reps))


def _kmean_kernel(k_ref, o_ref, *, nblk):
    x = k_ref[...].reshape(nblk, MOBA_BLOCK, MIX)
    o_ref[...] = jnp.sum(x, axis=1) * (1.0 / MOBA_BLOCK)


def _kmean_prompt(proj, col, b, l):
    nblk = 8
    assert l % (nblk * MOBA_BLOCK) == 0
    per_seq = l // (nblk * MOBA_BLOCK)
    return pl.pallas_call(
        functools.partial(_kmean_kernel, nblk=nblk),
        out_shape=jax.ShapeDtypeStruct((b, l // MOBA_BLOCK, MIX), F32),
        grid=(b, per_seq),
        in_specs=[pl.BlockSpec((nblk * MOBA_BLOCK, MIX), lambda i, j: (i * per_seq + j, col))],
        out_specs=pl.BlockSpec((None, nblk, MIX), lambda i, j: (i, j, 0)),
        compiler_params=_params("parallel", "parallel"),
        name="kmean_prompt",
    )(proj)


def _lane_cumsum(x):
    n = x.shape[-1]
    lane = lax.broadcasted_iota(jnp.int32, x.shape, x.ndim - 1)
    s = 1
    while s < n:
        x = x + jnp.where(lane >= s, pltpu.roll(x, s, x.ndim - 1), 0.0)
        s *= 2
    return x


def _cumsum_kernel(x_ref, o_ref):
    o_ref[...] = _lane_cumsum(x_ref[...])


def _cumsum_prompt(logf_t):
    r, l = logf_t.shape
    return pl.pallas_call(
        _cumsum_kernel,
        out_shape=jax.ShapeDtypeStruct((r, l), F32),
        grid=(1,),
        in_specs=[pl.BlockSpec((r, l), lambda i: (0, 0))],
        out_specs=pl.BlockSpec((r, l), lambda i: (0, 0)),
        compiler_params=_params("arbitrary"),
        name="cumsum_prompt",
    )(logf_t)


def _softmax_first(s, v):
    m = jnp.max(s, axis=1, keepdims=True)
    p = jnp.exp(s - m)
    l = jnp.sum(p, axis=1, keepdims=True)
    acc = jnp.dot(p.astype(BF16), v, preferred_element_type=F32)
    return m, l, acc


def _softmax_update(carry, s, v):
    m, l, acc = carry
    m_new = jnp.maximum(m, jnp.max(s, axis=1, keepdims=True))
    alpha = jnp.exp(m - m_new)
    p = jnp.exp(s - m_new)
    l = alpha * l + jnp.sum(p, axis=1, keepdims=True)
    acc = alpha * acc + jnp.dot(p.astype(BF16), v, preferred_element_type=F32)
    return m_new, l, acc


def _topk_select(gate, n_top, axis):
    pos = lax.broadcasted_iota(jnp.int32, gate.shape, axis)
    sel = jnp.zeros(gate.shape, F32)
    g = gate
    for _ in range(n_top):
        mx = jnp.max(g, axis=axis, keepdims=True)
        idx = jnp.min(jnp.where(g == mx, pos, gate.shape[axis]), axis=axis, keepdims=True)
        hit = pos == idx
        sel = jnp.maximum(sel, jnp.where(hit, jnp.where(mx > NEG_INF, 1.0, 0.0), 0.0))
        g = jnp.where(hit, NEG_INF, g)
    return sel


def _split3(x):
    hi = x.astype(BF16).astype(F32)
    r = x - hi
    mid = r.astype(BF16).astype(F32)
    return hi, mid, r - mid


def _attn_prompt_kernel(qt_ref, k_ref, vt_ref, x_ref, o_ref, qa_ref, *sel_refs, fox):
    t = MOBA_BLOCK
    i = pl.program_id(1)
    own = pl.multiple_of(i * t, t)
    causal = (lax.broadcasted_iota(jnp.int32, (t, t), 0) <= lax.broadcasted_iota(jnp.int32, (t, t), 1))
    pad = jnp.zeros((HEAD_DIM - AUG, t), F32)
    heads = range(N_HEADS)
    for h in heads:
        qh = qt_ref[h * HEAD_DIM:(h + 1) * HEAD_DIM, :]
        if fox:
            one = jnp.ones((1, t), F32)
            zero = jnp.zeros((1, t), F32)
            aug = jnp.concatenate(_split3(x_ref[h:h + 1, :] * LOG2E) + (one, one, one, zero, zero), axis=0)
        else:
            aug = jnp.zeros((AUG, t), F32)
            nb = x_ref.shape[1]
            gate = jnp.dot(x_ref[h], qh, precision=lax.Precision.HIGHEST, preferred_element_type=F32)
            blk = lax.broadcasted_iota(jnp.int32, (nb, t), 0)
            sel_refs[0][h] = _topk_select(jnp.where(blk < i, gate, NEG_INF), min(MOBA_TOPK, nb), 0)
        qa_ref[h] = jnp.concatenate([qh * (SCALE * LOG2E), aug, pad], axis=0).astype(BF16)

    ss = [jnp.dot(k_ref[h, pl.ds(own, t), :], qa_ref[h], preferred_element_type=F32) for h in heads]
    ss = [jnp.where(causal, s, NEG_INF) for s in ss]
    ms = [jnp.max(s, axis=0, keepdims=True) for s in ss]
    ps = [jnp.exp2(ss[h] - ms[h]) for h in heads]
    pvs = [jnp.dot(vt_ref[h * HEAD_DIM:(h + 1) * HEAD_DIM, pl.ds(own, t)], ps[h].astype(BF16),
                   preferred_element_type=F32) for h in heads]
    carry = []
    for h in heads:
        carry += [ms[h], jnp.sum(ps[h], axis=0, keepdims=True), pvs[h]]

    def body(j, carry):
        start = pl.multiple_of(j * t, t)
        ss =[jnp.dot(k_ref[h, pl.ds(start, t), :], qa_ref[h], preferred_element_type=F32) for h in heads]
        if not fox:
            ss = [jnp.where(sel_refs[0][h, pl.ds(j, 1), :] > 0.5, ss[h], NEG_INF) for h in heads]
        ms = [jnp.maximum(carry[3 * h], jnp.max(ss[h], axis=0, keepdims=True)) for h in heads]
        ps = [jnp.exp2(ss[h] - ms[h]) for h in heads]
        pvs = [jnp.dot(vt_ref[h * HEAD_DIM:(h + 1) * HEAD_DIM, pl.ds(start, t)], ps[h].astype(BF16),
                       preferred_element_type=F32) for h in heads]
        out = []
        for h in heads:
            alpha = jnp.exp2(carry[3 * h] - ms[h])
            l = alpha * carry[3 * h + 1] + jnp.sum(ps[h], axis=0, keepdims=True)
            out += [ms[h], l, alpha * carry[3 * h + 2] + pvs[h]]
        return tuple(out)

    carry = lax.fori_loop(0, i, body, tuple(carry))
    ot = jnp.concatenate([carry[3 * h + 2] / carry[3 * h + 1] for h in range(N_HEADS)], axis=0)
    o_ref[...] = ot.T


def _attn_prompt(tr, grp, k_aug, vtb, which, x, fox):
    b, _, _, l = tr.shape
    nb = l // MOBA_BLOCK
    t = MOBA_BLOCK
    if fox:
        x_spec = pl.BlockSpec((None, N_HEADS, t), lambda bi, i: (bi, 0, i))
        scratch = []
    else:
        x_spec = pl.BlockSpec((None, N_HEADS, nb, HEAD_DIM), lambda bi, i: (bi, 0, 0, 0))
        scratch = [pltpu.VMEM((N_HEADS, nb, t), F32)]
    return pl.pallas_call(
        functools.partial(_attn_prompt_kernel, fox=fox),
        out_shape=jax.ShapeDtypeStruct((b * l, MIX), F32),
        grid=(b, nb),
        in_specs=[pl.BlockSpec((None, None, MIX, t), lambda bi, i: (bi, grp, 0, i)),
                  pl.BlockSpec((None, N_HEADS, l, 2 * HEAD_DIM), lambda bi, i: (bi, 0, 0, 0)),
                  pl.BlockSpec((None, None, MIX, l), lambda bi, i: (bi, which, 0, 0)),
                  x_spec],
        out_specs=pl.BlockSpec((t, MIX), lambda bi, i: (bi * nb + i, 0)),
        scratch_shapes=[pltpu.VMEM((N_HEADS, 2 * HEAD_DIM, t), BF16)] + scratch,
        compiler_params=_params("parallel", "arbitrary"),
        name="fox_prompt" if fox else "moba_prompt",
    )(tr, k_aug, vtb, x)


def _k_aug(k, b, l, c=None):
    kh = k.reshape(b, l, N_HEADS, HEAD_DIM).transpose(0, 2, 1, 3)
    if c is None:
        aug = jnp.zeros((b, N_HEADS, l, HEAD_DIM), F32)
    else:
        hi, mid, lo = _split3(c * LOG2E)
        one = jnp.ones_like(c)
        aug = jnp.pad(jnp.stack([one, one, one, -hi, -mid, -lo], axis=-1),
                      ((0, 0), (0, 0), (0, 0), (0, HEAD_DIM - 6)))
    return jnp.concatenate([kh, aug], axis=-1).astype(BF16)


def _gmlp_kernel(u_ref, v_ref, w_ref, b_ref, o_ref, *, c):
    v = v_ref[...]
    grp = lax.broadcasted_iota(jnp.int32, v.shape, 1) // HEAD_DIM
    row = lax.broadcasted_iota(jnp.int32, (c, c), 0)
    col = lax.broadcasted_iota(jnp.int32, (c, c), 1)
    w_cat = jnp.concatenate([jnp.where(col <= row, w_ref[g], 0.0) for g in range(N_HEADS)], axis=1)
    v_cat = jnp.concatenate([jnp.where(grp == g, v, 0.0) for g in range(N_HEADS)], axis=0)
    mixed = jnp.dot(w_cat.astype(BF16), v_cat.astype(BF16), preferred_element_type=F32)
    o_ref[...] = u_ref[...] * (mixed + b_ref[...])


def _gmlp(proj, u_col, v_col, ws, bs, c):
    n = proj.shape[0]
    bias = jnp.repeat(bs.T, HEAD_DIM, axis=1)
    col = lambda g: pl.BlockSpec((c, MIX), lambda i: (i, g))
    return pl.pallas_call(
        functools.partial(_gmlp_kernel, c=c),
        out_shape=jax.ShapeDtypeStruct((n, MIX), F32),
        grid=(n // c,),
        in_specs=[col(u_col), col(v_col), pl.BlockSpec((N_HEADS, c, c), lambda i: (0, 0, 0)),
                  pl.BlockSpec((c, MIX), lambda i: (0, 0))],
        out_specs=col(0),
        compiler_params=_params("parallel"),
        name="gmlp",
    )(proj, proj, ws, bias)


def _retention_kernel(q_ref, kt_ref, v_ref, g_ref, s0_ref, dm_ref, qd_ref, kd_ref, cd_ref, gn_ref,
                      o_ref, s_ref, state, *, c, cps):
    step = pl.program_id(2)

    @pl.when(step == 0)
    def _():
        state[...] = s0_ref[...]

    dmat = dm_ref[...]
    q_dec = qd_ref[...]
    k_dec = kd_ref[...]
    chunk_dec = cd_ref[...]
    gn = gn_ref[...]
    s = state[...]
    for ci in range(cps):
        rows = slice(ci * c, (ci + 1) * c)
        qc = q_ref[rows, :].astype(BF16)
        vc = v_ref[rows, :].astype(BF16)
        kt = kt_ref[:, rows]
        att = jnp.dot(qc, kt.astype(BF16), preferred_element_type=F32) * dmat
        o = (jnp.dot(att.astype(BF16), vc, preferred_element_type=F32)
             + jnp.dot(qc, s.astype(BF16), preferred_element_type=F32) * q_dec)
        s = s * chunk_dec + jnp.dot((kt * k_dec).astype(BF16), vc, preferred_element_type=F32)
        mu = jnp.mean(o, axis=-1, keepdims=True)
        oc = o - mu
        var = jnp.mean(oc * oc, axis=-1, keepdims=True)
        o_ref[rows, :] = jax.nn.silu(g_ref[rows, :]) * (oc * lax.rsqrt(var + EPS) * gn)
    state[...] = s

    @pl.when(step == pl.num_programs(2) - 1)
    def _():
        s_ref[...] = s


def _retention(q, kt, v, g, s0, gn_g, c, cps):
    b, h, l, dh = q.shape
    lg = jnp.log(1.0 - 2.0 ** (-5.0 - jnp.arange(h, dtype=F32)))
    j = jnp.arange(c, dtype=F32)
    diff = j[:, None] - j[None, :]
    dmat = jnp.where(diff[None] >= 0, jnp.exp(jnp.maximum(diff, 0.0)[None] * lg[:, None, None]), 0.0)
    q_dec = jnp.broadcast_to(jnp.exp((j[None, :] + 1.0) * lg[:, None])[:, :, None], (h, c, dh))
    k_dec = jnp.broadcast_to(jnp.exp((c - 1.0 - j)[None, :] * lg[:, None])[:, None, :], (h, dh, c))
    chunk_dec = jnp.broadcast_to(jnp.exp(c * lg)[:, None, None], (h, dh, dh))
    r = c * cps
    rows = pl.BlockSpec((None, None, r, dh), lambda bi, hi, i: (bi, hi, i, 0))
    per_bh = pl.BlockSpec((None, None, dh, dh), lambda bi, hi, i: (bi, hi, 0, 0))
    per_h = lambda s1, s2: pl.BlockSpec((None, s1, s2), lambda bi, hi, i: (hi, 0, 0))
    return pl.pallas_call(
        functools.partial(_retention_kernel, c=c, cps=cps),
        out_shape=(jax.ShapeDtypeStruct((b, h, l, dh), F32), jax.ShapeDtypeStruct((b, h, dh, dh), F32)),
        grid=(b, h, l // r),
        in_specs=[rows, pl.BlockSpec((None, None, dh, r), lambda bi, hi, i: (bi, hi, 0, i)), rows, rows,
                  per_bh, per_h(c, c), per_h(c, dh), per_h(dh, c), per_h(dh, dh), per_h(1, dh)],
        out_specs=(rows, per_bh),
        scratch_shapes=[pltpu.VMEM((dh, dh), F32)],
        compiler_params=_params("parallel", "parallel", "arbitrary"),
        name="retention",
    )(q, kt, v, g, s0, dmat, q_dec, k_dec, chunk_dec, gn_g.reshape(h, 1, dh))


def _merge_kernel(h_ref, g_ref, oa_ref, ob_ref, oc_ref, od_ref, wm_ref, wb_ref, wo_ref, o_ref):
    h = h_ref[...]
    d = h.shape[1]
    xn = _rms(h, g_ref[...]).astype(BF16)
    z = None
    for n, br_ref in enumerate((oa_ref, ob_ref, oc_ref, od_ref)):
        gate = jax.nn.sigmoid(jnp.dot(xn, wm_ref[:, n * d:(n + 1) * d], preferred_element_type=F32))
        br = jnp.dot(br_ref[...].astype(BF16), wb_ref[n], preferred_element_type=F32)
        z = gate * br if z is None else z + gate * br
    o_ref[...] = h + jnp.dot(z.astype(BF16), wo_ref[...], preferred_element_type=F32)


def _merge(h, g, branches, w_merge, w_branch, w_out, tm):
    n, d = h.shape
    rows = lambda w: pl.BlockSpec((tm, w), lambda i: (i, 0))
    return pl.pallas_call(
        _merge_kernel,
        out_shape=jax.ShapeDtypeStruct((n, d), F32),
        grid=(n // tm,),
        in_specs=[rows(d), pl.BlockSpec((1, d), lambda i: (0, 0)), rows(MIX), rows(MIX), rows(MIX), rows(MIX),
                  pl.BlockSpec(w_merge.shape, lambda i: (0, 0)),
                  pl.BlockSpec(w_branch.shape, lambda i: (0, 0, 0)),
                  pl.BlockSpec(w_out.shape, lambda i: (0, 0))],
        out_specs=rows(d),
        compiler_params=_params("parallel"),
        name="merge",
    )(h, g, *branches, w_merge, w_branch, w_out)


def _ffn_kernel(h_ref, g_ref, wg_ref, wu_ref, wo_ref, o_ref, xn_ref, acc_ref):
    k = pl.program_id(1)

    @pl.when(k == 0)
    def _():
        xn_ref[...] = _rms(h_ref[...], g_ref[...]).astype(BF16)
        acc_ref[...] = h_ref[...]

    xn = xn_ref[...]
    gate = jnp.dot(xn, wg_ref[...], preferred_element_type=F32)
    up = jnp.dot(xn, wu_ref[...], preferred_element_type=F32)
    act = (jax.nn.silu(gate) * up).astype(BF16)
    acc_ref[...] += jnp.dot(act, wo_ref[...], preferred_element_type=F32)

    @pl.when(k == pl.num_programs(1) - 1)
    def _():
        o_ref[...] = acc_ref[...]


def _ffn(h, g, w_in, w_out, tm, tf):
    n, d = h.shape
    d_ff = w_out.shape[0]
    nk = d_ff // tf
    return pl.pallas_call(
        _ffn_kernel,
        out_shape=jax.ShapeDtypeStruct((n, d), F32),
        grid=(n // tm, nk),
        in_specs=[pl.BlockSpec((tm, d), lambda i, k: (i, 0)), pl.BlockSpec((1, d), lambda i, k: (0, 0)),
                  pl.BlockSpec((d, tf), lambda i, k: (0, k)), pl.BlockSpec((d, tf), lambda i, k: (0, nk + k)),
                  pl.BlockSpec((tf, d), lambda i, k: (k, 0))],
        out_specs=pl.BlockSpec((tm, d), lambda i, k: (i, 0)),
        scratch_shapes=[pltpu.VMEM((tm, d), BF16), pltpu.VMEM((tm, d), F32)],
        compiler_params=_params("parallel", "arbitrary"),
        name="ffn",
    )(h, g, w_in, w_in, w_out)


def _ple_kernel(h_ref, g_ref, p_ref, wg_ref, wp_ref, gf_ref, o_ref, *, final):
    h = h_ref[...]
    xn = _rms(h, g_ref[...]).astype(BF16)
    gate = jax.nn.sigmoid(jnp.dot(xn, wg_ref[...], preferred_element_type=F32))
    emb = jnp.dot(p_ref[...].astype(BF16), wp_ref[...], preferred_element_type=F32)
    out = h + gate * emb
    o_ref[...] = _rms(out, gf_ref[...]) if final else out


def _ple(h, g, p, w_gate, w_proj, g_final, tm, final):
    n, d = h.shape
    dp = p.shape[1]
    vec = pl.BlockSpec((1, d), lambda i: (0, 0))
    return pl.pallas_call(
        functools.partial(_ple_kernel, final=final),
        out_shape=jax.ShapeDtypeStruct((n, d), F32),
        grid=(n // tm,),
        in_specs=[pl.BlockSpec((tm, d), lambda i: (i, 0)), vec, pl.BlockSpec((tm, dp), lambda i: (i, 0)),
                  pl.BlockSpec((d, d), lambda i: (0, 0)), pl.BlockSpec((dp, d), lambda i: (0, 0)), vec],
        out_specs=pl.BlockSpec((tm, d), lambda i: (i, 0)),
        compiler_params=_params("parallel"),
        name="ple",
    )(h, g, p, w_gate, w_proj, g_final)


def _page_specs(layer, block, pp):
    return [pl.BlockSpec((None, None) + block, lambda b, c, pt, k=k: (layer, pt[b, c * pp + k], 0, 0))
            for k in range(pp)]


def _kmean_sample_kernel(pt_ref, *refs, pp):
    pages, o_ref = refs[:pp], refs[pp]
    c = pl.program_id(1)
    per_block = MOBA_BLOCK // PAGE_SIZE
    nblk = pp // per_block

    @pl.when(c == 0)
    def _():
        o_ref[...] = jnp.zeros(o_ref.shape, F32)

    lane = lax.broadcasted_iota(jnp.int32, o_ref.shape, 1)
    acc = o_ref[...]
    for blk in range(nblk):
        tot = pages[blk * per_block][...]
        for k in range(1, per_block):
            tot = tot + pages[blk * per_block + k][...]
        mean = jnp.sum(tot, axis=1, keepdims=True) * (1.0 / MOBA_BLOCK)
        acc = jnp.where(lane == c * nblk + blk, mean, acc)
    o_ref[...] = acc


def _kmean_sample(cache_k, layer, page_table):
    db, n_pages = page_table.shape
    pp = PAGES_PER_STEP
    assert n_pages * PAGE_SIZE // MOBA_BLOCK <= LANES
    grid_spec = pltpu.PrefetchScalarGridSpec(
        num_scalar_prefetch=1, grid=(db, n_pages // pp),
        in_specs=_page_specs(layer, (MIX, PAGE_SIZE), pp),
        out_specs=pl.BlockSpec((None, MIX, LANES), lambda b, c, pt: (b, 0, 0)))
    return pl.pallas_call(
        functools.partial(_kmean_sample_kernel, pp=pp),
        out_shape=jax.ShapeDtypeStruct((db, MIX, LANES), F32),
        grid_spec=grid_spec,
        compiler_params=_params("parallel", "arbitrary"),
        name="kmean_sample",
    )(page_table, *([cache_k] * pp))


def _cumsum_sample_kernel(pt_ref, *refs, pp):
    pages, new_ref, o_ref, onew_ref, carry = refs[:pp], refs[pp], refs[pp + 1], refs[pp + 2], refs[pp + 3]
    c = pl.program_id(1)

    @pl.when(c == 0)
    def _():
        carry[...] = jnp.zeros(carry.shape, F32)

    x = jnp.concatenate([p[...] for p in pages], axis=1)
    cs = _lane_cumsum(x) + carry[:, :1]
    o_ref[...] = cs
    total = cs[:, pp * PAGE_SIZE - 1:]
    carry[...] = jnp.broadcast_to(total, carry.shape)

    @pl.when(c == pl.num_programs(1) - 1)
    def _():
        onew_ref[...] = _lane_cumsum(new_ref[...]) + total


def _cumsum_sample(cache_logf_t, layer, page_table, logf_new):
    db, n_pages = page_table.shape
    pp = PAGES_PER_STEP
    h = N_HEADS
    grid_spec = pltpu.PrefetchScalarGridSpec(
        num_scalar_prefetch=1, grid=(db, n_pages // pp),
        in_specs=_page_specs(layer, (h, PAGE_SIZE), pp)
        + [pl.BlockSpec((None, h, LANES), lambda b, c, pt: (b, 0, 0))],
        out_specs=(pl.BlockSpec((None, h, pp * PAGE_SIZE), lambda b, c, pt: (b, 0, c)),
                   pl.BlockSpec((None, h, LANES), lambda b, c, pt: (b, 0, 0))),
        scratch_shapes=[pltpu.VMEM((h, LANES), F32)])
    return pl.pallas_call(
        functools.partial(_cumsum_sample_kernel, pp=pp),
        out_shape=(jax.ShapeDtypeStruct((db, h, n_pages * PAGE_SIZE), F32),
                   jax.ShapeDtypeStruct((db, h, LANES), F32)),
        grid_spec=grid_spec,
        compiler_params=_params("parallel", "arbitrary"),
        name="cumsum_sample",
    )(page_table, *([cache_logf_t] * pp), logf_new)


def _expand_heads(x, t):
    return jnp.concatenate([jnp.broadcast_to(x[h:h + 1, :], (t, x.shape[1])) for h in range(N_HEADS)], axis=0)


def _decode_attn_kernel(pt_ref, *refs, pp, t, fox, nb):
    kp, vp = refs[:pp], refs[pp:2 * pp]
    q_ref, kn_ref, vn_ref = refs[2 * pp:2 * pp + 3]
    if fox:
        cq_ref, cn_ref, ck_ref = refs[2 * pp + 3:2 * pp + 6]
        o_ref, s_ref, m_ref, l_ref, acc_ref = refs[2 * pp + 6:]
    else:
        km_ref = refs[2 * pp + 3]
        o_ref, s_ref, m_ref, l_ref, acc_ref, sel_ref = refs[2 * pp + 4:]
    c = pl.program_id(1)
    r = N_HEADS * t
    q = q_ref[...]
    qb = (q * SCALE).astype(BF16)

    @pl.when(c == 0)
    def _():
        s = jnp.dot(qb, kn_ref[...].astype(BF16), preferred_element_type=F32)
        if fox:
            s = s + cq_ref[...] - _expand_heads(cn_ref[...], t)
        key = lax.broadcasted_iota(jnp.int32, (r, LANES), 1)
        qi = lax.broadcasted_iota(jnp.int32, (r, LANES), 0) % t
        s = jnp.where(key <= qi, s, NEG_INF)
        m = jnp.max(s, axis=1, keepdims=True)
        p = jnp.exp(s - m)
        m_ref[...] = jnp.broadcast_to(m, m_ref.shape)
        l_ref[...] = jnp.broadcast_to(jnp.sum(p, axis=1, keepdims=True), l_ref.shape)
        acc_ref[...] = _nt_dot(p.astype(BF16), vn_ref[...].astype(BF16))
        if not fox:
            gate = jnp.dot(q, km_ref[...], precision=lax.Precision.HIGHEST, preferred_element_type=F32)
            blk = lax.broadcasted_iota(jnp.int32, gate.shape, 1)
            sel_ref[...] = _topk_select(jnp.where(blk < nb, gate, NEG_INF), min(MOBA_TOPK, nb), 1)

    for k in range(pp):
        s_ref[:, k * PAGE_SIZE:(k + 1) * PAGE_SIZE] = jnp.dot(qb, kp[k][...].astype(BF16),
                                                              preferred_element_type=F32)
    s = s_ref[...]
    if fox:
        cq = jnp.concatenate([cq_ref[...]] * pp, axis=1)
        s = s + cq - _expand_heads(ck_ref[...], t)
    else:
        per_block = MOBA_BLOCK // PAGE_SIZE
        nblk = pp // per_block
        sel = sel_ref[...]
        blk = lax.broadcasted_iota(jnp.int32, sel.shape, 1)
        cols = []
        for b in range(nblk):
            chosen = jnp.sum(jnp.where(blk == c * nblk + b, sel, 0.0), axis=1, keepdims=True)
            cols.append(jnp.broadcast_to(chosen, (r, MOBA_BLOCK)))
        s = jnp.where(jnp.concatenate(cols, axis=1) > 0.5, s, NEG_INF)
    m = m_ref[:, :1]
    m_new = jnp.maximum(m, jnp.max(s, axis=1, keepdims=True))
    alpha = jnp.exp(m - m_new)
    p = jnp.exp(s - m_new).astype(BF16)
    l_new = alpha * l_ref[:, :1] + jnp.sum(p.astype(F32), axis=1, keepdims=True)
    acc = alpha * acc_ref[...]
    for k in range(pp):
        acc = acc + _nt_dot(p[:, k * PAGE_SIZE:(k + 1) * PAGE_SIZE], vp[k][...].astype(BF16))
    m_ref[...] = jnp.broadcast_to(m_new, m_ref.shape)
    l_ref[...] = jnp.broadcast_to(l_new, l_ref.shape)
    acc_ref[...] = acc

    @pl.when(c == pl.num_programs(1) - 1)
    def _():
        o = acc / l_new
        grp = lax.broadcasted_iota(jnp.int32, (t, MIX), 1) // HEAD_DIM
        out = jnp.zeros((t, MIX), F32)
        for h in range(N_HEADS):
            out = out + jnp.where(grp == h, o[h * t:(h + 1) * t, :], 0.0)
        o_ref[...] = out


def _decode_attn(cache_k, cache_v, layer, page_table, q_blk, k_new, v_new, extra, fox):
    db, n_pages = page_table.shape
    pp = PAGES_PER_STEP
    r = q_blk.shape[1]
    t = r // N_HEADS
    per_b = lambda s1, s2: pl.BlockSpec((None, s1, s2), lambda b, c, pt: (b, 0, 0))
    in_specs = (_page_specs(layer, (MIX, PAGE_SIZE), pp) + _page_specs(layer, (MIX, PAGE_SIZE), pp)
                + [per_b(r, MIX), per_b(MIX, LANES), per_b(MIX, LANES)])
    scratch = [pltpu.VMEM((r, pp * PAGE_SIZE), F32), pltpu.VMEM((r, LANES), F32), pltpu.VMEM((r, LANES), F32),
               pltpu.VMEM((r, MIX), F32)]
    if fox:
        cq, c_new, c_past = extra
        in_specs += [per_b(r, LANES), per_b(N_HEADS, LANES),
                     pl.BlockSpec((None, N_HEADS, pp * PAGE_SIZE), lambda b, c, pt: (b, 0, c))]
    else:
        in_specs += [per_b(MIX, LANES)]
        scratch += [pltpu.VMEM((r, LANES), F32)]
    grid_spec = pltpu.PrefetchScalarGridSpec(
        num_scalar_prefetch=1, grid=(db, n_pages // pp), in_specs=in_specs,
        out_specs=pl.BlockSpec((None, t, MIX), lambda b, c, pt: (b, 0, 0)),
        scratch_shapes=scratch)
    return pl.pallas_call(
        functools.partial(_decode_attn_kernel, pp=pp, t=t, fox=fox, nb=n_pages * PAGE_SIZE // MOBA_BLOCK),
        out_shape=jax.ShapeDtypeStruct((db, t, MIX), F32),
        grid_spec=grid_spec,
        compiler_params=_params("parallel", "arbitrary"),
        name="fox_sample" if fox else "moba_sample",
    )(page_table, *([cache_k] * pp), *([cache_v] * pp), q_blk, k_new, v_new, *extra)


def _to_heads(x, b, l):
    return x.reshape(b, l, N_HEADS, HEAD_DIM).transpose(0, 2, 1, 3)


def _from_heads(x):
    b, h, l, dh = x.shape
    return x.transpose(0, 2, 1, 3).reshape(b * l, h * dh)


def _split_proj(proj):
    cols = [proj[:, g * MIX:(g + 1) * MIX] for g in range(N_PROJ_GROUPS)]
    logf = proj[:, N_PROJ_GROUPS * MIX:N_PROJ_GROUPS * MIX + N_HEADS]
    return cols, logf


def _tail(h, p, prm, branches, final, tm):
    h = _merge(h, prm["norm_mix"], branches, prm["w_merge"], prm["w_branch"], prm["w_out"], tm)
    d_ff = prm["w_ffn_out"].shape[0]
    tf = d_ff // 2 if (d_ff // 2) % LANES == 0 else d_ff
    h = _ffn(h, prm["norm_ffn"], prm["w_ffn_in"], prm["w_ffn_out"], min(512, h.shape[0]), tf)
    return _ple(h, prm["norm_ple"], p, prm["w_ple_gate"], prm["w_ple_proj"], prm["norm_final"], tm, final)


def _prompt_layer(h, p, prm, tabs, b, l, final):
    tm = 256
    proj, tr, vtb = _proj(h, prm["norm_mix"], prm["w_in"], prm["b_forget"], prm["ln_g"], prm["ln_b"], tabs, tm,
                          batch=b)
    (_, ka, _, _, kb, _, _, _, qd, kd, vd, gd), logf = _split_proj(proj)
    hd = lambda x: _to_heads(x, b, l)
    kmean = _kmean_prompt(proj, 1, b, l).reshape(b, -1, N_HEADS, HEAD_DIM).transpose(0, 2, 1, 3)
    oa = _attn_prompt(tr, 0, _k_aug(ka, b, l), vtb, 0, kmean, fox=False)
    c = _cumsum_prompt(logf.reshape(b, l, N_HEADS).transpose(0, 2, 1).reshape(b * N_HEADS, l))
    c = c.reshape(b, N_HEADS, l)
    ob = _attn_prompt(tr, 3, _k_aug(kb, b, l, c), vtb, 1, c, fox=True)
    oc = _gmlp(proj, 6, 7, prm["gmlp_ws"], prm["gmlp_bs"], GMLP_CHUNK)
    s0 = jnp.zeros((b, N_HEADS, HEAD_DIM, HEAD_DIM), F32)
    cps = 8 if l % (8 * RET_CHUNK) == 0 else 1
    od, s_new = _retention(hd(qd), hd(kd).transpose(0, 1, 3, 2), hd(vd), hd(gd), s0, prm["ret_gn_g"],
                           RET_CHUNK, cps)
    h = _tail(h, p, prm, (oa, ob, oc, _from_heads(od)), final, tm)
    heads = lambda g: tr[:, g].reshape(b, N_HEADS, HEAD_DIM, l).transpose(0, 3, 1, 2)
    rows = (heads(1), heads(2), heads(4), heads(5), logf.reshape(b, l, N_HEADS), s_new)
    return h, rows


def _block_diag_q(q, db, t):
    grp = jnp.arange(MIX) // HEAD_DIM
    mask = (grp[None, None, :] == jnp.arange(N_HEADS)[:, None, None]).astype(F32)
    return (q.reshape(db, 1, t, MIX) * mask[None]).reshape(db, N_HEADS * t, MIX)


def _new_keys_t(x, db, t):
    return jnp.pad(x.reshape(db, t, MIX).transpose(0, 2, 1), ((0, 0), (0, 0), (0, LANES - t)))


def _sample_layer(h, p, prm, tabs, caches, layer, page_table, db, t, final):
    cache_k_a, cache_v_a, cache_k_b, cache_v_b, cache_logf_t, state = caches
    n = db * t
    (proj,) = _proj(h, prm["norm_mix"], prm["w_in"], prm["b_forget"], prm["ln_g"], prm["ln_b"], tabs, n)
    (qa, ka, va, qb, kb, vb, _, vn, qd, kd, vd, gd), logf = _split_proj(proj)
    kmean = _kmean_sample(cache_k_a, layer, page_table)
    oa = _decode_attn(cache_k_a, cache_v_a, layer, page_table, _block_diag_q(qa, db, t),
                      _new_keys_t(ka, db, t), _new_keys_t(va, db, t), (kmean,), fox=False)
    logf_new = jnp.pad(logf.reshape(db, t, N_HEADS).transpose(0, 2, 1), ((0, 0), (0, 0), (0, LANES - t)))
    c_past, c_new = _cumsum_sample(cache_logf_t, layer, page_table, logf_new)
    cq = jnp.broadcast_to(c_new[:, :, :t].reshape(db, N_HEADS * t, 1), (db, N_HEADS * t, LANES))
    ob = _decode_attn(cache_k_b, cache_v_b, layer, page_table, _block_diag_q(qb, db, t),
                      _new_keys_t(kb, db, t), _new_keys_t(vb, db, t), (cq, c_new, c_past), fox=True)
    oc = _gmlp(proj, 6, 7, prm["gmlp_ws"][:, :t, :t], prm["gmlp_bs"][:, :t], t)
    hd = lambda x: _to_heads(x, db, t)
    od, s_new = _retention(hd(qd), hd(kd).transpose(0, 1, 3, 2), hd(vd), hd(gd), state[layer],
                           prm["ret_gn_g"], t, 1)
    h = _tail(h, p, prm, (oa.reshape(n, MIX), ob.reshape(n, MIX), oc, _from_heads(od)), final, n)
    rows = (ka.reshape(db, t, N_HEADS, HEAD_DIM), va.reshape(db, t, N_HEADS, HEAD_DIM),
            kb.reshape(db, t, N_HEADS, HEAD_DIM), vb.reshape(db, t, N_HEADS, HEAD_DIM),
            logf.reshape(db, t, N_HEADS), s_new, vn.reshape(db, t, MIX))
    return h, rows


def kernel(x_prompt, x_sample, cache_k_a, cache_v_a, cache_k_b, cache_v_b, cache_logf_b, state_ret, page_table, p_prompt, p_sample, norm_mix, w_in, b_forget, gmlp_ln_g, gmlp_ln_b, gmlp_ws, gmlp_bs, ret_gn_g, w_branch, w_merge, w_out, norm_ffn, w_ffn_in, w_ffn_out, norm_ple, w_ple_gate, w_ple_proj, norm_final):
    b, l, d = x_prompt.shape
    db, t, _ = x_sample.shape
    depth = w_in.shape[0]
    n_pages = page_table.shape[1]
    past_len = n_pages * PAGE_SIZE
    assert d == N_HEADS * MIX and l % MOBA_BLOCK == 0
    assert past_len % MOBA_BLOCK == 0 and t <= LANES and t % 8 == 0 and t < MOBA_BLOCK
    assert n_pages % PAGES_PER_STEP == 0

    pos_p = jnp.arange(l, dtype=jnp.int32)
    pos_s = jnp.tile(past_len + jnp.arange(t, dtype=jnp.int32), db)
    tabs_p = _rope_tables(pos_p, ROT_DIMS, ROPE_THETA) + _rope_tables(pos_p, HEAD_DIM, RET_THETA)
    tabs_s = _rope_tables(pos_s, ROT_DIMS, ROPE_THETA) + _rope_tables(pos_s, HEAD_DIM, RET_THETA)

    pool = cache_k_a.shape[1]
    pages = lambda x: x.transpose(0, 1, 3, 4, 2).reshape(depth, pool, MIX, PAGE_SIZE)
    caches = (pages(cache_k_a), pages(cache_v_a), pages(cache_k_b), pages(cache_v_b),
              cache_logf_b.transpose(0, 1, 3, 2), state_ret)

    hp = x_prompt.reshape(b * l, d)
    hs = x_sample.reshape(db * t, d)
    rows_p, rows_s = [], []
    for i in range(depth):
        w_split = w_in[i]
        w_main = jnp.concatenate([w_split[:, :6 * MIX], w_split[:, 6 * MIX + N_HEADS:]], axis=1)
        w_f = jnp.pad(w_split[:, 6 * MIX:6 * MIX + N_HEADS], ((0, 0), (0, LANES - N_HEADS)))
        prm = dict(
            norm_mix=norm_mix[i][None], w_in=jnp.concatenate([w_main, w_f], axis=1).astype(BF16),
            b_forget=jnp.pad(b_forget[i], (0, LANES - N_HEADS))[None],
            ln_g=gmlp_ln_g[i][None], ln_b=gmlp_ln_b[i][None], gmlp_ws=gmlp_ws[i], gmlp_bs=gmlp_bs[i],
            ret_gn_g=ret_gn_g[i], w_branch=w_branch[i].astype(BF16), w_merge=w_merge[i].astype(BF16),
            w_out=w_out[i].astype(BF16), norm_ffn=norm_ffn[i][None], w_ffn_in=w_ffn_in[i].astype(BF16),
            w_ffn_out=w_ffn_out[i].astype(BF16), norm_ple=norm_ple[i][None],
            w_ple_gate=w_ple_gate[i].astype(BF16), w_ple_proj=w_ple_proj[i].astype(BF16),
            norm_final=norm_final[None])
        final = i == depth - 1
        hp, rp = _prompt_layer(hp, p_prompt[i].reshape(b * l, -1), prm, tabs_p, b, l, final)
        hs, rs = _sample_layer(hs, p_sample[i].reshape(db * t, -1), prm, tabs_s, caches, i, page_table,
                               db, t, final)
        rows_p.append(rp)
        rows_s.append(rs)
    stack = lambda rows, j: jnp.stack([r[j] for r in rows], axis=0)
    return ((hp.reshape(b, l, d), hs.reshape(db, t, d))
            + tuple(stack(rows_p, j) for j in range(6)) + tuple(stack(rows_s, j) for j in range(7)))
```

```python
import functools

import jax
import jax.numpy as jnp
from jax import lax
from jax.experimental import pallas as pl
from jax.experimental.pallas import tpu as pltpu

F32 = jnp.float32
BF16 = jnp.bfloat16

N_HEADS = 4
HEAD_DIM = 64
MIX = N_HEADS * HEAD_DIM
ROT_DIMS = HEAD_DIM // 4
ROPE_THETA = 500000.0
RET_THETA = 10000.0
MOBA_BLOCK = 256
MOBA_TOPK = 3
GMLP_CHUNK = 128
RET_CHUNK = 128
PAGE_SIZE = 128
EPS = 1e-6
SCALE = HEAD_DIM ** -0.5
NEG_INF = float("-inf")

LANES = 128
VMEM_LIMIT = 56 * 1024 * 1024
PAGES_PER_STEP = 16
N_PROJ_GROUPS = 12
N_ATTN_GROUPS = 6
W_IN_PAD = N_PROJ_GROUPS * MIX + LANES
LOG2E = 1.4426950408889634
AUG = 8


def _params(*sem):
    return pltpu.CompilerParams(dimension_semantics=sem, vmem_limit_bytes=VMEM_LIMIT)


def _rms(x, g):
    return x * lax.rsqrt(jnp.mean(x * x, axis=-1, keepdims=True) + EPS) * g


def _nt_dot(a, b, precision=None):
    return lax.dot_general(a, b, (((1,), (1,)), ((), ())), precision=precision,
                           preferred_element_type=F32)


def _log_sigmoid(x):
    return jnp.minimum(x, 0.0) - jnp.log(1.0 + jnp.exp(-jnp.abs(x)))


def _proj_kernel(h_ref, g_ref, w_ref, bf_ref, lng_ref, lnb_ref, ca_ref, sa_ref, cd_ref, sd_ref, o_ref,
                 *t_refs):
    xn = _rms(h_ref[...], g_ref[...]).astype(BF16)
    lane = lax.broadcasted_iota(jnp.int32, (1, MIX), 1) & (HEAD_DIM - 1)

    def rope(y, c_ref, s_ref, half):
        c = jnp.concatenate([c_ref[...], c_ref[...]], axis=1)
        s = jnp.concatenate([s_ref[...], s_ref[...]], axis=1)
        rot = jnp.where(lane < half, pltpu.roll(y, MIX - half, 1), pltpu.roll(y, half, 1))
        return y * c + rot * s

    for grp in range(N_PROJ_GROUPS):
        y = jnp.dot(xn, w_ref[:, grp * MIX:(grp + 1) * MIX], preferred_element_type=F32)
        if grp in (0, 1):
            y = rope(y, ca_ref, sa_ref, ROT_DIMS // 2)
        elif grp == 6:
            y = jax.nn.gelu(y)
        elif grp == 7:
            y = jax.nn.gelu(y)
            mu = jnp.mean(y, axis=-1, keepdims=True)
            yc = y - mu
            var = jnp.mean(yc * yc, axis=-1, keepdims=True)
            y = yc * lax.rsqrt(var + EPS) * lng_ref[...] + lnb_ref[...]
        elif grp == 8:
            y = rope(y, cd_ref, sd_ref, HEAD_DIM // 2)
        elif grp == 9:
            y = rope(y, cd_ref, sd_ref, HEAD_DIM // 2) * SCALE
        o_ref[:, grp * MIX:(grp + 1) * MIX] = y
        if t_refs and grp < N_ATTN_GROUPS:
            yt = y.T
            t_refs[0][grp] = yt
            if grp % 3 == 2:
                t_refs[1][grp // 3] = yt.astype(BF16)
    f = jnp.dot(xn, w_ref[:, N_PROJ_GROUPS * MIX:], preferred_element_type=F32) + bf_ref[...]
    o_ref[:, N_PROJ_GROUPS * MIX:] = _log_sigmoid(f)


def _proj(h, g, w_pad, bf_pad, ln_g, ln_b, tabs, tm, batch=None):
    n, d = h.shape
    ca, sa, cd, sd = tabs
    tab_blocks = ca.shape[0] // tm
    full = lambda shape: pl.BlockSpec(shape, lambda i: (0, 0))
    tab = pl.BlockSpec((tm, LANES), lambda i: (i % tab_blocks, 0))
    out_shape = [jax.ShapeDtypeStruct((n, W_IN_PAD), F32)]
    out_specs = [pl.BlockSpec((tm, W_IN_PAD), lambda i: (i, 0))]
    if batch is not None:
        l = n // batch
        per_seq = l // tm
        t_map = lambda i: (i // per_seq, 0, 0, i % per_seq)
        out_shape += [jax.ShapeDtypeStruct((batch, N_ATTN_GROUPS, MIX, l), F32),
                      jax.ShapeDtypeStruct((batch, N_ATTN_GROUPS // 3, MIX, l), BF16)]
        out_specs += [pl.BlockSpec((None, N_ATTN_GROUPS, MIX, tm), t_map),
                      pl.BlockSpec((None, N_ATTN_GROUPS // 3, MIX, tm), t_map)]
    return pl.pallas_call(
        _proj_kernel,
        out_shape=out_shape,
        grid=(n // tm,),
        in_specs=[pl.BlockSpec((tm, d), lambda i: (i, 0)), full((1, d)), full((d, W_IN_PAD)),
                  full((1, LANES)), full((1, MIX)), full((1, MIX)), tab, tab, tab, tab],
        out_specs=out_specs,
        compiler_params=_params("parallel"),
        name="proj",
    )(h, g, w_pad, bf_pad, ln_g, ln_b, ca, sa, cd, sd)


def _rope_tables(pos, n_rot, theta):
    half = n_rot // 2
    inv = theta ** (-jnp.arange(half, dtype=F32) / half)
    ang = pos.astype(F32)[:, None] * inv[None, :]
    cos, sin = jnp.cos(ang), jnp.sin(ang)
    rest = HEAD_DIM - n_rot
    ones = jnp.ones((pos.shape[0], rest), F32)
    zeros = jnp.zeros((pos.shape[0], rest), F32)
    c = jnp.concatenate([cos, cos, ones], axis=1)
    s = jnp.concatenate([-sin, sin, zeros], axis=1)
    reps = LANES // HEAD_DIM
    return jnp.tile(c, (1, reps)), jnp.tile(s, (1, reps))


def _kmean_kernel(k_ref, o_ref, *, nblk):
    x = k_ref[...].reshape(nblk, MOBA_BLOCK, MIX)
    o_ref[...] = jnp.sum(x, axis=1) * (1.0 / MOBA_BLOCK)


def _kmean_prompt(proj, col, b, l):
    nblk = 8
    assert l % (nblk * MOBA_BLOCK) == 0
    per_seq = l // (nblk * MOBA_BLOCK)
    return pl.pallas_call(
        functools.partial(_kmean_kernel, nblk=nblk),
        out_shape=jax.ShapeDtypeStruct((b, l // MOBA_BLOCK, MIX), F32),
        grid=(b, per_seq),
        in_specs=[pl.BlockSpec((nblk * MOBA_BLOCK, MIX), lambda i, j: (i * per_seq + j, col))],
        out_specs=pl.BlockSpec((None, nblk, MIX), lambda i, j: (i, j, 0)),
        compiler_params=_params("parallel", "parallel"),
        name="kmean_prompt",
    )(proj)


def _lane_cumsum(x):
    n = x.shape[-1]
    lane = lax.broadcasted_iota(jnp.int32, x.shape, x.ndim - 1)
    s = 1
    while s < n:
        x = x + jnp.where(lane >= s, pltpu.roll(x, s, x.ndim - 1), 0.0)
        s *= 2
    return x


def _cumsum_kernel(x_ref, o_ref):
    o_ref[...] = _lane_cumsum(x_ref[...])


def _cumsum_prompt(logf_t):
    r, l = logf_t.shape
    return pl.pallas_call(
        _cumsum_kernel,
        out_shape=jax.ShapeDtypeStruct((r, l), F32),
        grid=(1,),
        in_specs=[pl.BlockSpec((r, l), lambda i: (0, 0))],
        out_specs=pl.BlockSpec((r, l), lambda i: (0, 0)),
        compiler_params=_params("arbitrary"),
        name="cumsum_prompt",
    )(logf_t)


def _softmax_first(s, v):
    m = jnp.max(s, axis=1, keepdims=True)
    p = jnp.exp(s - m)
    l = jnp.sum(p, axis=1, keepdims=True)
    acc = jnp.dot(p.astype(BF16), v, preferred_element_type=F32)
    return m, l, acc


def _softmax_update(carry, s, v):
    m, l, acc = carry
    m_new = jnp.maximum(m, jnp.max(s, axis=1, keepdims=True))
    alpha = jnp.exp(m - m_new)
    p = jnp.exp(s - m_new)
    l = alpha * l + jnp.sum(p, axis=1, keepdims=True)
    acc = alpha * acc + jnp.dot(p.astype(BF16), v, preferred_element_type=F32)
    return m_new, l, acc


def _topk_select(gate, n_top, axis):
    pos = lax.broadcasted_iota(jnp.int32, gate.shape, axis)
    sel = jnp.zeros(gate.shape, F32)
    g = gate
    for _ in range(n_top):
        mx = jnp.max(g, axis=axis, keepdims=True)
        idx = jnp.min(jnp.where(g == mx, pos, gate.shape[axis]), axis=axis, keepdims=True)
        hit = pos == idx
        sel = jnp.maximum(sel, jnp.where(hit, jnp.where(mx > NEG_INF, 1.0, 0.0), 0.0))
        g = jnp.where(hit, NEG_INF, g)
    return sel


def _split3(x):
    hi = x.astype(BF16).astype(F32)
    r = x - hi
    mid = r.astype(BF16).astype(F32)
    return hi, mid, r - mid


def _attn_prompt_kernel(qt_ref, kt_ref, vt_ref, x_ref, o_ref, k_ref, qa_ref, *sel_refs, fox):
    t = MOBA_BLOCK
    i = pl.program_id(1)
    own = pl.multiple_of(i * t, t)
    causal = (lax.broadcasted_iota(jnp.int32, (t, t), 0) <= lax.broadcasted_iota(jnp.int32, (t, t), 1))
    pad = jnp.zeros((HEAD_DIM - AUG, t), F32)
    one = jnp.ones((1, t), F32)
    zero = jnp.zeros((1, t), F32)
    heads = range(N_HEADS)

    @pl.when(i == 0)
    def _():
        def build(jt, _):
            st = pl.multiple_of(jt * t, t)
            for h in heads:
                if fox:
                    hi, mid, lo = _split3(x_ref[h:h + 1, pl.ds(st, t)] * LOG2E)
                    aug = jnp.concatenate([one, one, one, -hi, -mid, -lo, zero, zero], axis=0)
                else:
                    aug = jnp.zeros((AUG, t), F32)
                kh = jnp.concatenate([kt_ref[h * HEAD_DIM:(h + 1) * HEAD_DIM, pl.ds(st, t)], aug, pad], axis=0)
                k_ref[h, pl.ds(st, t), :] = kh.T.astype(BF16)
            return 0

        lax.fori_loop(0, kt_ref.shape[1] // t, build, 0)

    for h in heads:
        qh = qt_ref[h * HEAD_DIM:(h + 1) * HEAD_DIM, :]
        if fox:
            aug = jnp.concatenate(_split3(x_ref[h:h + 1, pl.ds(own, t)] * LOG2E) + (one, one, one, zero, zero),
                                  axis=0)
        else:
            aug = jnp.zeros((AUG, t), F32)
            nb = x_ref.shape[1]
            gate = jnp.dot(x_ref[h], qh, precision=lax.Precision.HIGHEST, preferred_element_type=F32)
            blk = lax.broadcasted_iota(jnp.int32, (nb, t), 0)
            sel_refs[0][h] = _topk_select(jnp.where(blk < i, gate, NEG_INF), min(MOBA_TOPK, nb), 0)
        qa_ref[h] = jnp.concatenate([qh * (SCALE * LOG2E), aug, pad], axis=0).astype(BF16)

    ss = [jnp.dot(k_ref[h, pl.ds(own, t), :], qa_ref[h], preferred_element_type=F32) for h in heads]
    ss = [jnp.where(causal, s, NEG_INF) for s in ss]
    ms = [jnp.max(s, axis=0, keepdims=True) for s in ss]
    ps = [jnp.exp2(ss[h] - ms[h]) for h in heads]
    pvs = [jnp.dot(vt_ref[h * HEAD_DIM:(h + 1) * HEAD_DIM, pl.ds(own, t)], ps[h].astype(BF16),
                   preferred_element_type=F32) for h in heads]
    carry = []
    for h in heads:
        carry += [ms[h], jnp.sum(ps[h], axis=0, keepdims=True), pvs[h]]

    def body(j, carry):
        start = pl.multiple_of(j * t, t)
        ss =[jnp.dot(k_ref[h, pl.ds(start, t), :], qa_ref[h], preferred_element_type=F32) for h in heads]
        if not fox:
            ss = [jnp.where(sel_refs[0][h, pl.ds(j, 1), :] > 0.5, ss[h], NEG_INF) for h in heads]
        ms = [jnp.maximum(carry[3 * h], jnp.max(ss[h], axis=0, keepdims=True)) for h in heads]
        ps = [jnp.exp2(ss[h] - ms[h]) for h in heads]
        pvs = [jnp.dot(vt_ref[h * HEAD_DIM:(h + 1) * HEAD_DIM, pl.ds(start, t)], ps[h].astype(BF16),
                       preferred_element_type=F32) for h in heads]
        out = []
        for h in heads:
            alpha = jnp.exp2(carry[3 * h] - ms[h])
            l = alpha * carry[3 * h + 1] + jnp.sum(ps[h], axis=0, keepdims=True)
            out += [ms[h], l, alpha * carry[3 * h + 2] + pvs[h]]
        return tuple(out)

    carry = lax.fori_loop(0, i, body, tuple(carry))
    ot = jnp.concatenate([carry[3 * h + 2] / carry[3 * h + 1] for h in range(N_HEADS)], axis=0)
    o_ref[...] = ot.T


def _attn_prompt(tr, grp, vtb, which, x, fox):
    b, _, _, l = tr.shape
    nb = l // MOBA_BLOCK
    t = MOBA_BLOCK
    if fox:
        x_spec = pl.BlockSpec((None, N_HEADS, l), lambda bi, i: (bi, 0, 0))
        scratch = []
    else:
        x_spec = pl.BlockSpec((None, N_HEADS, nb, HEAD_DIM), lambda bi, i: (bi, 0, 0, 0))
        scratch = [pltpu.VMEM((N_HEADS, nb, t), F32)]
    return pl.pallas_call(
        functools.partial(_attn_prompt_kernel, fox=fox),
        out_shape=jax.ShapeDtypeStruct((b * l, MIX), F32),
        grid=(b, nb),
        in_specs=[pl.BlockSpec((None, None, MIX, t), lambda bi, i: (bi, grp, 0, i)),
                  pl.BlockSpec((None, None, MIX, l), lambda bi, i: (bi, grp + 1, 0, 0)),
                  pl.BlockSpec((None, None, MIX, l), lambda bi, i: (bi, which, 0, 0)),
                  x_spec],
        out_specs=pl.BlockSpec((t, MIX), lambda bi, i: (bi * nb + i, 0)),
        scratch_shapes=[pltpu.VMEM((N_HEADS, l, 2 * HEAD_DIM), BF16),
                        pltpu.VMEM((N_HEADS, 2 * HEAD_DIM, t), BF16)] + scratch,
        compiler_params=_params("parallel", "arbitrary"),
        name="fox_prompt" if fox else "moba_prompt",
    )(tr, tr, vtb, x)


def _gmlp_kernel(u_ref, v_ref, w_ref, b_ref, o_ref, *, c):
    v = v_ref[...]
    grp = lax.broadcasted_iota(jnp.int32, v.shape, 1) // HEAD_DIM
    row = lax.broadcasted_iota(jnp.int32, (c, c), 0)
    col = lax.broadcasted_iota(jnp.int32, (c, c), 1)
    w_cat = jnp.concatenate([jnp.where(col <= row, w_ref[g], 0.0) for g in range(N_HEADS)], axis=1)
    v_cat = jnp.concatenate([jnp.where(grp == g, v, 0.0) for g in range(N_HEADS)], axis=0)
    mixed = jnp.dot(w_cat.astype(BF16), v_cat.astype(BF16), preferred_element_type=F32)
    o_ref[...] = u_ref[...] * (mixed + b_ref[...])


def _gmlp(proj, u_col, v_col, ws, bs, c):
    n = proj.shape[0]
    bias = jnp.repeat(bs.T, HEAD_DIM, axis=1)
    col = lambda g: pl.BlockSpec((c, MIX), lambda i: (i, g))
    return pl.pallas_call(
        functools.partial(_gmlp_kernel, c=c),
        out_shape=jax.ShapeDtypeStruct((n, MIX), F32),
        grid=(n // c,),
        in_specs=[col(u_col), col(v_col), pl.BlockSpec((N_HEADS, c, c), lambda i: (0, 0, 0)),
                  pl.BlockSpec((c, MIX), lambda i: (0, 0))],
        out_specs=col(0),
        compiler_params=_params("parallel"),
        name="gmlp",
    )(proj, proj, ws, bias)


def _retention_kernel(q_ref, kt_ref, v_ref, g_ref, s0_ref, dm_ref, qd_ref, kd_ref, cd_ref, gn_ref,
                      o_ref, s_ref, state, *, c, cps):
    step = pl.program_id(2)

    @pl.when(step == 0)
    def _():
        state[...] = s0_ref[...]

    dmat = dm_ref[...]
    q_dec = qd_ref[...]
    k_dec = kd_ref[...]
    chunk_dec = cd_ref[...]
    gn = gn_ref[...]
    s = state[...]
    for ci in range(cps):
        rows = slice(ci * c, (ci + 1) * c)
        qc = q_ref[rows, :].astype(BF16)
        vc = v_ref[rows, :].astype(BF16)
        kt = kt_ref[:, rows]
        att = jnp.dot(qc, kt.astype(BF16), preferred_element_type=F32) * dmat
        o = (jnp.dot(att.astype(BF16), vc, preferred_element_type=F32)
             + jnp.dot(qc, s.astype(BF16), preferred_element_type=F32) * q_dec)
        s = s * chunk_dec + jnp.dot((kt * k_dec).astype(BF16), vc, preferred_element_type=F32)
        mu = jnp.mean(o, axis=-1, keepdims=True)
        oc = o - mu
        var = jnp.mean(oc * oc, axis=-1, keepdims=True)
        o_ref[rows, :] = jax.nn.silu(g_ref[rows, :]) * (oc * lax.rsqrt(var + EPS) * gn)
    state[...] = s

    @pl.when(step == pl.num_programs(2) - 1)
    def _():
        s_ref[...] = s


def _retention(q, kt, v, g, s0, gn_g, c, cps):
    b, h, l, dh = q.shape
    lg = jnp.log(1.0 - 2.0 ** (-5.0 - jnp.arange(h, dtype=F32)))
    j = jnp.arange(c, dtype=F32)
    diff = j[:, None] - j[None, :]
    dmat = jnp.where(diff[None] >= 0, jnp.exp(jnp.maximum(diff, 0.0)[None] * lg[:, None, None]), 0.0)
    q_dec = jnp.broadcast_to(jnp.exp((j[None, :] + 1.0) * lg[:, None])[:, :, None], (h, c, dh))
    k_dec = jnp.broadcast_to(jnp.exp((c - 1.0 - j)[None, :] * lg[:, None])[:, None, :], (h, dh, c))
    chunk_dec = jnp.broadcast_to(jnp.exp(c * lg)[:, None, None], (h, dh, dh))
    r = c * cps
    rows = pl.BlockSpec((None, None, r, dh), lambda bi, hi, i: (bi, hi, i, 0))
    per_bh = pl.BlockSpec((None, None, dh, dh), lambda bi, hi, i: (bi, hi, 0, 0))
    per_h = lambda s1, s2: pl.BlockSpec((None, s1, s2), lambda bi, hi, i: (hi, 0, 0))
    return pl.pallas_call(
        functools.partial(_retention_kernel, c=c, cps=cps),
        out_shape=(jax.ShapeDtypeStruct((b, h, l, dh), F32), jax.ShapeDtypeStruct((b, h, dh, dh), F32)),
        grid=(b, h, l // r),
        in_specs=[rows, pl.BlockSpec((None, None, dh, r), lambda bi, hi, i: (bi, hi, 0, i)), rows, rows,
                  per_bh, per_h(c, c), per_h(c, dh), per_h(dh, c), per_h(dh, dh), per_h(1, dh)],
        out_specs=(rows, per_bh),
        scratch_shapes=[pltpu.VMEM((dh, dh), F32)],
        compiler_params=_params("parallel", "parallel", "arbitrary"),
        name="retention",
    )(q, kt, v, g, s0, dmat, q_dec, k_dec, chunk_dec, gn_g.reshape(h, 1, dh))


def _retention_rows_kernel(q_ref, k_ref, v_ref, g_ref, s0_ref, dm_ref, qd_ref, kd_ref, cd_ref, bd_ref, gn_ref,
                           o_ref, s_ref, state, *, c, cps):
    step = pl.program_id(1)

    @pl.when(step == 0)
    def _():
        state[...] = s0_ref[...]

    heads = range(N_HEADS)
    grp = lax.broadcasted_iota(jnp.int32, (c, MIX), 1) // HEAD_DIM
    bd = bd_ref[...]
    avg = (bd * (1.0 / HEAD_DIM)).astype(BF16)
    s = state[...]
    for ci in range(cps):
        rows = slice(ci * c, (ci + 1) * c)
        q, k, v = q_ref[rows, :], k_ref[rows, :], v_ref[rows, :]
        kb, vb = k.astype(BF16), v.astype(BF16)
        att = jnp.concatenate([_nt_dot(jnp.where(grp == h, q, 0.0).astype(BF16), kb) * dm_ref[h] for h in heads],
                              axis=1)
        v_cat = jnp.concatenate([jnp.where(grp == h, v, 0.0).astype(BF16) for h in heads], axis=0)
        o = (jnp.dot(att.astype(BF16), v_cat, preferred_element_type=F32)
             + jnp.dot(q.astype(BF16), s.astype(BF16), preferred_element_type=F32) * qd_ref[...])
        s = s * cd_ref[...] + bd * jnp.dot((k * kd_ref[...]).T.astype(BF16), vb, preferred_element_type=F32)
        mu = jnp.dot(o.astype(BF16), avg, preferred_element_type=F32)
        oc = o - mu
        var = jnp.dot((oc * oc).astype(BF16), avg, preferred_element_type=F32)
        o_ref[rows, :] = jax.nn.silu(g_ref[rows, :]) * (oc * lax.rsqrt(var + EPS) * gn_ref[...])
    state[...] = s

    @pl.when(step == pl.num_programs(1) - 1)
    def _():
        s_ref[...] = s


def _retention_rows(proj, cols, b, l, gn_g, c, cps):
    h, dh = N_HEADS, HEAD_DIM
    lg = jnp.log(1.0 - 2.0 ** (-5.0 - jnp.arange(h, dtype=F32)))
    j = jnp.arange(c, dtype=F32)
    diff = j[:, None] - j[None, :]
    dmat = jnp.where(diff[None] >= 0, jnp.exp(jnp.maximum(diff, 0.0)[None] * lg[:, None, None]), 0.0)
    lanes = lambda x: jnp.repeat(x, dh, axis=-1)
    q_dec = lanes(jnp.exp((j[:, None] + 1.0) * lg[None, :]))
    k_dec = lanes(jnp.exp((c - 1.0 - j)[:, None] * lg[None, :]))
    chunk_dec = jnp.broadcast_to(lanes(jnp.exp(c * lg)[None, :]), (MIX, MIX))
    head_of = jnp.arange(MIX) // dh
    bd = (head_of[:, None] == head_of[None, :]).astype(F32)
    r = c * cps
    per = l // r
    col = lambda g: pl.BlockSpec((r, MIX), lambda bi, i: (bi * per + i, g))
    const = lambda *shape: pl.BlockSpec(shape, lambda bi, i: (0,) * len(shape))
    state = pl.BlockSpec((None, MIX, MIX), lambda bi, i: (bi, 0, 0))
    od, s_bd = pl.pallas_call(
        functools.partial(_retention_rows_kernel, c=c, cps=cps),
        out_shape=(jax.ShapeDtypeStruct((b * l, MIX), F32), jax.ShapeDtypeStruct((b, MIX, MIX), F32)),
        grid=(b, per),
        in_specs=[col(cols[0]), col(cols[1]), col(cols[2]), col(cols[3]), state, const(h, c, c),
                  const(c, MIX), const(c, MIX), const(MIX, MIX), const(MIX, MIX), const(1, MIX)],
        out_specs=(col(0), state),
        scratch_shapes=[pltpu.VMEM((MIX, MIX), F32)],
        compiler_params=_params("parallel", "arbitrary"),
        name="retention_rows",
    )(proj, proj, proj, proj, jnp.zeros((b, MIX, MIX), F32), dmat, q_dec, k_dec, chunk_dec, bd, gn_g[None])
    s5 = s_bd.reshape(b, h, dh, h, dh)
    return od, jnp.stack([s5[:, i, :, i, :] for i in range(h)], axis=1)


def _merge_kernel(h_ref, g_ref, oa_ref, ob_ref, oc_ref, od_ref, wm_ref, wb_ref, wo_ref, o_ref):
    h = h_ref[...]
    d = h.shape[1]
    xn = _rms(h, g_ref[...]).astype(BF16)
    z = None
    for n, br_ref in enumerate((oa_ref, ob_ref, oc_ref, od_ref)):
        gate = jax.nn.sigmoid(jnp.dot(xn, wm_ref[:, n * d:(n + 1) * d], preferred_element_type=F32))
        br = jnp.dot(br_ref[...].astype(BF16), wb_ref[n], preferred_element_type=F32)
        z = gate * br if z is None else z + gate * br
    o_ref[...] = h + jnp.dot(z.astype(BF16), wo_ref[...], preferred_element_type=F32)


def _merge(h, g, branches, w_merge, w_branch, w_out, tm):
    n, d = h.shape
    rows = lambda w: pl.BlockSpec((tm, w), lambda i: (i, 0))
    return pl.pallas_call(
        _merge_kernel,
        out_shape=jax.ShapeDtypeStruct((n, d), F32),
        grid=(n // tm,),
        in_specs=[rows(d), pl.BlockSpec((1, d), lambda i: (0, 0)), rows(MIX), rows(MIX), rows(MIX), rows(MIX),
                  pl.BlockSpec(w_merge.shape, lambda i: (0, 0)),
                  pl.BlockSpec(w_branch.shape, lambda i: (0, 0, 0)),
                  pl.BlockSpec(w_out.shape, lambda i: (0, 0))],
        out_specs=rows(d),
        compiler_params=_params("parallel"),
        name="merge",
    )(h, g, *branches, w_merge, w_branch, w_out)


def _ffn_kernel(h_ref, g_ref, wg_ref, wu_ref, wo_ref, o_ref, xn_ref, acc_ref):
    k = pl.program_id(1)

    @pl.when(k == 0)
    def _():
        xn_ref[...] = _rms(h_ref[...], g_ref[...]).astype(BF16)
        acc_ref[...] = h_ref[...]

    xn = xn_ref[...]
    gate = jnp.dot(xn, wg_ref[...], preferred_element_type=F32)
    up = jnp.dot(xn, wu_ref[...], preferred_element_type=F32)
    act = (jax.nn.silu(gate) * up).astype(BF16)
    acc_ref[...] += jnp.dot(act, wo_ref[...], preferred_element_type=F32)

    @pl.when(k == pl.num_programs(1) - 1)
    def _():
        o_ref[...] = acc_ref[...]


def _ffn(h, g, w_in, w_out, tm, tf):
    n, d = h.shape
    d_ff = w_out.shape[0]
    nk = d_ff // tf
    return pl.pallas_call(
        _ffn_kernel,
        out_shape=jax.ShapeDtypeStruct((n, d), F32),
        grid=(n // tm, nk),
        in_specs=[pl.BlockSpec((tm, d), lambda i, k: (i, 0)), pl.BlockSpec((1, d), lambda i, k: (0, 0)),
                  pl.BlockSpec((d, tf), lambda i, k: (0, k)), pl.BlockSpec((d, tf), lambda i, k: (0, nk + k)),
                  pl.BlockSpec((tf, d), lambda i, k: (k, 0))],
        out_specs=pl.BlockSpec((tm, d), lambda i, k: (i, 0)),
        scratch_shapes=[pltpu.VMEM((tm, d), BF16), pltpu.VMEM((tm, d), F32)],
        compiler_params=_params("parallel", "arbitrary"),
        name="ffn",
    )(h, g, w_in, w_in, w_out)


def _ple_kernel(h_ref, g_ref, p_ref, wg_ref, wp_ref, gf_ref, o_ref, *, final):
    h = h_ref[...]
    xn = _rms(h, g_ref[...]).astype(BF16)
    gate = jax.nn.sigmoid(jnp.dot(xn, wg_ref[...], preferred_element_type=F32))
    emb = jnp.dot(p_ref[...].astype(BF16), wp_ref[...], preferred_element_type=F32)
    out = h + gate * emb
    o_ref[...] = _rms(out, gf_ref[...]) if final else out


def _ple(h, g, p, w_gate, w_proj, g_final, tm, final):
    n, d = h.shape
    dp = p.shape[1]
    vec = pl.BlockSpec((1, d), lambda i: (0, 0))
    return pl.pallas_call(
        functools.partial(_ple_kernel, final=final),
        out_shape=jax.ShapeDtypeStruct((n, d), F32),
        grid=(n // tm,),
        in_specs=[pl.BlockSpec((tm, d), lambda i: (i, 0)), vec, pl.BlockSpec((tm, dp), lambda i: (i, 0)),
                  pl.BlockSpec((d, d), lambda i: (0, 0)), pl.BlockSpec((dp, d), lambda i: (0, 0)), vec],
        out_specs=pl.BlockSpec((tm, d), lambda i: (i, 0)),
        compiler_params=_params("parallel"),
        name="ple",
    )(h, g, p, w_gate, w_proj, g_final)


def _stream_pages(pt_ref, streams, sems, layer, pp, n_chunks, n_seq, reverse, step):
    total = n_seq * n_chunks
    assert total % 2 == 0

    def locate(g):
        b = g // n_chunks
        c = g - b * n_chunks
        return b, (n_chunks - 1 - c if reverse else c)

    def copies(page_of, slot):
        return [pltpu.make_async_copy(hbm.at[layer, page_of(k)], buf.at[slot, k], sems.at[n, slot])
                for k in range(pp) for n, (hbm, buf) in enumerate(streams)]

    def fetch(g, slot):
        b, c = locate(g)
        for cp in copies(lambda k: pt_ref[b, c * pp + k], slot):
            cp.start()

    def pair(i, _):
        for slot in (0, 1):
            g = 2 * i + slot

            @pl.when(g + 1 < total)
            def _():
                fetch(g + 1, 1 - slot)

            for cp in copies(lambda k: 0, slot):
                cp.wait()
            step(g, *locate(g), slot)
        return 0

    fetch(0, 0)
    lax.fori_loop(0, total // 2, pair, 0)


def _paged_call(body, page_table, caches, dense, out_shape, scratch, name, page_shapes):
    whole = lambda x: pl.BlockSpec(x.shape, lambda i, pt, n=len(x.shape): (0,) * n)
    grid_spec = pltpu.PrefetchScalarGridSpec(
        num_scalar_prefetch=1, grid=(1,),
        in_specs=[pl.BlockSpec(memory_space=pl.ANY)] * len(caches) + [whole(x) for x in dense],
        out_specs=whole(out_shape),
        scratch_shapes=[pltpu.VMEM((2, PAGES_PER_STEP) + shape, F32) for shape in page_shapes]
        + [pltpu.SemaphoreType.DMA((len(caches), 2))] + scratch)
    return pl.pallas_call(body, out_shape=out_shape, grid_spec=grid_spec,
                          compiler_params=_params("arbitrary"), name=name)(page_table, *caches, *dense)


def _kmean_sample_kernel(pt_ref, k_hbm, o_ref, kbuf, sems, *, layer, pp, n_chunks):
    per_block = MOBA_BLOCK // PAGE_SIZE
    nblk = pp // per_block
    lane = lax.broadcasted_iota(jnp.int32, (MIX, LANES), 1)

    def step(g, b, c, slot):
        @pl.when(c == 0)
        def _():
            o_ref[b] = jnp.zeros((MIX, LANES), F32)

        acc = o_ref[b]
        for blk in range(nblk):
            tot = kbuf[slot, blk * per_block]
            for k in range(1, per_block):
                tot = tot + kbuf[slot, blk * per_block + k]
            mean = jnp.sum(tot, axis=1, keepdims=True) * (1.0 / MOBA_BLOCK)
            acc = jnp.where(lane == c * nblk + blk, mean, acc)
        o_ref[b] = acc

    _stream_pages(pt_ref, [(k_hbm, kbuf)], sems, layer, pp, n_chunks, o_ref.shape[0], False, step)


def _kmean_sample(cache_k, layer, page_table):
    db, n_pages = page_table.shape
    pp = PAGES_PER_STEP
    assert n_pages * PAGE_SIZE // MOBA_BLOCK <= LANES
    return _paged_call(
        functools.partial(_kmean_sample_kernel, layer=layer, pp=pp, n_chunks=n_pages // pp),
        page_table, [cache_k], [], jax.ShapeDtypeStruct((db, MIX, LANES), F32), [], "kmean_sample",
        [(MIX, PAGE_SIZE)])


def _lane_suffix_sum(x):
    n = x.shape[-1]
    lane = lax.broadcasted_iota(jnp.int32, x.shape, x.ndim - 1)
    s = 1
    while s < n:
        x = x + jnp.where(lane < n - s, pltpu.roll(x, n - s, x.ndim - 1), 0.0)
        s *= 2
    return x


def _expand_heads(x, t):
    return jnp.concatenate([jnp.broadcast_to(x[h:h + 1, :], (t, x.shape[1])) for h in range(N_HEADS)], axis=0)


def _decode_attn_kernel(pt_ref, *refs, layer, pp, n_chunks, t, fox, nb):
    if fox:
        (k_hbm, v_hbm, f_hbm, q_ref, kn_ref, vn_ref, fn_ref, o_ref, kbuf, vbuf, fbuf, sems,
         s_ref, m_ref, l_ref, acc_ref, cq_ref, tail_ref) = refs
        streams = [(k_hbm, kbuf), (v_hbm, vbuf), (f_hbm, fbuf)]
    else:
        (k_hbm, v_hbm, q_ref, kn_ref, vn_ref, km_ref, o_ref, kbuf, vbuf, sems,
         s_ref, m_ref, l_ref, acc_ref, sel_ref) = refs
        streams = [(k_hbm, kbuf), (v_hbm, vbuf)]
    r = N_HEADS * t

    def step(g, b, c, slot):
        first = c == (n_chunks - 1 if fox else 0)
        last = c == (0 if fox else n_chunks - 1)
        q = q_ref[b]
        qb = (q * SCALE).astype(BF16)

        @pl.when(first)
        def _():
            s = jnp.dot(qb, kn_ref[b].astype(BF16), preferred_element_type=F32)
            key = lax.broadcasted_iota(jnp.int32, (r, LANES), 1)
            qi = lax.broadcasted_iota(jnp.int32, (r, LANES), 0) % t
            if fox:
                cn = _expand_heads(_lane_cumsum(fn_ref[b]), t)
                cq = jnp.sum(jnp.where(key == qi, cn, 0.0), axis=1, keepdims=True)
                cq_ref[...] = jnp.broadcast_to(cq, cq_ref.shape)
                tail_ref[...] = jnp.zeros(tail_ref.shape, F32)
                s = s + cq - cn
            s = jnp.where(key <= qi, s, NEG_INF)
            m = jnp.max(s, axis=1, keepdims=True)
            p = jnp.exp(s - m)
            m_ref[...] = jnp.broadcast_to(m, m_ref.shape)
            l_ref[...] = jnp.broadcast_to(jnp.sum(p, axis=1, keepdims=True), l_ref.shape)
            acc_ref[...] = _nt_dot(p.astype(BF16), vn_ref[b].astype(BF16))
            if not fox:
                gate = jnp.dot(q, km_ref[b], precision=lax.Precision.HIGHEST, preferred_element_type=F32)
                blk = lax.broadcasted_iota(jnp.int32, gate.shape, 1)
                sel_ref[...] = _topk_select(jnp.where(blk < nb, gate, NEG_INF), min(MOBA_TOPK, nb), 1)

        for k in range(pp):
            s_ref[:, k * PAGE_SIZE:(k + 1) * PAGE_SIZE] = jnp.dot(qb, kbuf[slot, k].astype(BF16),
                                                                  preferred_element_type=F32)
        s = s_ref[...]
        if fox:
            f = jnp.concatenate([fbuf[slot, k] for k in range(pp)], axis=1)
            suffix = _lane_suffix_sum(f)
            tail = tail_ref[:, :1]
            s = s + jnp.concatenate([cq_ref[...]] * pp, axis=1) + _expand_heads(suffix - f + tail, t)
            tail_ref[...] = jnp.broadcast_to(tail + suffix[:, :1], tail_ref.shape)
        else:
            per_block = MOBA_BLOCK // PAGE_SIZE
            nblk = pp // per_block
            sel = sel_ref[...]
            blk = lax.broadcasted_iota(jnp.int32, sel.shape, 1)
            cols = []
            for j in range(nblk):
                chosen = jnp.sum(jnp.where(blk == c * nblk + j, sel, 0.0), axis=1, keepdims=True)
                cols.append(jnp.broadcast_to(chosen, (r, MOBA_BLOCK)))
            s = jnp.where(jnp.concatenate(cols, axis=1) > 0.5, s, NEG_INF)
        m = m_ref[:, :1]
        m_new = jnp.maximum(m, jnp.max(s, axis=1, keepdims=True))
        alpha = jnp.exp(m - m_new)
        p = jnp.exp(s - m_new).astype(BF16)
        l_new = alpha * l_ref[:, :1] + jnp.sum(p.astype(F32), axis=1, keepdims=True)
        acc = alpha * acc_ref[...]
        for k in range(pp):
            acc = acc + _nt_dot(p[:, k * PAGE_SIZE:(k + 1) * PAGE_SIZE], vbuf[slot, k].astype(BF16))
        m_ref[...] = jnp.broadcast_to(m_new, m_ref.shape)
        l_ref[...] = jnp.broadcast_to(l_new, l_ref.shape)
        acc_ref[...] = acc

        @pl.when(last)
        def _():
            o = acc / l_new
            grp = lax.broadcasted_iota(jnp.int32, (t, MIX), 1) // HEAD_DIM
            out = jnp.zeros((t, MIX), F32)
            for h in range(N_HEADS):
                out = out + jnp.where(grp == h, o[h * t:(h + 1) * t, :], 0.0)
            o_ref[b] = out

    _stream_pages(pt_ref, streams, sems, layer, pp, n_chunks, q_ref.shape[0], fox, step)


def _decode_attn(cache_k, cache_v, layer, page_table, q_blk, k_new, v_new, extra, fox):
    db, n_pages = page_table.shape
    pp = PAGES_PER_STEP
    r = q_blk.shape[1]
    t = r // N_HEADS
    scratch = [pltpu.VMEM((r, pp * PAGE_SIZE), F32), pltpu.VMEM((r, LANES), F32), pltpu.VMEM((r, LANES), F32),
               pltpu.VMEM((r, MIX), F32), pltpu.VMEM((r, LANES), F32)]
    caches, pages = [cache_k, cache_v], [(MIX, PAGE_SIZE), (MIX, PAGE_SIZE)]
    if fox:
        caches, pages = caches + [extra[0]], pages + [(N_HEADS, PAGE_SIZE)]
        scratch += [pltpu.VMEM((N_HEADS, LANES), F32)]
    return _paged_call(
        functools.partial(_decode_attn_kernel, layer=layer, pp=pp, n_chunks=n_pages // pp, t=t, fox=fox,
                          nb=n_pages * PAGE_SIZE // MOBA_BLOCK),
        page_table, caches, [q_blk, k_new, v_new, extra[-1]], jax.ShapeDtypeStruct((db, t, MIX), F32), scratch,
        "fox_sample" if fox else "moba_sample", pages)


def _to_heads(x, b, l):
    return x.reshape(b, l, N_HEADS, HEAD_DIM).transpose(0, 2, 1, 3)


def _from_heads(x):
    b, h, l, dh = x.shape
    return x.transpose(0, 2, 1, 3).reshape(b * l, h * dh)


def _split_proj(proj):
    cols = [proj[:, g * MIX:(g + 1) * MIX] for g in range(N_PROJ_GROUPS)]
    logf = proj[:, N_PROJ_GROUPS * MIX:N_PROJ_GROUPS * MIX + N_HEADS]
    return cols, logf


def _tail(h, p, prm, branches, final, tm):
    h = _merge(h, prm["norm_mix"], branches, prm["w_merge"], prm["w_branch"], prm["w_out"], tm)
    d_ff = prm["w_ffn_out"].shape[0]
    tf = d_ff // 2 if (d_ff // 2) % LANES == 0 else d_ff
    h = _ffn(h, prm["norm_ffn"], prm["w_ffn_in"], prm["w_ffn_out"], min(512, h.shape[0]), tf)
    return _ple(h, prm["norm_ple"], p, prm["w_ple_gate"], prm["w_ple_proj"], prm["norm_final"], tm, final)


def _prompt_layer(h, p, prm, tabs, b, l, final):
    tm = 256
    proj, tr, vtb = _proj(h, prm["norm_mix"], prm["w_in"], prm["b_forget"], prm["ln_g"], prm["ln_b"], tabs, tm,
                          batch=b)
    logf = proj[:, N_PROJ_GROUPS * MIX:N_PROJ_GROUPS * MIX + N_HEADS]
    kmean = _kmean_prompt(proj, 1, b, l).reshape(b, -1, N_HEADS, HEAD_DIM).transpose(0, 2, 1, 3)
    oa = _attn_prompt(tr, 0, vtb, 0, kmean, fox=False)
    c = _cumsum_prompt(logf.reshape(b, l, N_HEADS).transpose(0, 2, 1).reshape(b * N_HEADS, l))
    ob = _attn_prompt(tr, 3, vtb, 1, c.reshape(b, N_HEADS, l), fox=True)
    oc = _gmlp(proj, 6, 7, prm["gmlp_ws"], prm["gmlp_bs"], GMLP_CHUNK)
    cps = 8 if l % (8 * RET_CHUNK) == 0 else 1
    od, s_new = _retention_rows(proj, (8, 9, 10, 11), b, l, prm["ret_gn_g"], RET_CHUNK, cps)
    h = _tail(h, p, prm, (oa, ob, oc, od), final, tm)
    heads = lambda g: tr[:, g].reshape(b, N_HEADS, HEAD_DIM, l).transpose(0, 3, 1, 2)
    rows = (heads(1), heads(2), heads(4), heads(5), logf.reshape(b, l, N_HEADS), s_new)
    return h, rows


def _block_diag_q(q, db, t):
    grp = jnp.arange(MIX) // HEAD_DIM
    mask = (grp[None, None, :] == jnp.arange(N_HEADS)[:, None, None]).astype(F32)
    return (q.reshape(db, 1, t, MIX) * mask[None]).reshape(db, N_HEADS * t, MIX)


def _new_keys_t(x, db, t):
    return jnp.pad(x.reshape(db, t, MIX).transpose(0, 2, 1), ((0, 0), (0, 0), (0, LANES - t)))


def _sample_layer(h, p, prm, tabs, caches, layer, page_table, db, t, final):
    cache_k_a, cache_v_a, cache_k_b, cache_v_b, cache_logf_t, state = caches
    n = db * t
    (proj,) = _proj(h, prm["norm_mix"], prm["w_in"], prm["b_forget"], prm["ln_g"], prm["ln_b"], tabs, n)
    (qa, ka, va, qb, kb, vb, _, vn, qd, kd, vd, gd), logf = _split_proj(proj)
    kmean = _kmean_sample(cache_k_a, layer, page_table)
    oa = _decode_attn(cache_k_a, cache_v_a, layer, page_table, _block_diag_q(qa, db, t),
                      _new_keys_t(ka, db, t), _new_keys_t(va, db, t), (kmean,), fox=False)
    logf_new = jnp.pad(logf.reshape(db, t, N_HEADS).transpose(0, 2, 1), ((0, 0), (0, 0), (0, LANES - t)))
    ob = _decode_attn(cache_k_b, cache_v_b, layer, page_table, _block_diag_q(qb, db, t),
                      _new_keys_t(kb, db, t), _new_keys_t(vb, db, t), (cache_logf_t, logf_new), fox=True)
    oc = _gmlp(proj, 6, 7, prm["gmlp_ws"][:, :t, :t], prm["gmlp_bs"][:, :t], t)
    hd = lambda x: _to_heads(x, db, t)
    od, s_new = _retention(hd(qd), hd(kd).transpose(0, 1, 3, 2), hd(vd), hd(gd), state[layer],
                           prm["ret_gn_g"], t, 1)
    h = _tail(h, p, prm, (oa.reshape(n, MIX), ob.reshape(n, MIX), oc, _from_heads(od)), final, n)
    rows = (ka.reshape(db, t, N_HEADS, HEAD_DIM), va.reshape(db, t, N_HEADS, HEAD_DIM),
            kb.reshape(db, t, N_HEADS, HEAD_DIM), vb.reshape(db, t, N_HEADS, HEAD_DIM),
            logf.reshape(db, t, N_HEADS), s_new, vn.reshape(db, t, MIX))
    return h, rows


def kernel(x_prompt, x_sample, cache_k_a, cache_v_a, cache_k_b, cache_v_b, cache_logf_b, state_ret, page_table, p_prompt, p_sample, norm_mix, w_in, b_forget, gmlp_ln_g, gmlp_ln_b, gmlp_ws, gmlp_bs, ret_gn_g, w_branch, w_merge, w_out, norm_ffn, w_ffn_in, w_ffn_out, norm_ple, w_ple_gate, w_ple_proj, norm_final):
    b, l, d = x_prompt.shape
    db, t, _ = x_sample.shape
    depth = w_in.shape[0]
    n_pages = page_table.shape[1]
    past_len = n_pages * PAGE_SIZE
    assert d == N_HEADS * MIX and l % MOBA_BLOCK == 0
    assert past_len % MOBA_BLOCK == 0 and t <= LANES and t % 8 == 0 and t < MOBA_BLOCK
    assert n_pages % PAGES_PER_STEP == 0

    pos_p = jnp.arange(l, dtype=jnp.int32)
    pos_s = jnp.tile(past_len + jnp.arange(t, dtype=jnp.int32), db)
    tabs_p = _rope_tables(pos_p, ROT_DIMS, ROPE_THETA) + _rope_tables(pos_p, HEAD_DIM, RET_THETA)
    tabs_s = _rope_tables(pos_s, ROT_DIMS, ROPE_THETA) + _rope_tables(pos_s, HEAD_DIM, RET_THETA)

    pool = cache_k_a.shape[1]
    pages = lambda x: x.transpose(0, 1, 3, 4, 2).reshape(depth, pool, MIX, PAGE_SIZE)
    caches = (pages(cache_k_a), pages(cache_v_a), pages(cache_k_b), pages(cache_v_b),
              cache_logf_b.transpose(0, 1, 3, 2), state_ret)

    hp = x_prompt.reshape(b * l, d)
    hs = x_sample.reshape(db * t, d)
    rows_p, rows_s = [], []
    f_lo, f_hi = 6 * MIX, 6 * MIX + N_HEADS
    w_in_pad = jnp.concatenate(
        [w_in[:, :, :f_lo].astype(BF16), w_in[:, :, f_hi:].astype(BF16),
         jnp.pad(w_in[:, :, f_lo:f_hi].astype(BF16), ((0, 0), (0, 0), (0, LANES - N_HEADS)))], axis=2)
    for i in range(depth):
        prm = dict(
            norm_mix=norm_mix[i][None], w_in=w_in_pad[i],
            b_forget=jnp.pad(b_forget[i], (0, LANES - N_HEADS))[None],
            ln_g=gmlp_ln_g[i][None], ln_b=gmlp_ln_b[i][None], gmlp_ws=gmlp_ws[i], gmlp_bs=gmlp_bs[i],
            ret_gn_g=ret_gn_g[i], w_branch=w_branch[i].astype(BF16), w_merge=w_merge[i].astype(BF16),
            w_out=w_out[i].astype(BF16), norm_ffn=norm_ffn[i][None], w_ffn_in=w_ffn_in[i].astype(BF16),
            w_ffn_out=w_ffn_out[i].astype(BF16), norm_ple=norm_ple[i][None],
            w_ple_gate=w_ple_gate[i].astype(BF16), w_ple_proj=w_ple_proj[i].astype(BF16),
            norm_final=norm_final[None])
        final = i == depth - 1
        hp, rp = _prompt_layer(hp, p_prompt[i].reshape(b * l, -1), prm, tabs_p, b, l, final)
        hs, rs = _sample_layer(hs, p_sample[i].reshape(db * t, -1), prm, tabs_s, caches, i, page_table,
                               db, t, final)
        rows_p.append(rp)
        rows_s.append(rs)
    stack = lambda rows, j: jnp.stack([r[j] for r in rows], axis=0)
    return ((hp.reshape(b, l, d), hs.reshape(db, t, d))
            + tuple(stack(rows_p, j) for j in range(6)) + tuple(stack(rows_s, j) for j in range(7)))
```

```python
import functools

import jax
import jax.numpy as jnp
from jax import lax
from jax.experimental import pallas as pl
from jax.experimental.pallas import tpu as pltpu

F32 = jnp.float32
BF16 = jnp.bfloat16

N_HEADS = 4
HEAD_DIM = 64
MIX = N_HEADS * HEAD_DIM
ROT_DIMS = HEAD_DIM // 4
ROPE_THETA = 500000.0
RET_THETA = 10000.0
MOBA_BLOCK = 256
MOBA_TOPK = 3
GMLP_CHUNK = 128
RET_CHUNK = 128
PAGE_SIZE = 128
EPS = 1e-6
SCALE = HEAD_DIM ** -0.5
NEG_INF = float("-inf")

LANES = 128
VMEM_LIMIT = 56 * 1024 * 1024
PAGES_PER_STEP = 32
N_PROJ_GROUPS = 12
N_ATTN_GROUPS = 6
W_IN_PAD = N_PROJ_GROUPS * MIX + LANES
LOG2E = 1.4426950408889634
AUG = 8
KV_BLOCKS_PER_TRIP = 4


def _params(*sem):
    return pltpu.CompilerParams(dimension_semantics=sem, vmem_limit_bytes=VMEM_LIMIT)


def _rms(x, g):
    return x * lax.rsqrt(jnp.mean(x * x, axis=-1, keepdims=True) + EPS) * g


def _nt_dot(a, b, precision=None):
    return lax.dot_general(a, b, (((1,), (1,)), ((), ())), precision=precision,
                           preferred_element_type=F32)


def _log_sigmoid(x):
    return jnp.minimum(x, 0.0) - jnp.log(1.0 + jnp.exp(-jnp.abs(x)))


def _proj_kernel(h_ref, g_ref, w_ref, bf_ref, lng_ref, lnb_ref, ca_ref, sa_ref, cd_ref, sd_ref, o_ref,
                 *t_refs):
    xn = _rms(h_ref[...], g_ref[...]).astype(BF16)
    lane = lax.broadcasted_iota(jnp.int32, (1, MIX), 1) & (HEAD_DIM - 1)

    def rope(y, c_ref, s_ref, half):
        c = jnp.concatenate([c_ref[...], c_ref[...]], axis=1)
        s = jnp.concatenate([s_ref[...], s_ref[...]], axis=1)
        rot = jnp.where(lane < half, pltpu.roll(y, MIX - half, 1), pltpu.roll(y, half, 1))
        return y * c + rot * s

    for grp in range(N_PROJ_GROUPS):
        y = jnp.dot(xn, w_ref[:, grp * MIX:(grp + 1) * MIX], preferred_element_type=F32)
        if grp in (0, 1):
            y = rope(y, ca_ref, sa_ref, ROT_DIMS // 2)
        elif grp == 6:
            y = jax.nn.gelu(y)
        elif grp == 7:
            y = jax.nn.gelu(y)
            mu = jnp.mean(y, axis=-1, keepdims=True)
            yc = y - mu
            var = jnp.mean(yc * yc, axis=-1, keepdims=True)
            y = yc * lax.rsqrt(var + EPS) * lng_ref[...] + lnb_ref[...]
        elif grp == 8:
            y = rope(y, cd_ref, sd_ref, HEAD_DIM // 2)
        elif grp == 9:
            y = rope(y, cd_ref, sd_ref, HEAD_DIM // 2) * SCALE
        o_ref[:, grp * MIX:(grp + 1) * MIX] = y
        if t_refs and grp < N_ATTN_GROUPS:
            yt = y.T
            t_refs[0][grp] = yt
            if grp % 3 == 2:
                t_refs[1][grp // 3] = yt.astype(BF16)
    f = jnp.dot(xn, w_ref[:, N_PROJ_GROUPS * MIX:], preferred_element_type=F32) + bf_ref[...]
    o_ref[:, N_PROJ_GROUPS * MIX:] = _log_sigmoid(f)


def _proj(h, g, w_pad, bf_pad, ln_g, ln_b, tabs, tm, batch=None):
    n, d = h.shape
    ca, sa, cd, sd = tabs
    tab_blocks = ca.shape[0] // tm
    full = lambda shape: pl.BlockSpec(shape, lambda i: (0, 0))
    tab = pl.BlockSpec((tm, LANES), lambda i: (i % tab_blocks, 0))
    out_shape = [jax.ShapeDtypeStruct((n, W_IN_PAD), F32)]
    out_specs = [pl.BlockSpec((tm, W_IN_PAD), lambda i: (i, 0))]
    if batch is not None:
        l = n // batch
        per_seq = l // tm
        t_map = lambda i: (i // per_seq, 0, 0, i % per_seq)
        out_shape += [jax.ShapeDtypeStruct((batch, N_ATTN_GROUPS, MIX, l), F32),
                      jax.ShapeDtypeStruct((batch, N_ATTN_GROUPS // 3, MIX, l), BF16)]
        out_specs += [pl.BlockSpec((None, N_ATTN_GROUPS, MIX, tm), t_map),
                      pl.BlockSpec((None, N_ATTN_GROUPS // 3, MIX, tm), t_map)]
    return pl.pallas_call(
        _proj_kernel,
        out_shape=out_shape,
        grid=(n // tm,),
        in_specs=[pl.BlockSpec((tm, d), lambda i: (i, 0)), full((1, d)), full((d, W_IN_PAD)),
                  full((1, LANES)), full((1, MIX)), full((1, MIX)), tab, tab, tab, tab],
        out_specs=out_specs,
        compiler_params=_params("parallel"),
        name="proj",
    )(h, g, w_pad, bf_pad, ln_g, ln_b, ca, sa, cd, sd)


def _rope_tables(pos, n_rot, theta):
    half = n_rot // 2
    inv = theta ** (-jnp.arange(half, dtype=F32) / half)
    ang = pos.astype(F32)[:, None] * inv[None, :]
    cos, sin = jnp.cos(ang), jnp.sin(ang)
    rest = HEAD_DIM - n_rot
    ones = jnp.ones((pos.shape[0], rest), F32)
    zeros = jnp.zeros((pos.shape[0], rest), F32)
    c = jnp.concatenate([cos, cos, ones], axis=1)
    s = jnp.concatenate([-sin, sin, zeros], axis=1)
    reps = LANES // HEAD_DIM
    return jnp.tile(c, (1, reps)), jnp.tile(s, (1, reps))


def _kmean_kernel(k_ref, o_ref, *, nblk):
    x = k_ref[...].reshape(nblk, MOBA_BLOCK, MIX)
    o_ref[...] = jnp.sum(x, axis=1) * (1.0 / MOBA_BLOCK)


def _kmean_prompt(proj, col, b, l):
    nblk = 8
    assert l % (nblk * MOBA_BLOCK) == 0
    per_seq = l // (nblk * MOBA_BLOCK)
    return pl.pallas_call(
        functools.partial(_kmean_kernel, nblk=nblk),
        out_shape=jax.ShapeDtypeStruct((b, l // MOBA_BLOCK, MIX), F32),
        grid=(b, per_seq),
        in_specs=[pl.BlockSpec((nblk * MOBA_BLOCK, MIX), lambda i, j: (i * per_seq + j, col))],
        out_specs=pl.BlockSpec((None, nblk, MIX), lambda i, j: (i, j, 0)),
        compiler_params=_params("parallel", "parallel"),
        name="kmean_prompt",
    )(proj)


def _lane_cumsum(x):
    n = x.shape[-1]
    lane = lax.broadcasted_iota(jnp.int32, x.shape, x.ndim - 1)
    s = 1
    while s < n:
        x = x + jnp.where(lane >= s, pltpu.roll(x, s, x.ndim - 1), 0.0)
        s *= 2
    return x


def _cumsum_kernel(x_ref, o_ref):
    o_ref[...] = _lane_cumsum(x_ref[...])


def _cumsum_prompt(logf_t):
    r, l = logf_t.shape
    return pl.pallas_call(
        _cumsum_kernel,
        out_shape=jax.ShapeDtypeStruct((r, l), F32),
        grid=(1,),
        in_specs=[pl.BlockSpec((r, l), lambda i: (0, 0))],
        out_specs=pl.BlockSpec((r, l), lambda i: (0, 0)),
        compiler_params=_params("arbitrary"),
        name="cumsum_prompt",
    )(logf_t)


def _softmax_first(s, v):
    m = jnp.max(s, axis=1, keepdims=True)
    p = jnp.exp(s - m)
    l = jnp.sum(p, axis=1, keepdims=True)
    acc = jnp.dot(p.astype(BF16), v, preferred_element_type=F32)
    return m, l, acc


def _softmax_update(carry, s, v):
    m, l, acc = carry
    m_new = jnp.maximum(m, jnp.max(s, axis=1, keepdims=True))
    alpha = jnp.exp(m - m_new)
    p = jnp.exp(s - m_new)
    l = alpha * l + jnp.sum(p, axis=1, keepdims=True)
    acc = alpha * acc + jnp.dot(p.astype(BF16), v, preferred_element_type=F32)
    return m_new, l, acc


def _topk_select(gate, n_top, axis):
    pos = lax.broadcasted_iota(jnp.int32, gate.shape, axis)
    sel = jnp.zeros(gate.shape, F32)
    g = gate
    for _ in range(n_top):
        mx = jnp.max(g, axis=axis, keepdims=True)
        idx = jnp.min(jnp.where(g == mx, pos, gate.shape[axis]), axis=axis, keepdims=True)
        hit = pos == idx
        sel = jnp.maximum(sel, jnp.where(hit, jnp.where(mx > NEG_INF, 1.0, 0.0), 0.0))
        g = jnp.where(hit, NEG_INF, g)
    return sel


def _split3(x):
    hi = x.astype(BF16).astype(F32)
    r = x - hi
    mid = r.astype(BF16).astype(F32)
    return hi, mid, r - mid


def _attn_prompt_kernel(qt_ref, kt_ref, vt_ref, x_ref, o_ref, k_ref, qa_ref, *sel_refs, fox):
    t = MOBA_BLOCK
    i = pl.program_id(1)
    own = pl.multiple_of(i * t, t)
    causal = (lax.broadcasted_iota(jnp.int32, (t, t), 0) <= lax.broadcasted_iota(jnp.int32, (t, t), 1))
    pad = jnp.zeros((HEAD_DIM - AUG, t), F32)
    one = jnp.ones((1, t), F32)
    zero = jnp.zeros((1, t), F32)
    heads = range(N_HEADS)

    @pl.when(i == 0)
    def _():
        def build(jt, _):
            st = pl.multiple_of(jt * t, t)
            for h in heads:
                if fox:
                    hi, mid, lo = _split3(x_ref[h:h + 1, pl.ds(st, t)] * LOG2E)
                    aug = jnp.concatenate([one, one, one, -hi, -mid, -lo, zero, zero], axis=0)
                else:
                    aug = jnp.zeros((AUG, t), F32)
                kh = jnp.concatenate([kt_ref[h * HEAD_DIM:(h + 1) * HEAD_DIM, pl.ds(st, t)], aug, pad], axis=0)
                k_ref[h, pl.ds(st, t), :] = kh.T.astype(BF16)
            return 0

        lax.fori_loop(0, kt_ref.shape[1] // t, build, 0)

    for h in heads:
        qh = qt_ref[h * HEAD_DIM:(h + 1) * HEAD_DIM, :]
        if fox:
            aug = jnp.concatenate(_split3(x_ref[h:h + 1, pl.ds(own, t)] * LOG2E) + (one, one, one, zero, zero),
                                  axis=0)
        else:
            aug = jnp.zeros((AUG, t), F32)
            nb = x_ref.shape[1]
            gate = jnp.dot(x_ref[h], qh, precision=lax.Precision.HIGHEST, preferred_element_type=F32)
            blk = lax.broadcasted_iota(jnp.int32, (nb, t), 0)
            sel_refs[0][h] = _topk_select(jnp.where(blk < i, gate, NEG_INF), min(MOBA_TOPK, nb), 0)
        qa_ref[h] = jnp.concatenate([qh * (SCALE * LOG2E), aug, pad], axis=0).astype(BF16)

    def attend(carry, start, n, mask):
        ss = [mask(h, jnp.dot(k_ref[h, pl.ds(start, n * t), :], qa_ref[h], preferred_element_type=F32))
              for h in heads]
        ms = [jnp.max(s, axis=0, keepdims=True) for s in ss]
        if carry is not None:
            ms = [jnp.maximum(carry[3 * h], ms[h]) for h in heads]
        ps = [jnp.exp2(ss[h] - ms[h]) for h in heads]
        pvs = [jnp.dot(vt_ref[h * HEAD_DIM:(h + 1) * HEAD_DIM, pl.ds(start, n * t)], ps[h].astype(BF16),
                       preferred_element_type=F32) for h in heads]
        out = []
        for h in heads:
            l = jnp.sum(ps[h], axis=0, keepdims=True)
            if carry is None:
                out += [ms[h], l, pvs[h]]
            else:
                alpha = jnp.exp2(carry[3 * h] - ms[h])
                out += [ms[h], alpha * carry[3 * h + 1] + l, alpha * carry[3 * h + 2] + pvs[h]]
        return tuple(out)

    def past(n):
        def body(j, carry):
            def mask(h, s):
                if fox:
                    return s
                keep = jnp.concatenate([jnp.broadcast_to(sel_refs[0][h, pl.ds(j * n + k, 1), :], (t, t))
                                        for k in range(n)], axis=0)
                return jnp.where(keep > 0.5, s, NEG_INF)

            return attend(carry, pl.multiple_of(j * (n * t), n * t), n, mask)

        return body

    carry = attend(None, own, 1, lambda h, s: jnp.where(causal, s, NEG_INF))
    n, done = KV_BLOCKS_PER_TRIP, 0
    while n >= 1:
        trips = (i - done) // n
        carry = lax.fori_loop(done // n, done // n + trips, past(n), carry)
        done = done + trips * n
        n //= 2
    ot = jnp.concatenate([carry[3 * h + 2] / carry[3 * h + 1] for h in heads], axis=0)
    o_ref[...] = ot.T


def _attn_prompt(tr, grp, vtb, which, x, fox):
    b, _, _, l = tr.shape
    nb = l // MOBA_BLOCK
    t = MOBA_BLOCK
    if fox:
        x_spec = pl.BlockSpec((None, N_HEADS, l), lambda bi, i: (bi, 0, 0))
        scratch = []
    else:
        x_spec = pl.BlockSpec((None, N_HEADS, nb, HEAD_DIM), lambda bi, i: (bi, 0, 0, 0))
        scratch = [pltpu.VMEM((N_HEADS, nb, t), F32)]
    return pl.pallas_call(
        functools.partial(_attn_prompt_kernel, fox=fox),
        out_shape=jax.ShapeDtypeStruct((b * l, MIX), F32),
        grid=(b, nb),
        in_specs=[pl.BlockSpec((None, None, MIX, t), lambda bi, i: (bi, grp, 0, i)),
                  pl.BlockSpec((None, None, MIX, l), lambda bi, i: (bi, grp + 1, 0, 0)),
                  pl.BlockSpec((None, None, MIX, l), lambda bi, i: (bi, which, 0, 0)),
                  x_spec],
        out_specs=pl.BlockSpec((t, MIX), lambda bi, i: (bi * nb + i, 0)),
        scratch_shapes=[pltpu.VMEM((N_HEADS, l, 2 * HEAD_DIM), BF16),
                        pltpu.VMEM((N_HEADS, 2 * HEAD_DIM, t), BF16)] + scratch,
        compiler_params=_params("parallel", "arbitrary"),
        name="fox_prompt" if fox else "moba_prompt",
    )(tr, tr, vtb, x)


def _gmlp_kernel(u_ref, v_ref, w_ref, b_ref, o_ref, *, c, cps):
    grp = lax.broadcasted_iota(jnp.int32, (c, MIX), 1) // HEAD_DIM
    row = lax.broadcasted_iota(jnp.int32, (c, c), 0)
    col = lax.broadcasted_iota(jnp.int32, (c, c), 1)
    w_cat = jnp.concatenate([jnp.where(col <= row, w_ref[g], 0.0) for g in range(N_HEADS)], axis=1).astype(BF16)
    bias = b_ref[...]
    for ci in range(cps):
        rows = slice(ci * c, (ci + 1) * c)
        v = v_ref[rows, :]
        v_cat = jnp.concatenate([jnp.where(grp == g, v, 0.0) for g in range(N_HEADS)], axis=0)
        mixed = jnp.dot(w_cat, v_cat.astype(BF16), preferred_element_type=F32)
        o_ref[rows, :] = u_ref[rows, :] * (mixed + bias)


def _gmlp(proj, u_col, v_col, ws, bs, c):
    n = proj.shape[0]
    bias = jnp.repeat(bs.T, HEAD_DIM, axis=1)
    cps = 8 if n % (8 * c) == 0 else 1
    col = lambda g: pl.BlockSpec((cps * c, MIX), lambda i: (i, g))
    return pl.pallas_call(
        functools.partial(_gmlp_kernel, c=c, cps=cps),
        out_shape=jax.ShapeDtypeStruct((n, MIX), F32),
        grid=(n // (cps * c),),
        in_specs=[col(u_col), col(v_col), pl.BlockSpec((N_HEADS, c, c), lambda i: (0, 0, 0)),
                  pl.BlockSpec((c, MIX), lambda i: (0, 0))],
        out_specs=col(0),
        compiler_params=_params("parallel"),
        name="gmlp",
    )(proj, proj, ws, bias)


def _retention_kernel(q_ref, kt_ref, v_ref, g_ref, s0_ref, dm_ref, qd_ref, kd_ref, cd_ref, gn_ref,
                      o_ref, s_ref, state, *, c, cps):
    step = pl.program_id(2)

    @pl.when(step == 0)
    def _():
        state[...] = s0_ref[...]

    dmat = dm_ref[...]
    q_dec = qd_ref[...]
    k_dec = kd_ref[...]
    chunk_dec = cd_ref[...]
    gn = gn_ref[...]
    s = state[...]
    for ci in range(cps):
        rows = slice(ci * c, (ci + 1) * c)
        qc = q_ref[rows, :].astype(BF16)
        vc = v_ref[rows, :].astype(BF16)
        kt = kt_ref[:, rows]
        att = jnp.dot(qc, kt.astype(BF16), preferred_element_type=F32) * dmat
        o = (jnp.dot(att.astype(BF16), vc, preferred_element_type=F32)
             + jnp.dot(qc, s.astype(BF16), preferred_element_type=F32) * q_dec)
        s = s * chunk_dec + jnp.dot((kt * k_dec).astype(BF16), vc, preferred_element_type=F32)
        mu = jnp.mean(o, axis=-1, keepdims=True)
        oc = o - mu
        var = jnp.mean(oc * oc, axis=-1, keepdims=True)
        o_ref[rows, :] = jax.nn.silu(g_ref[rows, :]) * (oc * lax.rsqrt(var + EPS) * gn)
    state[...] = s

    @pl.when(step == pl.num_programs(2) - 1)
    def _():
        s_ref[...] = s


def _retention(q, kt, v, g, s0, gn_g, c, cps):
    b, h, l, dh = q.shape
    lg = jnp.log(1.0 - 2.0 ** (-5.0 - jnp.arange(h, dtype=F32)))
    j = jnp.arange(c, dtype=F32)
    diff = j[:, None] - j[None, :]
    dmat = jnp.where(diff[None] >= 0, jnp.exp(jnp.maximum(diff, 0.0)[None] * lg[:, None, None]), 0.0)
    q_dec = jnp.broadcast_to(jnp.exp((j[None, :] + 1.0) * lg[:, None])[:, :, None], (h, c, dh))
    k_dec = jnp.broadcast_to(jnp.exp((c - 1.0 - j)[None, :] * lg[:, None])[:, None, :], (h, dh, c))
    chunk_dec = jnp.broadcast_to(jnp.exp(c * lg)[:, None, None], (h, dh, dh))
    r = c * cps
    rows = pl.BlockSpec((None, None, r, dh), lambda bi, hi, i: (bi, hi, i, 0))
    per_bh = pl.BlockSpec((None, None, dh, dh), lambda bi, hi, i: (bi, hi, 0, 0))
    per_h = lambda s1, s2: pl.BlockSpec((None, s1, s2), lambda bi, hi, i: (hi, 0, 0))
    return pl.pallas_call(
        functools.partial(_retention_kernel, c=c, cps=cps),
        out_shape=(jax.ShapeDtypeStruct((b, h, l, dh), F32), jax.ShapeDtypeStruct((b, h, dh, dh), F32)),
        grid=(b, h, l // r),
        in_specs=[rows, pl.BlockSpec((None, None, dh, r), lambda bi, hi, i: (bi, hi, 0, i)), rows, rows,
                  per_bh, per_h(c, c), per_h(c, dh), per_h(dh, c), per_h(dh, dh), per_h(1, dh)],
        out_specs=(rows, per_bh),
        scratch_shapes=[pltpu.VMEM((dh, dh), F32)],
        compiler_params=_params("parallel", "parallel", "arbitrary"),
        name="retention",
    )(q, kt, v, g, s0, dmat, q_dec, k_dec, chunk_dec, gn_g.reshape(h, 1, dh))


def _retention_rows_kernel(q_ref, k_ref, v_ref, g_ref, s0_ref, dm_ref, qd_ref, kd_ref, cd_ref, bd_ref, gn_ref,
                           o_ref, s_ref, state, *, c, cps):
    step = pl.program_id(1)

    @pl.when(step == 0)
    def _():
        state[...] = s0_ref[...]

    heads = range(N_HEADS)
    grp = lax.broadcasted_iota(jnp.int32, (c, MIX), 1) // HEAD_DIM
    bd = bd_ref[...]
    avg = (bd * (1.0 / HEAD_DIM)).astype(BF16)
    s = state[...]
    for ci in range(cps):
        rows = slice(ci * c, (ci + 1) * c)
        q, k, v = q_ref[rows, :], k_ref[rows, :], v_ref[rows, :]
        kb, vb = k.astype(BF16), v.astype(BF16)
        att = jnp.concatenate([_nt_dot(jnp.where(grp == h, q, 0.0).astype(BF16), kb) * dm_ref[h] for h in heads],
                              axis=1)
        v_cat = jnp.concatenate([jnp.where(grp == h, v, 0.0).astype(BF16) for h in heads], axis=0)
        o = (jnp.dot(att.astype(BF16), v_cat, preferred_element_type=F32)
             + jnp.dot(q.astype(BF16), s.astype(BF16), preferred_element_type=F32) * qd_ref[...])
        s = s * cd_ref[...] + bd * jnp.dot((k * kd_ref[...]).T.astype(BF16), vb, preferred_element_type=F32)
        mu = jnp.dot(o.astype(BF16), avg, preferred_element_type=F32)
        oc = o - mu
        var = jnp.dot((oc * oc).astype(BF16), avg, preferred_element_type=F32)
        o_ref[rows, :] = jax.nn.silu(g_ref[rows, :]) * (oc * lax.rsqrt(var + EPS) * gn_ref[...])
    state[...] = s

    @pl.when(step == pl.num_programs(1) - 1)
    def _():
        s_ref[...] = s


def _retention_rows(proj, cols, b, l, gn_g, c, cps):
    h, dh = N_HEADS, HEAD_DIM
    lg = jnp.log(1.0 - 2.0 ** (-5.0 - jnp.arange(h, dtype=F32)))
    j = jnp.arange(c, dtype=F32)
    diff = j[:, None] - j[None, :]
    dmat = jnp.where(diff[None] >= 0, jnp.exp(jnp.maximum(diff, 0.0)[None] * lg[:, None, None]), 0.0)
    lanes = lambda x: jnp.repeat(x, dh, axis=-1)
    q_dec = lanes(jnp.exp((j[:, None] + 1.0) * lg[None, :]))
    k_dec = lanes(jnp.exp((c - 1.0 - j)[:, None] * lg[None, :]))
    chunk_dec = jnp.broadcast_to(lanes(jnp.exp(c * lg)[None, :]), (MIX, MIX))
    head_of = jnp.arange(MIX) // dh
    bd = (head_of[:, None] == head_of[None, :]).astype(F32)
    r = c * cps
    per = l // r
    col = lambda g: pl.BlockSpec((r, MIX), lambda bi, i: (bi * per + i, g))
    const = lambda *shape: pl.BlockSpec(shape, lambda bi, i: (0,) * len(shape))
    state = pl.BlockSpec((None, MIX, MIX), lambda bi, i: (bi, 0, 0))
    od, s_bd = pl.pallas_call(
        functools.partial(_retention_rows_kernel, c=c, cps=cps),
        out_shape=(jax.ShapeDtypeStruct((b * l, MIX), F32), jax.ShapeDtypeStruct((b, MIX, MIX), F32)),
        grid=(b, per),
        in_specs=[col(cols[0]), col(cols[1]), col(cols[2]), col(cols[3]), state, const(h, c, c),
                  const(c, MIX), const(c, MIX), const(MIX, MIX), const(MIX, MIX), const(1, MIX)],
        out_specs=(col(0), state),
        scratch_shapes=[pltpu.VMEM((MIX, MIX), F32)],
        compiler_params=_params("parallel", "arbitrary"),
        name="retention_rows",
    )(proj, proj, proj, proj, jnp.zeros((b, MIX, MIX), F32), dmat, q_dec, k_dec, chunk_dec, bd, gn_g[None])
    s5 = s_bd.reshape(b, h, dh, h, dh)
    return od, jnp.stack([s5[:, i, :, i, :] for i in range(h)], axis=1)


def _merge_kernel(h_ref, g_ref, oa_ref, ob_ref, oc_ref, od_ref, wm_ref, wb_ref, wo_ref, o_ref):
    h = h_ref[...]
    d = h.shape[1]
    xn = _rms(h, g_ref[...]).astype(BF16)
    z = None
    for n, br_ref in enumerate((oa_ref, ob_ref, oc_ref, od_ref)):
        gate = jax.nn.sigmoid(jnp.dot(xn, wm_ref[:, n * d:(n + 1) * d], preferred_element_type=F32))
        br = jnp.dot(br_ref[...].astype(BF16), wb_ref[n], preferred_element_type=F32)
        z = gate * br if z is None else z + gate * br
    o_ref[...] = h + jnp.dot(z.astype(BF16), wo_ref[...], preferred_element_type=F32)


def _merge(h, g, branches, w_merge, w_branch, w_out, tm):
    n, d = h.shape
    rows = lambda w: pl.BlockSpec((tm, w), lambda i: (i, 0))
    return pl.pallas_call(
        _merge_kernel,
        out_shape=jax.ShapeDtypeStruct((n, d), F32),
        grid=(n // tm,),
        in_specs=[rows(d), pl.BlockSpec((1, d), lambda i: (0, 0)), rows(MIX), rows(MIX), rows(MIX), rows(MIX),
                  pl.BlockSpec(w_merge.shape, lambda i: (0, 0)),
                  pl.BlockSpec(w_branch.shape, lambda i: (0, 0, 0)),
                  pl.BlockSpec(w_out.shape, lambda i: (0, 0))],
        out_specs=rows(d),
        compiler_params=_params("parallel"),
        name="merge",
    )(h, g, *branches, w_merge, w_branch, w_out)


def _ffn_kernel(h_ref, g_ref, wg_ref, wu_ref, wo_ref, o_ref, xn_ref, acc_ref):
    k = pl.program_id(1)

    @pl.when(k == 0)
    def _():
        xn_ref[...] = _rms(h_ref[...], g_ref[...]).astype(BF16)
        acc_ref[...] = h_ref[...]

    xn = xn_ref[...]
    gate = jnp.dot(xn, wg_ref[...], preferred_element_type=F32)
    up = jnp.dot(xn, wu_ref[...], preferred_element_type=F32)
    act = (jax.nn.silu(gate) * up).astype(BF16)
    acc_ref[...] += jnp.dot(act, wo_ref[...], preferred_element_type=F32)

    @pl.when(k == pl.num_programs(1) - 1)
    def _():
        o_ref[...] = acc_ref[...]


def _ffn(h, g, w_in, w_out, tm, tf):
    n, d = h.shape
    d_ff = w_out.shape[0]
    nk = d_ff // tf
    return pl.pallas_call(
        _ffn_kernel,
        out_shape=jax.ShapeDtypeStruct((n, d), F32),
        grid=(n // tm, nk),
        in_specs=[pl.BlockSpec((tm, d), lambda i, k: (i, 0)), pl.BlockSpec((1, d), lambda i, k: (0, 0)),
                  pl.BlockSpec((d, tf), lambda i, k: (0, k)), pl.BlockSpec((d, tf), lambda i, k: (0, nk + k)),
                  pl.BlockSpec((tf, d), lambda i, k: (k, 0))],
        out_specs=pl.BlockSpec((tm, d), lambda i, k: (i, 0)),
        scratch_shapes=[pltpu.VMEM((tm, d), BF16), pltpu.VMEM((tm, d), F32)],
        compiler_params=_params("parallel", "arbitrary"),
        name="ffn",
    )(h, g, w_in, w_in, w_out)


def _ple_kernel(h_ref, g_ref, p_ref, wg_ref, wp_ref, gf_ref, o_ref, *, final):
    h = h_ref[...]
    xn = _rms(h, g_ref[...]).astype(BF16)
    gate = jax.nn.sigmoid(jnp.dot(xn, wg_ref[...], preferred_element_type=F32))
    emb = jnp.dot(p_ref[...].astype(BF16), wp_ref[...], preferred_element_type=F32)
    out = h + gate * emb
    o_ref[...] = _rms(out, gf_ref[...]) if final else out


def _ple(h, g, p, w_gate, w_proj, g_final, tm, final):
    n, d = h.shape
    dp = p.shape[1]
    vec = pl.BlockSpec((1, d), lambda i: (0, 0))
    return pl.pallas_call(
        functools.partial(_ple_kernel, final=final),
        out_shape=jax.ShapeDtypeStruct((n, d), F32),
        grid=(n // tm,),
        in_specs=[pl.BlockSpec((tm, d), lambda i: (i, 0)), vec, pl.BlockSpec((tm, dp), lambda i: (i, 0)),
                  pl.BlockSpec((d, d), lambda i: (0, 0)), pl.BlockSpec((dp, d), lambda i: (0, 0)), vec],
        out_specs=pl.BlockSpec((tm, d), lambda i: (i, 0)),
        compiler_params=_params("parallel"),
        name="ple",
    )(h, g, p, w_gate, w_proj, g_final)


def _stream_pages(pt_ref, streams, sems, layer, pp, n_chunks, n_seq, reverse, step):
    total = n_seq * n_chunks
    assert total % 2 == 0

    def locate(g):
        b = g // n_chunks
        c = g - b * n_chunks
        return b, (n_chunks - 1 - c if reverse else c)

    def copies(page_of, slot):
        return [pltpu.make_async_copy(hbm.at[layer, page_of(k)], buf.at[slot, k], sems.at[n, slot])
                for k in range(pp) for n, (hbm, buf) in enumerate(streams)]

    def fetch(g, slot):
        b, c = locate(g)
        for cp in copies(lambda k: pt_ref[b, c * pp + k], slot):
            cp.start()

    def pair(i, _):
        for slot in (0, 1):
            g = 2 * i + slot

            @pl.when(g + 1 < total)
            def _():
                fetch(g + 1, 1 - slot)

            for cp in copies(lambda k: 0, slot):
                cp.wait()
            step(g, *locate(g), slot)
        return 0

    fetch(0, 0)
    lax.fori_loop(0, total // 2, pair, 0)


def _paged_call(body, page_table, caches, dense, out_shape, scratch, name, page_shapes):
    whole = lambda x: pl.BlockSpec(x.shape, lambda i, pt, n=len(x.shape): (0,) * n)
    grid_spec = pltpu.PrefetchScalarGridSpec(
        num_scalar_prefetch=1, grid=(1,),
        in_specs=[pl.BlockSpec(memory_space=pl.ANY)] * len(caches) + [whole(x) for x in dense],
        out_specs=whole(out_shape),
        scratch_shapes=[pltpu.VMEM((2, PAGES_PER_STEP) + shape, F32) for shape in page_shapes]
        + [pltpu.SemaphoreType.DMA((len(caches), 2))] + scratch)
    return pl.pallas_call(body, out_shape=out_shape, grid_spec=grid_spec,
                          compiler_params=_params("arbitrary"), name=name)(page_table, *caches, *dense)


def _kmean_sample_kernel(pt_ref, k_hbm, o_ref, kbuf, sems, *, layer, pp, n_chunks):
    per_block = MOBA_BLOCK // PAGE_SIZE
    nblk = pp // per_block
    lane = lax.broadcasted_iota(jnp.int32, (MIX, LANES), 1)

    def step(g, b, c, slot):
        @pl.when(c == 0)
        def _():
            o_ref[b] = jnp.zeros((MIX, LANES), F32)

        acc = o_ref[b]
        for blk in range(nblk):
            tot = kbuf[slot, blk * per_block]
            for k in range(1, per_block):
                tot = tot + kbuf[slot, blk * per_block + k]
            mean = jnp.sum(tot, axis=1, keepdims=True) * (1.0 / MOBA_BLOCK)
            acc = jnp.where(lane == c * nblk + blk, mean, acc)
        o_ref[b] = acc

    _stream_pages(pt_ref, [(k_hbm, kbuf)], sems, layer, pp, n_chunks, o_ref.shape[0], False, step)


def _kmean_sample(cache_k, layer, page_table):
    db, n_pages = page_table.shape
    pp = PAGES_PER_STEP
    assert n_pages * PAGE_SIZE // MOBA_BLOCK <= LANES
    return _paged_call(
        functools.partial(_kmean_sample_kernel, layer=layer, pp=pp, n_chunks=n_pages // pp),
        page_table, [cache_k], [], jax.ShapeDtypeStruct((db, MIX, LANES), F32), [], "kmean_sample",
        [(MIX, PAGE_SIZE)])


def _lane_suffix_sum(x):
    n = x.shape[-1]
    lane = lax.broadcasted_iota(jnp.int32, x.shape, x.ndim - 1)
    s = 1
    while s < n:
        x = x + jnp.where(lane < n - s, pltpu.roll(x, n - s, x.ndim - 1), 0.0)
        s *= 2
    return x


def _expand_heads(x, t):
    return jnp.concatenate([jnp.broadcast_to(x[h:h + 1, :], (t, x.shape[1])) for h in range(N_HEADS)], axis=0)


def _decode_attn_kernel(pt_ref, *refs, layer, pp, n_chunks, t, fox, nb):
    if fox:
        (k_hbm, v_hbm, f_hbm, q_ref, kn_ref, vn_ref, fn_ref, o_ref, kbuf, vbuf, fbuf, sems,
         s_ref, m_ref, l_ref, acc_ref, cq_ref, tail_ref) = refs
        streams = [(k_hbm, kbuf), (v_hbm, vbuf), (f_hbm, fbuf)]
    else:
        (k_hbm, v_hbm, q_ref, kn_ref, vn_ref, km_ref, o_ref, kbuf, vbuf, sems,
         s_ref, m_ref, l_ref, acc_ref, sel_ref) = refs
        streams = [(k_hbm, kbuf), (v_hbm, vbuf)]
    r = N_HEADS * t

    def step(g, b, c, slot):
        first = c == (n_chunks - 1 if fox else 0)
        last = c == (0 if fox else n_chunks - 1)
        q = q_ref[b]
        qb = (q * SCALE).astype(BF16)

        @pl.when(first)
        def _():
            s = jnp.dot(qb, kn_ref[b].astype(BF16), preferred_element_type=F32)
            key = lax.broadcasted_iota(jnp.int32, (r, LANES), 1)
            qi = lax.broadcasted_iota(jnp.int32, (r, LANES), 0) % t
            if fox:
                cn = _expand_heads(_lane_cumsum(fn_ref[b]), t)
                cq = jnp.sum(jnp.where(key == qi, cn, 0.0), axis=1, keepdims=True)
                cq_ref[...] = jnp.broadcast_to(cq, cq_ref.shape)
                tail_ref[...] = jnp.zeros(tail_ref.shape, F32)
                s = s + cq - cn
            s = jnp.where(key <= qi, s, NEG_INF)
            m = jnp.max(s, axis=1, keepdims=True)
            p = jnp.exp(s - m)
            m_ref[...] = jnp.broadcast_to(m, m_ref.shape)
            l_ref[...] = jnp.broadcast_to(jnp.sum(p, axis=1, keepdims=True), l_ref.shape)
            acc_ref[...] = _nt_dot(p.astype(BF16), vn_ref[b].astype(BF16))
            if not fox:
                gate = jnp.dot(q, km_ref[b], precision=lax.Precision.HIGHEST, preferred_element_type=F32)
                blk = lax.broadcasted_iota(jnp.int32, gate.shape, 1)
                sel_ref[...] = _topk_select(jnp.where(blk < nb, gate, NEG_INF), min(MOBA_TOPK, nb), 1)

        for k in range(pp):
            s_ref[:, k * PAGE_SIZE:(k + 1) * PAGE_SIZE] = jnp.dot(qb, kbuf[slot, k].astype(BF16),
                                                                  preferred_element_type=F32)
        s = s_ref[...]
        if fox:
            f = jnp.concatenate([fbuf[slot, k] for k in range(pp)], axis=1)
            suffix = _lane_suffix_sum(f)
            tail = tail_ref[:, :1]
            s = s + jnp.concatenate([cq_ref[...]] * pp, axis=1) + _expand_heads(suffix - f + tail, t)
            tail_ref[...] = jnp.broadcast_to(tail + suffix[:, :1], tail_ref.shape)
        else:
            per_block = MOBA_BLOCK // PAGE_SIZE
            nblk = pp // per_block
            sel = sel_ref[...]
            blk = lax.broadcasted_iota(jnp.int32, sel.shape, 1)
            cols = []
            for j in range(nblk):
                chosen = jnp.sum(jnp.where(blk == c * nblk + j, sel, 0.0), axis=1, keepdims=True)
                cols.append(jnp.broadcast_to(chosen, (r, MOBA_BLOCK)))
            s = jnp.where(jnp.concatenate(cols, axis=1) > 0.5, s, NEG_INF)
        m = m_ref[:, :1]
        m_new = jnp.maximum(m, jnp.max(s, axis=1, keepdims=True))
        alpha = jnp.exp(m - m_new)
        p = jnp.exp(s - m_new).astype(BF16)
        l_new = alpha * l_ref[:, :1] + jnp.sum(p.astype(F32), axis=1, keepdims=True)
        acc = alpha * acc_ref[...]
        for k in range(pp):
            acc = acc + _nt_dot(p[:, k * PAGE_SIZE:(k + 1) * PAGE_SIZE], vbuf[slot, k].astype(BF16))
        m_ref[...] = jnp.broadcast_to(m_new, m_ref.shape)
        l_ref[...] = jnp.broadcast_to(l_new, l_ref.shape)
        acc_ref[...] = acc

        @pl.when(last)
        def _():
            o = acc / l_new
            grp = lax.broadcasted_iota(jnp.int32, (t, MIX), 1) // HEAD_DIM
            out = jnp.zeros((t, MIX), F32)
            for h in range(N_HEADS):
                out = out + jnp.where(grp == h, o[h * t:(h + 1) * t, :], 0.0)
            o_ref[b] = out

    _stream_pages(pt_ref, streams, sems, layer, pp, n_chunks, q_ref.shape[0], fox, step)


def _decode_attn(cache_k, cache_v, layer, page_table, q_blk, k_new, v_new, extra, fox):
    db, n_pages = page_table.shape
    pp = PAGES_PER_STEP
    r = q_blk.shape[1]
    t = r // N_HEADS
    scratch = [pltpu.VMEM((r, pp * PAGE_SIZE), F32), pltpu.VMEM((r, LANES), F32), pltpu.VMEM((r, LANES), F32),
               pltpu.VMEM((r, MIX), F32), pltpu.VMEM((r, LANES), F32)]
    caches, pages = [cache_k, cache_v], [(MIX, PAGE_SIZE), (MIX, PAGE_SIZE)]
    if fox:
        caches, pages = caches + [extra[0]], pages + [(N_HEADS, PAGE_SIZE)]
        scratch += [pltpu.VMEM((N_HEADS, LANES), F32)]
    return _paged_call(
        functools.partial(_decode_attn_kernel, layer=layer, pp=pp, n_chunks=n_pages // pp, t=t, fox=fox,
                          nb=n_pages * PAGE_SIZE // MOBA_BLOCK),
        page_table, caches, [q_blk, k_new, v_new, extra[-1]], jax.ShapeDtypeStruct((db, t, MIX), F32), scratch,
        "fox_sample" if fox else "moba_sample", pages)


def _to_heads(x, b, l):
    return x.reshape(b, l, N_HEADS, HEAD_DIM).transpose(0, 2, 1, 3)


def _from_heads(x):
    b, h, l, dh = x.shape
    return x.transpose(0, 2, 1, 3).reshape(b * l, h * dh)


def _split_proj(proj):
    cols = [proj[:, g * MIX:(g + 1) * MIX] for g in range(N_PROJ_GROUPS)]
    logf = proj[:, N_PROJ_GROUPS * MIX:N_PROJ_GROUPS * MIX + N_HEADS]
    return cols, logf


def _tail(h, p, prm, branches, final, tm):
    h = _merge(h, prm["norm_mix"], branches, prm["w_merge"], prm["w_branch"], prm["w_out"], tm)
    d_ff = prm["w_ffn_out"].shape[0]
    tf = d_ff // 2 if (d_ff // 2) % LANES == 0 else d_ff
    h = _ffn(h, prm["norm_ffn"], prm["w_ffn_in"], prm["w_ffn_out"], min(512, h.shape[0]), tf)
    return _ple(h, prm["norm_ple"], p, prm["w_ple_gate"], prm["w_ple_proj"], prm["norm_final"], tm, final)


def _prompt_layer(h, p, prm, tabs, b, l, final):
    tm = 256
    proj, tr, vtb = _proj(h, prm["norm_mix"], prm["w_in"], prm["b_forget"], prm["ln_g"], prm["ln_b"], tabs, tm,
                          batch=b)
    logf = proj[:, N_PROJ_GROUPS * MIX:N_PROJ_GROUPS * MIX + N_HEADS]
    kmean = _kmean_prompt(proj, 1, b, l).reshape(b, -1, N_HEADS, HEAD_DIM).transpose(0, 2, 1, 3)
    oa = _attn_prompt(tr, 0, vtb, 0, kmean, fox=False)
    c = _cumsum_prompt(logf.reshape(b, l, N_HEADS).transpose(0, 2, 1).reshape(b * N_HEADS, l))
    ob = _attn_prompt(tr, 3, vtb, 1, c.reshape(b, N_HEADS, l), fox=True)
    oc = _gmlp(proj, 6, 7, prm["gmlp_ws"], prm["gmlp_bs"], GMLP_CHUNK)
    cps = 8 if l % (8 * RET_CHUNK) == 0 else 1
    od, s_new = _retention_rows(proj, (8, 9, 10, 11), b, l, prm["ret_gn_g"], RET_CHUNK, cps)
    h = _tail(h, p, prm, (oa, ob, oc, od), final, tm)
    heads = lambda g: tr[:, g].reshape(b, N_HEADS, HEAD_DIM, l).transpose(0, 3, 1, 2)
    rows = (heads(1), heads(2), heads(4), heads(5), logf.reshape(b, l, N_HEADS), s_new)
    return h, rows


def _block_diag_q(q, db, t):
    grp = jnp.arange(MIX) // HEAD_DIM
    mask = (grp[None, None, :] == jnp.arange(N_HEADS)[:, None, None]).astype(F32)
    return (q.reshape(db, 1, t, MIX) * mask[None]).reshape(db, N_HEADS * t, MIX)


def _new_keys_t(x, db, t):
    return jnp.pad(x.reshape(db, t, MIX).transpose(0, 2, 1), ((0, 0), (0, 0), (0, LANES - t)))


def _sample_layer(h, p, prm, tabs, caches, layer, page_table, db, t, final):
    cache_k_a, cache_v_a, cache_k_b, cache_v_b, cache_logf_t, state = caches
    n = db * t
    (proj,) = _proj(h, prm["norm_mix"], prm["w_in"], prm["b_forget"], prm["ln_g"], prm["ln_b"], tabs, n)
    (qa, ka, va, qb, kb, vb, _, vn, qd, kd, vd, gd), logf = _split_proj(proj)
    kmean = _kmean_sample(cache_k_a, layer, page_table)
    oa = _decode_attn(cache_k_a, cache_v_a, layer, page_table, _block_diag_q(qa, db, t),
                      _new_keys_t(ka, db, t), _new_keys_t(va, db, t), (kmean,), fox=False)
    logf_new = jnp.pad(logf.reshape(db, t, N_HEADS).transpose(0, 2, 1), ((0, 0), (0, 0), (0, LANES - t)))
    ob = _decode_attn(cache_k_b, cache_v_b, layer, page_table, _block_diag_q(qb, db, t),
                      _new_keys_t(kb, db, t), _new_keys_t(vb, db, t), (cache_logf_t, logf_new), fox=True)
    oc = _gmlp(proj, 6, 7, prm["gmlp_ws"][:, :t, :t], prm["gmlp_bs"][:, :t], t)
    hd = lambda x: _to_heads(x, db, t)
    od, s_new = _retention(hd(qd), hd(kd).transpose(0, 1, 3, 2), hd(vd), hd(gd), state[layer],
                           prm["ret_gn_g"], t, 1)
    h = _tail(h, p, prm, (oa.reshape(n, MIX), ob.reshape(n, MIX), oc, _from_heads(od)), final, n)
    rows = (ka.reshape(db, t, N_HEADS, HEAD_DIM), va.reshape(db, t, N_HEADS, HEAD_DIM),
            kb.reshape(db, t, N_HEADS, HEAD_DIM), vb.reshape(db, t, N_HEADS, HEAD_DIM),
            logf.reshape(db, t, N_HEADS), s_new, vn.reshape(db, t, MIX))
    return h, rows


def kernel(x_prompt, x_sample, cache_k_a, cache_v_a, cache_k_b, cache_v_b, cache_logf_b, state_ret, page_table, p_prompt, p_sample, norm_mix, w_in, b_forget, gmlp_ln_g, gmlp_ln_b, gmlp_ws, gmlp_bs, ret_gn_g, w_branch, w_merge, w_out, norm_ffn, w_ffn_in, w_ffn_out, norm_ple, w_ple_gate, w_ple_proj, norm_final):
    b, l, d = x_prompt.shape
    db, t, _ = x_sample.shape
    depth = w_in.shape[0]
    n_pages = page_table.shape[1]
    past_len = n_pages * PAGE_SIZE
    assert d == N_HEADS * MIX and l % MOBA_BLOCK == 0
    assert past_len % MOBA_BLOCK == 0 and t <= LANES and t % 8 == 0 and t < MOBA_BLOCK
    assert n_pages % PAGES_PER_STEP == 0

    pos_p = jnp.arange(l, dtype=jnp.int32)
    pos_s = jnp.tile(past_len + jnp.arange(t, dtype=jnp.int32), db)
    tabs_p = _rope_tables(pos_p, ROT_DIMS, ROPE_THETA) + _rope_tables(pos_p, HEAD_DIM, RET_THETA)
    tabs_s = _rope_tables(pos_s, ROT_DIMS, ROPE_THETA) + _rope_tables(pos_s, HEAD_DIM, RET_THETA)

    pool = cache_k_a.shape[1]
    pages = lambda x: x.transpose(0, 1, 3, 4, 2).reshape(depth, pool, MIX, PAGE_SIZE)
    caches = (pages(cache_k_a), pages(cache_v_a), pages(cache_k_b), pages(cache_v_b),
              cache_logf_b.transpose(0, 1, 3, 2), state_ret)

    hp = x_prompt.reshape(b * l, d)
    hs = x_sample.reshape(db * t, d)
    rows_p, rows_s = [], []
    f_lo, f_hi = 6 * MIX, 6 * MIX + N_HEADS
    w_in_pad = jnp.concatenate(
        [w_in[:, :, :f_lo].astype(BF16), w_in[:, :, f_hi:].astype(BF16),
         jnp.pad(w_in[:, :, f_lo:f_hi].astype(BF16), ((0, 0), (0, 0), (0, LANES - N_HEADS)))], axis=2)
    for i in range(depth):
        prm = dict(
            norm_mix=norm_mix[i][None], w_in=w_in_pad[i],
            b_forget=jnp.pad(b_forget[i], (0, LANES - N_HEADS))[None],
            ln_g=gmlp_ln_g[i][None], ln_b=gmlp_ln_b[i][None], gmlp_ws=gmlp_ws[i], gmlp_bs=gmlp_bs[i],
            ret_gn_g=ret_gn_g[i], w_branch=w_branch[i].astype(BF16), w_merge=w_merge[i].astype(BF16),
            w_out=w_out[i].astype(BF16), norm_ffn=norm_ffn[i][None], w_ffn_in=w_ffn_in[i].astype(BF16),
            w_ffn_out=w_ffn_out[i].astype(BF16), norm_ple=norm_ple[i][None],
            w_ple_gate=w_ple_gate[i].astype(BF16), w_ple_proj=w_ple_proj[i].astype(BF16),
            norm_final=norm_final[None])
        final = i == depth - 1
        hp, rp = _prompt_layer(hp, p_prompt[i].reshape(b * l, -1), prm, tabs_p, b, l, final)
        hs, rs = _sample_layer(hs, p_sample[i].reshape(db * t, -1), prm, tabs_s, caches, i, page_table,
                               db, t, final)
        rows_p.append(rp)
        rows_s.append(rs)
    stack = lambda rows, j: jnp.stack([r[j] for r in rows], axis=0)
    return ((hp.reshape(b, l, d), hs.reshape(db, t, d))
            + tuple(stack(rows_p, j) for j in range(6)) + tuple(stack(rows_s, j) for j in range(7)))
```

```python
import functools

import jax
import jax.numpy as jnp
from jax import lax
from jax.experimental import pallas as pl
from jax.experimental.pallas import tpu as pltpu

F32 = jnp.float32
BF16 = jnp.bfloat16

N_HEADS = 4
HEAD_DIM = 64
MIX = N_HEADS * HEAD_DIM
ROT_DIMS = HEAD_DIM // 4
ROPE_THETA = 500000.0
RET_THETA = 10000.0
MOBA_BLOCK = 256
MOBA_TOPK = 3
GMLP_CHUNK = 128
RET_CHUNK = 128
PAGE_SIZE = 128
EPS = 1e-6
SCALE = HEAD_DIM ** -0.5
NEG_INF = float("-inf")

LANES = 128
VMEM_LIMIT = 56 * 1024 * 1024
PAGES_PER_STEP = 32
N_PROJ_GROUPS = 12
N_ATTN_GROUPS = 6
W_IN_PAD = N_PROJ_GROUPS * MIX + LANES
LOG2E = 1.4426950408889634
AUG = 8
FFN_SLAB = 256
KV_BLOCKS_PER_TRIP = 4


def _params(*sem):
    return pltpu.CompilerParams(dimension_semantics=sem, vmem_limit_bytes=VMEM_LIMIT)


def _rms(x, g):
    return x * lax.rsqrt(jnp.mean(x * x, axis=-1, keepdims=True) + EPS) * g


def _nt_dot(a, b, precision=None):
    return lax.dot_general(a, b, (((1,), (1,)), ((), ())), precision=precision,
                           preferred_element_type=F32)


def _log_sigmoid(x):
    return jnp.minimum(x, 0.0) - jnp.log(1.0 + jnp.exp(-jnp.abs(x)))


def _proj_kernel(h_ref, g_ref, w_ref, bf_ref, lng_ref, lnb_ref, ca_ref, sa_ref, cd_ref, sd_ref, o_ref,
                 *t_refs):
    xn = _rms(h_ref[...], g_ref[...]).astype(BF16)
    lane = lax.broadcasted_iota(jnp.int32, (1, MIX), 1) & (HEAD_DIM - 1)

    def rope(y, c_ref, s_ref, half):
        c = jnp.concatenate([c_ref[...], c_ref[...]], axis=1)
        s = jnp.concatenate([s_ref[...], s_ref[...]], axis=1)
        rot = jnp.where(lane < half, pltpu.roll(y, MIX - half, 1), pltpu.roll(y, half, 1))
        return y * c + rot * s

    for grp in range(N_PROJ_GROUPS):
        y = jnp.dot(xn, w_ref[:, grp * MIX:(grp + 1) * MIX], preferred_element_type=F32)
        if grp in (0, 1):
            y = rope(y, ca_ref, sa_ref, ROT_DIMS // 2)
        elif grp == 6:
            y = jax.nn.gelu(y)
        elif grp == 7:
            y = jax.nn.gelu(y)
            mu = jnp.mean(y, axis=-1, keepdims=True)
            yc = y - mu
            var = jnp.mean(yc * yc, axis=-1, keepdims=True)
            y = yc * lax.rsqrt(var + EPS) * lng_ref[...] + lnb_ref[...]
        elif grp == 8:
            y = rope(y, cd_ref, sd_ref, HEAD_DIM // 2)
        elif grp == 9:
            y = rope(y, cd_ref, sd_ref, HEAD_DIM // 2) * SCALE
        o_ref[:, grp * MIX:(grp + 1) * MIX] = y
        if t_refs and grp < N_ATTN_GROUPS:
            yt = y.T
            t_refs[0][grp] = yt
            if grp % 3 == 2:
                t_refs[1][grp // 3] = yt.astype(BF16)
    f = jnp.dot(xn, w_ref[:, N_PROJ_GROUPS * MIX:], preferred_element_type=F32) + bf_ref[...]
    o_ref[:, N_PROJ_GROUPS * MIX:] = _log_sigmoid(f)


def _proj(h, g, w_pad, bf_pad, ln_g, ln_b, tabs, tm, batch=None):
    n, d = h.shape
    ca, sa, cd, sd = tabs
    tab_blocks = ca.shape[0] // tm
    full = lambda shape: pl.BlockSpec(shape, lambda i: (0, 0))
    tab = pl.BlockSpec((tm, LANES), lambda i: (i % tab_blocks, 0))
    out_shape = [jax.ShapeDtypeStruct((n, W_IN_PAD), F32)]
    out_specs = [pl.BlockSpec((tm, W_IN_PAD), lambda i: (i, 0))]
    if batch is not None:
        l = n // batch
        per_seq = l // tm
        t_map = lambda i: (i // per_seq, 0, 0, i % per_seq)
        out_shape += [jax.ShapeDtypeStruct((batch, N_ATTN_GROUPS, MIX, l), F32),
                      jax.ShapeDtypeStruct((batch, N_ATTN_GROUPS // 3, MIX, l), BF16)]
        out_specs += [pl.BlockSpec((None, N_ATTN_GROUPS, MIX, tm), t_map),
                      pl.BlockSpec((None, N_ATTN_GROUPS // 3, MIX, tm), t_map)]
    return pl.pallas_call(
        _proj_kernel,
        out_shape=out_shape,
        grid=(n // tm,),
        in_specs=[pl.BlockSpec((tm, d), lambda i: (i, 0)), full((1, d)), full((d, W_IN_PAD)),
                  full((1, LANES)), full((1, MIX)), full((1, MIX)), tab, tab, tab, tab],
        out_specs=out_specs,
        compiler_params=_params("parallel"),
        name="proj",
    )(h, g, w_pad, bf_pad, ln_g, ln_b, ca, sa, cd, sd)


def _rope_tables(pos, n_rot, theta):
    half = n_rot // 2
    inv = theta ** (-jnp.arange(half, dtype=F32) / half)
    ang = pos.astype(F32)[:, None] * inv[None, :]
    cos, sin = jnp.cos(ang), jnp.sin(ang)
    rest = HEAD_DIM - n_rot
    ones = jnp.ones((pos.shape[0], rest), F32)
    zeros = jnp.zeros((pos.shape[0], rest), F32)
    c = jnp.concatenate([cos, cos, ones], axis=1)
    s = jnp.concatenate([-sin, sin, zeros], axis=1)
    reps = LANES // HEAD_DIM
    return jnp.tile(c, (1, reps)), jnp.tile(s, (1, reps))


def _kmean_kernel(k_ref, o_ref, *, nblk):
    x = k_ref[...].reshape(nblk, MOBA_BLOCK, MIX)
    o_ref[...] = jnp.sum(x, axis=1) * (1.0 / MOBA_BLOCK)


def _kmean_prompt(proj, col, b, l):
    nblk = 8
    assert l % (nblk * MOBA_BLOCK) == 0
    per_seq = l // (nblk * MOBA_BLOCK)
    return pl.pallas_call(
        functools.partial(_kmean_kernel, nblk=nblk),
        out_shape=jax.ShapeDtypeStruct((b, l // MOBA_BLOCK, MIX), F32),
        grid=(b, per_seq),
        in_specs=[pl.BlockSpec((nblk * MOBA_BLOCK, MIX), lambda i, j: (i * per_seq + j, col))],
        out_specs=pl.BlockSpec((None, nblk, MIX), lambda i, j: (i, j, 0)),
        compiler_params=_params("parallel", "parallel"),
        name="kmean_prompt",
    )(proj)


def _lane_cumsum(x):
    n = x.shape[-1]
    lane = lax.broadcasted_iota(jnp.int32, x.shape, x.ndim - 1)
    s = 1
    while s < n:
        x = x + jnp.where(lane >= s, pltpu.roll(x, s, x.ndim - 1), 0.0)
        s *= 2
    return x


def _cumsum_kernel(x_ref, o_ref):
    o_ref[...] = _lane_cumsum(x_ref[...])


def _cumsum_prompt(logf_t):
    r, l = logf_t.shape
    return pl.pallas_call(
        _cumsum_kernel,
        out_shape=jax.ShapeDtypeStruct((r, l), F32),
        grid=(1,),
        in_specs=[pl.BlockSpec((r, l), lambda i: (0, 0))],
        out_specs=pl.BlockSpec((r, l), lambda i: (0, 0)),
        compiler_params=_params("arbitrary"),
        name="cumsum_prompt",
    )(logf_t)


def _softmax_first(s, v):
    m = jnp.max(s, axis=1, keepdims=True)
    p = jnp.exp(s - m)
    l = jnp.sum(p, axis=1, keepdims=True)
    acc = jnp.dot(p.astype(BF16), v, preferred_element_type=F32)
    return m, l, acc


def _softmax_update(carry, s, v):
    m, l, acc = carry
    m_new = jnp.maximum(m, jnp.max(s, axis=1, keepdims=True))
    alpha = jnp.exp(m - m_new)
    p = jnp.exp(s - m_new)
    l = alpha * l + jnp.sum(p, axis=1, keepdims=True)
    acc = alpha * acc + jnp.dot(p.astype(BF16), v, preferred_element_type=F32)
    return m_new, l, acc


def _topk_select(gate, n_top, axis):
    pos = lax.broadcasted_iota(jnp.int32, gate.shape, axis)
    sel = jnp.zeros(gate.shape, F32)
    g = gate
    for _ in range(n_top):
        mx = jnp.max(g, axis=axis, keepdims=True)
        idx = jnp.min(jnp.where(g == mx, pos, gate.shape[axis]), axis=axis, keepdims=True)
        hit = pos == idx
        sel = jnp.maximum(sel, jnp.where(hit, jnp.where(mx > NEG_INF, 1.0, 0.0), 0.0))
        g = jnp.where(hit, NEG_INF, g)
    return sel


def _split3(x):
    hi = x.astype(BF16).astype(F32)
    r = x - hi
    mid = r.astype(BF16).astype(F32)
    return hi, mid, r - mid


def _attn_prompt_kernel(qt_ref, kt_ref, vt_ref, x_ref, o_ref, k_ref, qa_ref, *sel_refs, fox):
    t = MOBA_BLOCK
    i = pl.program_id(1)
    own = pl.multiple_of(i * t, t)
    causal = (lax.broadcasted_iota(jnp.int32, (t, t), 0) <= lax.broadcasted_iota(jnp.int32, (t, t), 1))
    pad = jnp.zeros((HEAD_DIM - AUG, t), F32)
    one = jnp.ones((1, t), F32)
    zero = jnp.zeros((1, t), F32)
    heads = range(N_HEADS)

    @pl.when(i == 0)
    def _():
        def build(jt, _):
            st = pl.multiple_of(jt * t, t)
            for h in heads:
                if fox:
                    hi, mid, lo = _split3(x_ref[h:h + 1, pl.ds(st, t)] * LOG2E)
                    aug = jnp.concatenate([one, one, one, -hi, -mid, -lo, zero, zero], axis=0)
                else:
                    aug = jnp.zeros((AUG, t), F32)
                kh = jnp.concatenate([kt_ref[h * HEAD_DIM:(h + 1) * HEAD_DIM, pl.ds(st, t)], aug, pad], axis=0)
                k_ref[h, pl.ds(st, t), :] = kh.T.astype(BF16)
            return 0

        lax.fori_loop(0, kt_ref.shape[1] // t, build, 0)

    for h in heads:
        qh = qt_ref[h * HEAD_DIM:(h + 1) * HEAD_DIM, :]
        if fox:
            aug = jnp.concatenate(_split3(x_ref[h:h + 1, pl.ds(own, t)] * LOG2E) + (one, one, one, zero, zero),
                                  axis=0)
        else:
            aug = jnp.zeros((AUG, t), F32)
            nb = x_ref.shape[1]
            gate = jnp.dot(x_ref[h], qh, precision=lax.Precision.HIGHEST, preferred_element_type=F32)
            blk = lax.broadcasted_iota(jnp.int32, (nb, t), 0)
            sel_refs[0][h] = _topk_select(jnp.where(blk < i, gate, NEG_INF), min(MOBA_TOPK, nb), 0)
        qa_ref[h] = jnp.concatenate([qh * (SCALE * LOG2E), aug, pad], axis=0).astype(BF16)

    def attend(carry, start, n, mask):
        ss = [mask(h, jnp.dot(k_ref[h, pl.ds(start, n * t), :], qa_ref[h], preferred_element_type=F32))
              for h in heads]
        ms = [jnp.max(s, axis=0, keepdims=True) for s in ss]
        if carry is not None:
            ms = [jnp.maximum(carry[3 * h], ms[h]) for h in heads]
        ps = [jnp.exp2(ss[h] - ms[h]) for h in heads]
        pvs = [jnp.dot(vt_ref[h * HEAD_DIM:(h + 1) * HEAD_DIM, pl.ds(start, n * t)], ps[h].astype(BF16),
                       preferred_element_type=F32) for h in heads]
        out = []
        for h in heads:
            l = jnp.sum(ps[h], axis=0, keepdims=True)
            if carry is None:
                out += [ms[h], l, pvs[h]]
            else:
                alpha = jnp.exp2(carry[3 * h] - ms[h])
                out += [ms[h], alpha * carry[3 * h + 1] + l, alpha * carry[3 * h + 2] + pvs[h]]
        return tuple(out)

    def past(n):
        def body(j, carry):
            def mask(h, s):
                if fox:
                    return s
                keep = jnp.concatenate([jnp.broadcast_to(sel_refs[0][h, pl.ds(j * n + k, 1), :], (t, t))
                                        for k in range(n)], axis=0)
                return jnp.where(keep > 0.5, s, NEG_INF)

            return attend(carry, pl.multiple_of(j * (n * t), n * t), n, mask)

        return body

    carry = attend(None, own, 1, lambda h, s: jnp.where(causal, s, NEG_INF))
    n, done = KV_BLOCKS_PER_TRIP, 0
    while n >= 1:
        trips = (i - done) // n
        carry = lax.fori_loop(done // n, done // n + trips, past(n), carry)
        done = done + trips * n
        n //= 2
    ot = jnp.concatenate([carry[3 * h + 2] / carry[3 * h + 1] for h in heads], axis=0)
    o_ref[...] = ot.T


def _attn_prompt(tr, grp, vtb, which, x, fox):
    b, _, _, l = tr.shape
    nb = l // MOBA_BLOCK
    t = MOBA_BLOCK
    if fox:
        x_spec = pl.BlockSpec((None, N_HEADS, l), lambda bi, i: (bi, 0, 0))
        scratch = []
    else:
        x_spec = pl.BlockSpec((None, N_HEADS, nb, HEAD_DIM), lambda bi, i: (bi, 0, 0, 0))
        scratch = [pltpu.VMEM((N_HEADS, nb, t), F32)]
    return pl.pallas_call(
        functools.partial(_attn_prompt_kernel, fox=fox),
        out_shape=jax.ShapeDtypeStruct((b * l, MIX), F32),
        grid=(b, nb),
        in_specs=[pl.BlockSpec((None, None, MIX, t), lambda bi, i: (bi, grp, 0, i)),
                  pl.BlockSpec((None, None, MIX, l), lambda bi, i: (bi, grp + 1, 0, 0)),
                  pl.BlockSpec((None, None, MIX, l), lambda bi, i: (bi, which, 0, 0)),
                  x_spec],
        out_specs=pl.BlockSpec((t, MIX), lambda bi, i: (bi * nb + i, 0)),
        scratch_shapes=[pltpu.VMEM((N_HEADS, l, 2 * HEAD_DIM), BF16),
                        pltpu.VMEM((N_HEADS, 2 * HEAD_DIM, t), BF16)] + scratch,
        compiler_params=_params("parallel", "arbitrary"),
        name="fox_prompt" if fox else "moba_prompt",
    )(tr, tr, vtb, x)


def _gmlp_kernel(u_ref, v_ref, w_ref, b_ref, o_ref, *, c, cps):
    grp = lax.broadcasted_iota(jnp.int32, (c, MIX), 1) // HEAD_DIM
    row = lax.broadcasted_iota(jnp.int32, (c, c), 0)
    col = lax.broadcasted_iota(jnp.int32, (c, c), 1)
    w_cat = jnp.concatenate([jnp.where(col <= row, w_ref[g], 0.0) for g in range(N_HEADS)], axis=1).astype(BF16)
    bias = b_ref[...]
    for ci in range(cps):
        rows = slice(ci * c, (ci + 1) * c)
        v = v_ref[rows, :]
        v_cat = jnp.concatenate([jnp.where(grp == g, v, 0.0) for g in range(N_HEADS)], axis=0)
        mixed = jnp.dot(w_cat, v_cat.astype(BF16), preferred_element_type=F32)
        o_ref[rows, :] = u_ref[rows, :] * (mixed + bias)


def _gmlp(proj, u_col, v_col, ws, bs, c):
    n = proj.shape[0]
    bias = jnp.repeat(bs.T, HEAD_DIM, axis=1)
    cps = 8 if n % (8 * c) == 0 else 1
    col = lambda g: pl.BlockSpec((cps * c, MIX), lambda i: (i, g))
    return pl.pallas_call(
        functools.partial(_gmlp_kernel, c=c, cps=cps),
        out_shape=jax.ShapeDtypeStruct((n, MIX), F32),
        grid=(n // (cps * c),),
        in_specs=[col(u_col), col(v_col), pl.BlockSpec((N_HEADS, c, c), lambda i: (0, 0, 0)),
                  pl.BlockSpec((c, MIX), lambda i: (0, 0))],
        out_specs=col(0),
        compiler_params=_params("parallel"),
        name="gmlp",
    )(proj, proj, ws, bias)


def _retention_kernel(q_ref, kt_ref, v_ref, g_ref, s0_ref, dm_ref, qd_ref, kd_ref, cd_ref, gn_ref,
                      o_ref, s_ref, state, *, c, cps):
    step = pl.program_id(1)

    @pl.when(step == 0)
    def _():
        state[...] = s0_ref[...]

    for h in range(N_HEADS):
        s = state[h]
        for ci in range(cps):
            rows = slice(ci * c, (ci + 1) * c)
            qc = q_ref[h, rows, :].astype(BF16)
            vc = v_ref[h, rows, :].astype(BF16)
            kt = kt_ref[h, :, rows]
            att = jnp.dot(qc, kt.astype(BF16), preferred_element_type=F32) * dm_ref[h]
            o = (jnp.dot(att.astype(BF16), vc, preferred_element_type=F32)
                 + jnp.dot(qc, s.astype(BF16), preferred_element_type=F32) * qd_ref[h])
            s = s * cd_ref[h] + jnp.dot((kt * kd_ref[h]).astype(BF16), vc, preferred_element_type=F32)
            mu = jnp.mean(o, axis=-1, keepdims=True)
            oc = o - mu
            var = jnp.mean(oc * oc, axis=-1, keepdims=True)
            o_ref[h, rows, :] = jax.nn.silu(g_ref[h, rows, :]) * (oc * lax.rsqrt(var + EPS) * gn_ref[h])
        state[h] = s

        @pl.when(step == pl.num_programs(1) - 1)
        def _():
            s_ref[h] = s


def _retention(q, kt, v, g, s0, gn_g, c, cps):
    b, h, l, dh = q.shape
    lg = jnp.log(1.0 - 2.0 ** (-5.0 - jnp.arange(h, dtype=F32)))
    j = jnp.arange(c, dtype=F32)
    diff = j[:, None] - j[None, :]
    dmat = jnp.where(diff[None] >= 0, jnp.exp(jnp.maximum(diff, 0.0)[None] * lg[:, None, None]), 0.0)
    q_dec = jnp.broadcast_to(jnp.exp((j[None, :] + 1.0) * lg[:, None])[:, :, None], (h, c, dh))
    k_dec = jnp.broadcast_to(jnp.exp((c - 1.0 - j)[None, :] * lg[:, None])[:, None, :], (h, dh, c))
    chunk_dec = jnp.broadcast_to(jnp.exp(c * lg)[:, None, None], (h, dh, dh))
    r = c * cps
    rows = pl.BlockSpec((None, h, r, dh), lambda bi, i: (bi, 0, i, 0))
    per_b = pl.BlockSpec((None, h, dh, dh), lambda bi, i: (bi, 0, 0, 0))
    table = lambda s1, s2: pl.BlockSpec((h, s1, s2), lambda bi, i: (0, 0, 0))
    return pl.pallas_call(
        functools.partial(_retention_kernel, c=c, cps=cps),
        out_shape=(jax.ShapeDtypeStruct((b, h, l, dh), F32), jax.ShapeDtypeStruct((b, h, dh, dh), F32)),
        grid=(b, l // r),
        in_specs=[rows, pl.BlockSpec((None, h, dh, r), lambda bi, i: (bi, 0, 0, i)), rows, rows,
                  per_b, table(c, c), table(c, dh), table(dh, c), table(dh, dh), table(1, dh)],
        out_specs=(rows, per_b),
        scratch_shapes=[pltpu.VMEM((h, dh, dh), F32)],
        compiler_params=_params("parallel", "arbitrary"),
        name="retention",
    )(q, kt, v, g, s0, dmat, q_dec, k_dec, chunk_dec, gn_g.reshape(h, 1, dh))


def _retention_rows_kernel(q_ref, k_ref, v_ref, g_ref, s0_ref, dm_ref, qd_ref, kd_ref, cd_ref, bd_ref, gn_ref,
                           o_ref, s_ref, state, *, c, cps):
    step = pl.program_id(1)

    @pl.when(step == 0)
    def _():
        state[...] = s0_ref[...]

    heads = range(N_HEADS)
    grp = lax.broadcasted_iota(jnp.int32, (c, MIX), 1) // HEAD_DIM
    bd = bd_ref[...]
    avg = (bd * (1.0 / HEAD_DIM)).astype(BF16)
    s = state[...]
    for ci in range(cps):
        rows = slice(ci * c, (ci + 1) * c)
        q, k, v = q_ref[rows, :], k_ref[rows, :], v_ref[rows, :]
        kb, vb = k.astype(BF16), v.astype(BF16)
        att = jnp.concatenate([_nt_dot(jnp.where(grp == h, q, 0.0).astype(BF16), kb) * dm_ref[h] for h in heads],
                              axis=1)
        v_cat = jnp.concatenate([jnp.where(grp == h, v, 0.0).astype(BF16) for h in heads], axis=0)
        o = (jnp.dot(att.astype(BF16), v_cat, preferred_element_type=F32)
             + jnp.dot(q.astype(BF16), s.astype(BF16), preferred_element_type=F32) * qd_ref[...])
        s = s * cd_ref[...] + bd * jnp.dot((k * kd_ref[...]).T.astype(BF16), vb, preferred_element_type=F32)
        mu = jnp.dot(o.astype(BF16), avg, preferred_element_type=F32)
        oc = o - mu
        var = jnp.dot((oc * oc).astype(BF16), avg, preferred_element_type=F32)
        o_ref[rows, :] = jax.nn.silu(g_ref[rows, :]) * (oc * lax.rsqrt(var + EPS) * gn_ref[...])
    state[...] = s

    @pl.when(step == pl.num_programs(1) - 1)
    def _():
        s_ref[...] = s


def _retention_rows(proj, cols, b, l, gn_g, c, cps):
    h, dh = N_HEADS, HEAD_DIM
    lg = jnp.log(1.0 - 2.0 ** (-5.0 - jnp.arange(h, dtype=F32)))
    j = jnp.arange(c, dtype=F32)
    diff = j[:, None] - j[None, :]
    dmat = jnp.where(diff[None] >= 0, jnp.exp(jnp.maximum(diff, 0.0)[None] * lg[:, None, None]), 0.0)
    lanes = lambda x: jnp.repeat(x, dh, axis=-1)
    q_dec = lanes(jnp.exp((j[:, None] + 1.0) * lg[None, :]))
    k_dec = lanes(jnp.exp((c - 1.0 - j)[:, None] * lg[None, :]))
    chunk_dec = jnp.broadcast_to(lanes(jnp.exp(c * lg)[None, :]), (MIX, MIX))
    head_of = jnp.arange(MIX) // dh
    bd = (head_of[:, None] == head_of[None, :]).astype(F32)
    r = c * cps
    per = l // r
    col = lambda g: pl.BlockSpec((r, MIX), lambda bi, i: (bi * per + i, g))
    const = lambda *shape: pl.BlockSpec(shape, lambda bi, i: (0,) * len(shape))
    state = pl.BlockSpec((None, MIX, MIX), lambda bi, i: (bi, 0, 0))
    od, s_bd = pl.pallas_call(
        functools.partial(_retention_rows_kernel, c=c, cps=cps),
        out_shape=(jax.ShapeDtypeStruct((b * l, MIX), F32), jax.ShapeDtypeStruct((b, MIX, MIX), F32)),
        grid=(b, per),
        in_specs=[col(cols[0]), col(cols[1]), col(cols[2]), col(cols[3]), state, const(h, c, c),
                  const(c, MIX), const(c, MIX), const(MIX, MIX), const(MIX, MIX), const(1, MIX)],
        out_specs=(col(0), state),
        scratch_shapes=[pltpu.VMEM((MIX, MIX), F32)],
        compiler_params=_params("parallel", "arbitrary"),
        name="retention_rows",
    )(proj, proj, proj, proj, jnp.zeros((b, MIX, MIX), F32), dmat, q_dec, k_dec, chunk_dec, bd, gn_g[None])
    s5 = s_bd.reshape(b, h, dh, h, dh)
    return od, jnp.stack([s5[:, i, :, i, :] for i in range(h)], axis=1)


def _merge_kernel(h_ref, g_ref, oa_ref, ob_ref, oc_ref, od_ref, wm_ref, wb_ref, wo_ref, o_ref):
    h = h_ref[...]
    d = h.shape[1]
    xn = _rms(h, g_ref[...]).astype(BF16)
    z = None
    for n, br_ref in enumerate((oa_ref, ob_ref, oc_ref, od_ref)):
        gate = jax.nn.sigmoid(jnp.dot(xn, wm_ref[:, n * d:(n + 1) * d], preferred_element_type=F32))
        br = jnp.dot(br_ref[...].astype(BF16), wb_ref[n], preferred_element_type=F32)
        z = gate * br if z is None else z + gate * br
    o_ref[...] = h + jnp.dot(z.astype(BF16), wo_ref[...], preferred_element_type=F32)


def _merge(h, g, branches, w_merge, w_branch, w_out, tm):
    n, d = h.shape
    rows = lambda w: pl.BlockSpec((tm, w), lambda i: (i, 0))
    return pl.pallas_call(
        _merge_kernel,
        out_shape=jax.ShapeDtypeStruct((n, d), F32),
        grid=(n // tm,),
        in_specs=[rows(d), pl.BlockSpec((1, d), lambda i: (0, 0)), rows(MIX), rows(MIX), rows(MIX), rows(MIX),
                  pl.BlockSpec(w_merge.shape, lambda i: (0, 0)),
                  pl.BlockSpec(w_branch.shape, lambda i: (0, 0, 0)),
                  pl.BlockSpec(w_out.shape, lambda i: (0, 0))],
        out_specs=rows(d),
        compiler_params=_params("parallel"),
        name="merge",
    )(h, g, *branches, w_merge, w_branch, w_out)


def _ffn_kernel(h_ref, g_ref, wg_ref, wu_ref, wo_ref, o_ref, xn_ref, acc_ref):
    k = pl.program_id(1)

    @pl.when(k == 0)
    def _():
        xn_ref[...] = _rms(h_ref[...], g_ref[...]).astype(BF16)
        acc_ref[...] = h_ref[...]

    xn = xn_ref[...]
    tf = wg_ref.shape[1]
    slab = FFN_SLAB if tf % FFN_SLAB == 0 else tf
    part = None
    for lo in range(0, tf, slab):
        gate = jnp.dot(xn, wg_ref[:, lo:lo + slab], preferred_element_type=F32)
        up = jnp.dot(xn, wu_ref[:, lo:lo + slab], preferred_element_type=F32)
        act = (jax.nn.silu(gate) * up).astype(BF16)
        out = jnp.dot(act, wo_ref[lo:lo + slab, :], preferred_element_type=F32)
        part = out if part is None else part + out
    acc_ref[...] += part

    @pl.when(k == pl.num_programs(1) - 1)
    def _():
        o_ref[...] = acc_ref[...]


def _ffn(h, g, w_in, w_out, tm, tf):
    n, d = h.shape
    d_ff = w_out.shape[0]
    nk = d_ff // tf
    return pl.pallas_call(
        _ffn_kernel,
        out_shape=jax.ShapeDtypeStruct((n, d), F32),
        grid=(n // tm, nk),
        in_specs=[pl.BlockSpec((tm, d), lambda i, k: (i, 0)), pl.BlockSpec((1, d), lambda i, k: (0, 0)),
                  pl.BlockSpec((d, tf), lambda i, k: (0, k)), pl.BlockSpec((d, tf), lambda i, k: (0, nk + k)),
                  pl.BlockSpec((tf, d), lambda i, k: (k, 0))],
        out_specs=pl.BlockSpec((tm, d), lambda i, k: (i, 0)),
        scratch_shapes=[pltpu.VMEM((tm, d), BF16), pltpu.VMEM((tm, d), F32)],
        compiler_params=_params("parallel", "arbitrary"),
        name="ffn",
    )(h, g, w_in, w_in, w_out)


def _ple_kernel(h_ref, g_ref, p_ref, wg_ref, wp_ref, gf_ref, o_ref, *, final):
    h = h_ref[...]
    xn = _rms(h, g_ref[...]).astype(BF16)
    gate = jax.nn.sigmoid(jnp.dot(xn, wg_ref[...], preferred_element_type=F32))
    emb = jnp.dot(p_ref[...].astype(BF16), wp_ref[...], preferred_element_type=F32)
    out = h + gate * emb
    o_ref[...] = _rms(out, gf_ref[...]) if final else out


def _ple(h, g, p, w_gate, w_proj, g_final, tm, final):
    n, d = h.shape
    p, layer = p
    dp = p.shape[2]
    vec = pl.BlockSpec((1, d), lambda i: (0, 0))
    return pl.pallas_call(
        functools.partial(_ple_kernel, final=final),
        out_shape=jax.ShapeDtypeStruct((n, d), F32),
        grid=(n // tm,),
        in_specs=[pl.BlockSpec((tm, d), lambda i: (i, 0)), vec, pl.BlockSpec((None, tm, dp), lambda i: (layer, i, 0)),
                  pl.BlockSpec((d, d), lambda i: (0, 0)), pl.BlockSpec((dp, d), lambda i: (0, 0)), vec],
        out_specs=pl.BlockSpec((tm, d), lambda i: (i, 0)),
        compiler_params=_params("parallel"),
        name="ple",
    )(h, g, p, w_gate, w_proj, g_final)


def _stream_pages(pt_ref, streams, sems, layer, pp, n_chunks, n_seq, reverse, step):
    total = n_seq * n_chunks
    assert total % 2 == 0

    def locate(g):
        b = g // n_chunks
        c = g - b * n_chunks
        return b, (n_chunks - 1 - c if reverse else c)

    def copies(page_of, slot):
        return [pltpu.make_async_copy(hbm.at[layer, page_of(k)], buf.at[slot, k], sems.at[n, slot])
                for k in range(pp) for n, (hbm, buf) in enumerate(streams)]

    def fetch(g, slot):
        b, c = locate(g)
        for cp in copies(lambda k: pt_ref[b, c * pp + k], slot):
            cp.start()

    def pair(i, _):
        for slot in (0, 1):
            g = 2 * i + slot

            @pl.when(g + 1 < total)
            def _():
                fetch(g + 1, 1 - slot)

            for cp in copies(lambda k: 0, slot):
                cp.wait()
            step(g, *locate(g), slot)
        return 0

    fetch(0, 0)
    lax.fori_loop(0, total // 2, pair, 0)


def _paged_call(body, page_table, caches, dense, out_shape, scratch, name, page_shapes):
    whole = lambda x: pl.BlockSpec(x.shape, lambda i, pt, n=len(x.shape): (0,) * n)
    grid_spec = pltpu.PrefetchScalarGridSpec(
        num_scalar_prefetch=1, grid=(1,),
        in_specs=[pl.BlockSpec(memory_space=pl.ANY)] * len(caches) + [whole(x) for x in dense],
        out_specs=whole(out_shape),
        scratch_shapes=[pltpu.VMEM((2, PAGES_PER_STEP) + shape, F32) for shape in page_shapes]
        + [pltpu.SemaphoreType.DMA((len(caches), 2))] + scratch)
    return pl.pallas_call(body, out_shape=out_shape, grid_spec=grid_spec,
                          compiler_params=_params("arbitrary"), name=name)(page_table, *caches, *dense)


def _kmean_sample_kernel(pt_ref, k_hbm, o_ref, kbuf, sems, *, layer, pp, n_chunks):
    per_block = MOBA_BLOCK // PAGE_SIZE
    nblk = pp // per_block
    lane = lax.broadcasted_iota(jnp.int32, (MIX, LANES), 1)

    def step(g, b, c, slot):
        @pl.when(c == 0)
        def _():
            o_ref[b] = jnp.zeros((MIX, LANES), F32)

        acc = o_ref[b]
        for blk in range(nblk):
            tot = kbuf[slot, blk * per_block]
            for k in range(1, per_block):
                tot = tot + kbuf[slot, blk * per_block + k]
            mean = jnp.sum(tot, axis=1, keepdims=True) * (1.0 / MOBA_BLOCK)
            acc = jnp.where(lane == c * nblk + blk, mean, acc)
        o_ref[b] = acc

    _stream_pages(pt_ref, [(k_hbm, kbuf)], sems, layer, pp, n_chunks, o_ref.shape[0], False, step)


def _kmean_sample(cache_k, layer, page_table):
    db, n_pages = page_table.shape
    pp = PAGES_PER_STEP
    assert n_pages * PAGE_SIZE // MOBA_BLOCK <= LANES
    return _paged_call(
        functools.partial(_kmean_sample_kernel, layer=layer, pp=pp, n_chunks=n_pages // pp),
        page_table, [cache_k], [], jax.ShapeDtypeStruct((db, MIX, LANES), F32), [], "kmean_sample",
        [(MIX, PAGE_SIZE)])


def _lane_suffix_sum(x):
    n = x.shape[-1]
    lane = lax.broadcasted_iota(jnp.int32, x.shape, x.ndim - 1)
    s = 1
    while s < n:
        x = x + jnp.where(lane < n - s, pltpu.roll(x, n - s, x.ndim - 1), 0.0)
        s *= 2
    return x


def _expand_heads(x, t):
    return jnp.concatenate([jnp.broadcast_to(x[h:h + 1, :], (t, x.shape[1])) for h in range(N_HEADS)], axis=0)


def _decode_attn_kernel(pt_ref, *refs, layer, pp, n_chunks, t, fox, nb):
    if fox:
        (k_hbm, v_hbm, f_hbm, q_ref, kn_ref, vn_ref, fn_ref, o_ref, kbuf, vbuf, fbuf, sems,
         s_ref, m_ref, l_ref, acc_ref, cq_ref, tail_ref) = refs
        streams = [(k_hbm, kbuf), (v_hbm, vbuf), (f_hbm, fbuf)]
    else:
        (k_hbm, v_hbm, q_ref, kn_ref, vn_ref, km_ref, o_ref, kbuf, vbuf, sems,
         s_ref, m_ref, l_ref, acc_ref, sel_ref) = refs
        streams = [(k_hbm, kbuf), (v_hbm, vbuf)]
    r = N_HEADS * t

    def step(g, b, c, slot):
        first = c == (n_chunks - 1 if fox else 0)
        last = c == (0 if fox else n_chunks - 1)
        q = q_ref[b]
        qb = (q * SCALE).astype(BF16)

        @pl.when(first)
        def _():
            s = jnp.dot(qb, kn_ref[b].astype(BF16), preferred_element_type=F32)
            key = lax.broadcasted_iota(jnp.int32, (r, LANES), 1)
            qi = lax.broadcasted_iota(jnp.int32, (r, LANES), 0) % t
            if fox:
                cn = _expand_heads(_lane_cumsum(fn_ref[b]), t)
                cq = jnp.sum(jnp.where(key == qi, cn, 0.0), axis=1, keepdims=True)
                cq_ref[...] = jnp.broadcast_to(cq, cq_ref.shape)
                tail_ref[...] = jnp.zeros(tail_ref.shape, F32)
                s = s + cq - cn
            s = jnp.where(key <= qi, s, NEG_INF)
            m = jnp.max(s, axis=1, keepdims=True)
            p = jnp.exp(s - m)
            m_ref[...] = jnp.broadcast_to(m, m_ref.shape)
            l_ref[...] = jnp.broadcast_to(jnp.sum(p, axis=1, keepdims=True), l_ref.shape)
            acc_ref[...] = _nt_dot(p.astype(BF16), vn_ref[b].astype(BF16))
            if not fox:
                gate = jnp.dot(q, km_ref[b], precision=lax.Precision.HIGHEST, preferred_element_type=F32)
                blk = lax.broadcasted_iota(jnp.int32, gate.shape, 1)
                sel_ref[...] = _topk_select(jnp.where(blk < nb, gate, NEG_INF), min(MOBA_TOPK, nb), 1)

        for k in range(pp):
            s_ref[:, k * PAGE_SIZE:(k + 1) * PAGE_SIZE] = jnp.dot(qb, kbuf[slot, k].astype(BF16),
                                                                  preferred_element_type=F32)
        s = s_ref[...]
        if fox:
            f = jnp.concatenate([fbuf[slot, k] for k in range(pp)], axis=1)
            suffix = _lane_suffix_sum(f)
            tail = tail_ref[:, :1]
            s = s + jnp.concatenate([cq_ref[...]] * pp, axis=1) + _expand_heads(suffix - f + tail, t)
            tail_ref[...] = jnp.broadcast_to(tail + suffix[:, :1], tail_ref.shape)
        else:
            per_block = MOBA_BLOCK // PAGE_SIZE
            nblk = pp // per_block
            sel = sel_ref[...]
            blk = lax.broadcasted_iota(jnp.int32, sel.shape, 1)
            cols = []
            for j in range(nblk):
                chosen = jnp.sum(jnp.where(blk == c * nblk + j, sel, 0.0), axis=1, keepdims=True)
                cols.append(jnp.broadcast_to(chosen, (r, MOBA_BLOCK)))
            s = jnp.where(jnp.concatenate(cols, axis=1) > 0.5, s, NEG_INF)
        m = m_ref[:, :1]
        m_new = jnp.maximum(m, jnp.max(s, axis=1, keepdims=True))
        alpha = jnp.exp(m - m_new)
        p = jnp.exp(s - m_new).astype(BF16)
        l_new = alpha * l_ref[:, :1] + jnp.sum(p.astype(F32), axis=1, keepdims=True)
        acc = alpha * acc_ref[...]
        for k in range(pp):
            acc = acc + _nt_dot(p[:, k * PAGE_SIZE:(k + 1) * PAGE_SIZE], vbuf[slot, k].astype(BF16))
        m_ref[...] = jnp.broadcast_to(m_new, m_ref.shape)
        l_ref[...] = jnp.broadcast_to(l_new, l_ref.shape)
        acc_ref[...] = acc

        @pl.when(last)
        def _():
            o = acc / l_new
            grp = lax.broadcasted_iota(jnp.int32, (t, MIX), 1) // HEAD_DIM
            out = jnp.zeros((t, MIX), F32)
            for h in range(N_HEADS):
                out = out + jnp.where(grp == h, o[h * t:(h + 1) * t, :], 0.0)
            o_ref[b] = out

    _stream_pages(pt_ref, streams, sems, layer, pp, n_chunks, q_ref.shape[0], fox, step)


def _decode_attn(cache_k, cache_v, layer, page_table, q_blk, k_new, v_new, extra, fox):
    db, n_pages = page_table.shape
    pp = PAGES_PER_STEP
    r = q_blk.shape[1]
    t = r // N_HEADS
    scratch = [pltpu.VMEM((r, pp * PAGE_SIZE), F32), pltpu.VMEM((r, LANES), F32), pltpu.VMEM((r, LANES), F32),
               pltpu.VMEM((r, MIX), F32), pltpu.VMEM((r, LANES), F32)]
    caches, pages = [cache_k, cache_v], [(MIX, PAGE_SIZE), (MIX, PAGE_SIZE)]
    if fox:
        caches, pages = caches + [extra[0]], pages + [(N_HEADS, PAGE_SIZE)]
        scratch += [pltpu.VMEM((N_HEADS, LANES), F32)]
    return _paged_call(
        functools.partial(_decode_attn_kernel, layer=layer, pp=pp, n_chunks=n_pages // pp, t=t, fox=fox,
                          nb=n_pages * PAGE_SIZE // MOBA_BLOCK),
        page_table, caches, [q_blk, k_new, v_new, extra[-1]], jax.ShapeDtypeStruct((db, t, MIX), F32), scratch,
        "fox_sample" if fox else "moba_sample", pages)


def _to_heads(x, b, l):
    return x.reshape(b, l, N_HEADS, HEAD_DIM).transpose(0, 2, 1, 3)


def _from_heads(x):
    b, h, l, dh = x.shape
    return x.transpose(0, 2, 1, 3).reshape(b * l, h * dh)


def _split_proj(proj):
    cols = [proj[:, g * MIX:(g + 1) * MIX] for g in range(N_PROJ_GROUPS)]
    logf = proj[:, N_PROJ_GROUPS * MIX:N_PROJ_GROUPS * MIX + N_HEADS]
    return cols, logf


def _tail(h, p, prm, branches, final, tm):
    h = _merge(h, prm["norm_mix"], branches, prm["w_merge"], prm["w_branch"], prm["w_out"], tm)
    d_ff = prm["w_ffn_out"].shape[0]
    tf = d_ff
    h = _ffn(h, prm["norm_ffn"], prm["w_ffn_in"], prm["w_ffn_out"], min(512, h.shape[0]), tf)
    return _ple(h, prm["norm_ple"], p, prm["w_ple_gate"], prm["w_ple_proj"], prm["norm_final"], tm, final)


def _prompt_layer(h, p, prm, tabs, b, l, final):
    tm = 256
    proj, tr, vtb = _proj(h, prm["norm_mix"], prm["w_in"], prm["b_forget"], prm["ln_g"], prm["ln_b"], tabs, tm,
                          batch=b)
    logf = proj[:, N_PROJ_GROUPS * MIX:N_PROJ_GROUPS * MIX + N_HEADS]
    kmean = _kmean_prompt(proj, 1, b, l).reshape(b, -1, N_HEADS, HEAD_DIM).transpose(0, 2, 1, 3)
    oa = _attn_prompt(tr, 0, vtb, 0, kmean, fox=False)
    c = _cumsum_prompt(logf.reshape(b, l, N_HEADS).transpose(0, 2, 1).reshape(b * N_HEADS, l))
    ob = _attn_prompt(tr, 3, vtb, 1, c.reshape(b, N_HEADS, l), fox=True)
    oc = _gmlp(proj, 6, 7, prm["gmlp_ws"], prm["gmlp_bs"], GMLP_CHUNK)
    cps = 8 if l % (8 * RET_CHUNK) == 0 else 1
    od, s_new = _retention_rows(proj, (8, 9, 10, 11), b, l, prm["ret_gn_g"], RET_CHUNK, cps)
    h = _tail(h, p, prm, (oa, ob, oc, od), final, tm)
    heads = lambda g: tr[:, g].reshape(b, N_HEADS, HEAD_DIM, l).transpose(0, 3, 1, 2)
    rows = (heads(1), heads(2), heads(4), heads(5), logf.reshape(b, l, N_HEADS), s_new)
    return h, rows


def _block_diag_q(q, db, t):
    grp = jnp.arange(MIX) // HEAD_DIM
    mask = (grp[None, None, :] == jnp.arange(N_HEADS)[:, None, None]).astype(F32)
    return (q.reshape(db, 1, t, MIX) * mask[None]).reshape(db, N_HEADS * t, MIX)


def _new_keys_t(x, db, t):
    return jnp.pad(x.reshape(db, t, MIX).transpose(0, 2, 1), ((0, 0), (0, 0), (0, LANES - t)))


def _sample_layer(h, p, prm, tabs, caches, layer, page_table, db, t, final):
    cache_k_a, cache_v_a, cache_k_b, cache_v_b, cache_logf_t, state = caches
    n = db * t
    (proj,) = _proj(h, prm["norm_mix"], prm["w_in"], prm["b_forget"], prm["ln_g"], prm["ln_b"], tabs, n)
    (qa, ka, va, qb, kb, vb, _, vn, qd, kd, vd, gd), logf = _split_proj(proj)
    kmean = _kmean_sample(cache_k_a, layer, page_table)
    oa = _decode_attn(cache_k_a, cache_v_a, layer, page_table, _block_diag_q(qa, db, t),
                      _new_keys_t(ka, db, t), _new_keys_t(va, db, t), (kmean,), fox=False)
    logf_new = jnp.pad(logf.reshape(db, t, N_HEADS).transpose(0, 2, 1), ((0, 0), (0, 0), (0, LANES - t)))
    ob = _decode_attn(cache_k_b, cache_v_b, layer, page_table, _block_diag_q(qb, db, t),
                      _new_keys_t(kb, db, t), _new_keys_t(vb, db, t), (cache_logf_t, logf_new), fox=True)
    oc = _gmlp(proj, 6, 7, prm["gmlp_ws"][:, :t, :t], prm["gmlp_bs"][:, :t], t)
    hd = lambda x: _to_heads(x, db, t)
    od, s_new = _retention(hd(qd), hd(kd).transpose(0, 1, 3, 2), hd(vd), hd(gd), state[layer],
                           prm["ret_gn_g"], t, 1)
    h = _tail(h, p, prm, (oa.reshape(n, MIX), ob.reshape(n, MIX), oc, _from_heads(od)), final, n)
    rows = (ka.reshape(db, t, N_HEADS, HEAD_DIM), va.reshape(db, t, N_HEADS, HEAD_DIM),
            kb.reshape(db, t, N_HEADS, HEAD_DIM), vb.reshape(db, t, N_HEADS, HEAD_DIM),
            logf.reshape(db, t, N_HEADS), s_new, vn.reshape(db, t, MIX))
    return h, rows


def kernel(x_prompt, x_sample, cache_k_a, cache_v_a, cache_k_b, cache_v_b, cache_logf_b, state_ret, page_table, p_prompt, p_sample, norm_mix, w_in, b_forget, gmlp_ln_g, gmlp_ln_b, gmlp_ws, gmlp_bs, ret_gn_g, w_branch, w_merge, w_out, norm_ffn, w_ffn_in, w_ffn_out, norm_ple, w_ple_gate, w_ple_proj, norm_final):
    b, l, d = x_prompt.shape
    db, t, _ = x_sample.shape
    depth = w_in.shape[0]
    n_pages = page_table.shape[1]
    past_len = n_pages * PAGE_SIZE
    assert d == N_HEADS * MIX and l % MOBA_BLOCK == 0
    assert past_len % MOBA_BLOCK == 0 and t <= LANES and t % 8 == 0 and t < MOBA_BLOCK
    assert n_pages % PAGES_PER_STEP == 0

    pos_p = jnp.arange(l, dtype=jnp.int32)
    pos_s = jnp.tile(past_len + jnp.arange(t, dtype=jnp.int32), db)
    tabs_p = _rope_tables(pos_p, ROT_DIMS, ROPE_THETA) + _rope_tables(pos_p, HEAD_DIM, RET_THETA)
    tabs_s = _rope_tables(pos_s, ROT_DIMS, ROPE_THETA) + _rope_tables(pos_s, HEAD_DIM, RET_THETA)

    pool = cache_k_a.shape[1]
    pages = lambda x: x.transpose(0, 1, 3, 4, 2).reshape(depth, pool, MIX, PAGE_SIZE)
    caches = (pages(cache_k_a), pages(cache_v_a), pages(cache_k_b), pages(cache_v_b),
              cache_logf_b.transpose(0, 1, 3, 2), state_ret)

    hp = x_prompt.reshape(b * l, d)
    hs = x_sample.reshape(db * t, d)
    rows_p, rows_s = [], []
    f_lo, f_hi = 6 * MIX, 6 * MIX + N_HEADS
    w_in_pad = jnp.concatenate(
        [w_in[:, :, :f_lo].astype(BF16), w_in[:, :, f_hi:].astype(BF16),
         jnp.pad(w_in[:, :, f_lo:f_hi].astype(BF16), ((0, 0), (0, 0), (0, LANES - N_HEADS)))], axis=2)
    for i in range(depth):
        prm = dict(
            norm_mix=norm_mix[i][None], w_in=w_in_pad[i],
            b_forget=jnp.pad(b_forget[i], (0, LANES - N_HEADS))[None],
            ln_g=gmlp_ln_g[i][None], ln_b=gmlp_ln_b[i][None], gmlp_ws=gmlp_ws[i], gmlp_bs=gmlp_bs[i],
            ret_gn_g=ret_gn_g[i], w_branch=w_branch[i].astype(BF16), w_merge=w_merge[i].astype(BF16),
            w_out=w_out[i].astype(BF16), norm_ffn=norm_ffn[i][None], w_ffn_in=w_ffn_in[i].astype(BF16),
            w_ffn_out=w_ffn_out[i].astype(BF16), norm_ple=norm_ple[i][None],
            w_ple_gate=w_ple_gate[i].astype(BF16), w_ple_proj=w_ple_proj[i].astype(BF16),
            norm_final=norm_final[None])
        final = i == depth - 1
        hp, rp = _prompt_layer(hp, (p_prompt.reshape(depth, b * l, -1), i), prm, tabs_p, b, l, final)
        hs, rs = _sample_layer(hs, (p_sample.reshape(depth, db * t, -1), i), prm, tabs_s, caches, i, page_table,
                               db, t, final)
        rows_p.append(rp)
        rows_s.append(rs)
    stack = lambda rows, j: jnp.stack([r[j] for r in rows], axis=0)
    return ((hp.reshape(b, l, d), hs.reshape(db, t, d))
            + tuple(stack(rows_p, j) for j in range(6)) + tuple(stack(rows_s, j) for j in range(7)))
```

```python
import functools

import jax
import jax.numpy as jnp
from jax import lax
from jax.experimental import pallas as pl
from jax.experimental.pallas import tpu as pltpu

F32 = jnp.float32
BF16 = jnp.bfloat16

N_HEADS = 4
HEAD_DIM = 64
MIX = N_HEADS * HEAD_DIM
ROT_DIMS = HEAD_DIM // 4
ROPE_THETA = 500000.0
RET_THETA = 10000.0
MOBA_BLOCK = 256
MOBA_TOPK = 3
GMLP_CHUNK = 128
RET_CHUNK = 128
PAGE_SIZE = 128
EPS = 1e-6
SCALE = HEAD_DIM ** -0.5
NEG_INF = float("-inf")

LANES = 128
VMEM_LIMIT = 56 * 1024 * 1024
PAGES_PER_STEP = 32
N_PROJ_GROUPS = 12
N_ATTN_GROUPS = 6
N_KV_OUT = 4
W_IN_PAD = N_PROJ_GROUPS * MIX + LANES
LOG2E = 1.4426950408889634
AUG = 8
FFN_SLAB = 256
KV_BLOCKS_PER_TRIP = 4


def _params(*sem):
    return pltpu.CompilerParams(dimension_semantics=sem, vmem_limit_bytes=VMEM_LIMIT)


def _rms(x, g):
    return x * lax.rsqrt(jnp.mean(x * x, axis=-1, keepdims=True) + EPS) * g


def _nt_dot(a, b, precision=None):
    return lax.dot_general(a, b, (((1,), (1,)), ((), ())), precision=precision,
                           preferred_element_type=F32)


def _log_sigmoid(x):
    return jnp.minimum(x, 0.0) - jnp.log(1.0 + jnp.exp(-jnp.abs(x)))


def _proj_kernel(h_ref, g_ref, w_ref, bf_ref, lng_ref, lnb_ref, ca_ref, sa_ref, cd_ref, sd_ref, *refs):
    o_ref = refs[0] if len(refs) == 1 else refs[N_KV_OUT]
    t_refs = refs[N_KV_OUT + 1:]
    xn = _rms(h_ref[...], g_ref[...]).astype(BF16)
    lane = lax.broadcasted_iota(jnp.int32, (1, MIX), 1) & (HEAD_DIM - 1)

    def rope(y, c_ref, s_ref, half):
        c = jnp.concatenate([c_ref[...], c_ref[...]], axis=1)
        s = jnp.concatenate([s_ref[...], s_ref[...]], axis=1)
        rot = jnp.where(lane < half, pltpu.roll(y, MIX - half, 1), pltpu.roll(y, half, 1))
        return y * c + rot * s

    for grp in range(N_PROJ_GROUPS):
        y = jnp.dot(xn, w_ref[:, grp * MIX:(grp + 1) * MIX], preferred_element_type=F32)
        if grp in (0, 1):
            y = rope(y, ca_ref, sa_ref, ROT_DIMS // 2)
        elif grp == 6:
            y = jax.nn.gelu(y)
        elif grp == 7:
            y = jax.nn.gelu(y)
            mu = jnp.mean(y, axis=-1, keepdims=True)
            yc = y - mu
            var = jnp.mean(yc * yc, axis=-1, keepdims=True)
            y = yc * lax.rsqrt(var + EPS) * lng_ref[...] + lnb_ref[...]
        elif grp == 8:
            y = rope(y, cd_ref, sd_ref, HEAD_DIM // 2)
        elif grp == 9:
            y = rope(y, cd_ref, sd_ref, HEAD_DIM // 2) * SCALE
        o_ref[:, grp * MIX:(grp + 1) * MIX] = y
        if t_refs and grp < N_ATTN_GROUPS:
            qt_ref, kv_refs, vtb_ref = t_refs[0], t_refs[1:1 + N_KV_OUT], t_refs[1 + N_KV_OUT]
            yt = y.T
            which, kind = divmod(grp, 3)
            if kind == 0:
                qt_ref[which] = yt
            else:
                kv_refs[2 * which + kind - 1][...] = yt
            if kind == 2:
                vtb_ref[which] = yt.astype(BF16)
    f = jnp.dot(xn, w_ref[:, N_PROJ_GROUPS * MIX:], preferred_element_type=F32) + bf_ref[...]
    o_ref[:, N_PROJ_GROUPS * MIX:] = _log_sigmoid(f)


def _proj(h, g, w_pad, bf_pad, ln_g, ln_b, tabs, tm, kv_all=None, layer=None):
    n, d = h.shape
    ca, sa, cd, sd = tabs
    tab_blocks = ca.shape[0] // tm
    full = lambda shape: pl.BlockSpec(shape, lambda i: (0, 0))
    tab = pl.BlockSpec((tm, LANES), lambda i: (i % tab_blocks, 0))
    in_specs = [pl.BlockSpec((tm, d), lambda i: (i, 0)), full((1, d)), full((d, W_IN_PAD)),
                full((1, LANES)), full((1, MIX)), full((1, MIX)), tab, tab, tab, tab]
    out_shape = [jax.ShapeDtypeStruct((n, W_IN_PAD), F32)]
    out_specs = [pl.BlockSpec((tm, W_IN_PAD), lambda i: (i, 0))]
    aliases, extra = {}, ()
    if kv_all is not None:
        _, batch, _, l = kv_all[0].shape
        per_seq = l // tm
        pair = pl.BlockSpec((None, 2, MIX, tm), lambda i: (i // per_seq, 0, 0, i % per_seq))
        aliases = {len(in_specs) + k: 1 + k + 1 for k in range(N_KV_OUT)}
        in_specs += [pl.BlockSpec(memory_space=pl.ANY)] * N_KV_OUT
        out_shape += ([jax.ShapeDtypeStruct((batch, 2, MIX, l), F32)]
                      + [jax.ShapeDtypeStruct(x.shape, F32) for x in kv_all]
                      + [jax.ShapeDtypeStruct((batch, 2, MIX, l), BF16)])
        out_specs += ([pair] + [pl.BlockSpec((None, None, MIX, tm),
                                             lambda i: (layer, i // per_seq, 0, i % per_seq))] * N_KV_OUT + [pair])
        extra = tuple(kv_all)
    return pl.pallas_call(
        _proj_kernel,
        out_shape=out_shape,
        grid=(n // tm,),
        in_specs=in_specs,
        out_specs=out_specs,
        input_output_aliases=aliases,
        compiler_params=_params("parallel"),
        name="proj",
    )(h, g, w_pad, bf_pad, ln_g, ln_b, ca, sa, cd, sd, *extra)


def _rope_tables(pos, n_rot, theta):
    half = n_rot // 2
    inv = theta ** (-jnp.arange(half, dtype=F32) / half)
    ang = pos.astype(F32)[:, None] * inv[None, :]
    cos, sin = jnp.cos(ang), jnp.sin(ang)
    rest = HEAD_DIM - n_rot
    ones = jnp.ones((pos.shape[0], rest), F32)
    zeros = jnp.zeros((pos.shape[0], rest), F32)
    c = jnp.concatenate([cos, cos, ones], axis=1)
    s = jnp.concatenate([-sin, sin, zeros], axis=1)
    reps = LANES // HEAD_DIM
    return jnp.tile(c, (1, reps)), jnp.tile(s, (1, reps))


def _kmean_kernel(k_ref, o_ref, *, nblk):
    x = k_ref[...].reshape(nblk, MOBA_BLOCK, MIX)
    o_ref[...] = jnp.sum(x, axis=1) * (1.0 / MOBA_BLOCK)


def _kmean_prompt(proj, col, b, l):
    nblk = 8
    assert l % (nblk * MOBA_BLOCK) == 0
    per_seq = l // (nblk * MOBA_BLOCK)
    return pl.pallas_call(
        functools.partial(_kmean_kernel, nblk=nblk),
        out_shape=jax.ShapeDtypeStruct((b, l // MOBA_BLOCK, MIX), F32),
        grid=(b, per_seq),
        in_specs=[pl.BlockSpec((nblk * MOBA_BLOCK, MIX), lambda i, j: (i * per_seq + j, col))],
        out_specs=pl.BlockSpec((None, nblk, MIX), lambda i, j: (i, j, 0)),
        compiler_params=_params("parallel", "parallel"),
        name="kmean_prompt",
    )(proj)


def _lane_cumsum(x):
    n = x.shape[-1]
    lane = lax.broadcasted_iota(jnp.int32, x.shape, x.ndim - 1)
    s = 1
    while s < n:
        x = x + jnp.where(lane >= s, pltpu.roll(x, s, x.ndim - 1), 0.0)
        s *= 2
    return x


def _cumsum_kernel(x_ref, o_ref):
    o_ref[...] = _lane_cumsum(x_ref[...])


def _cumsum_prompt(logf_t):
    r, l = logf_t.shape
    return pl.pallas_call(
        _cumsum_kernel,
        out_shape=jax.ShapeDtypeStruct((r, l), F32),
        grid=(1,),
        in_specs=[pl.BlockSpec((r, l), lambda i: (0, 0))],
        out_specs=pl.BlockSpec((r, l), lambda i: (0, 0)),
        compiler_params=_params("arbitrary"),
        name="cumsum_prompt",
    )(logf_t)


def _softmax_first(s, v):
    m = jnp.max(s, axis=1, keepdims=True)
    p = jnp.exp(s - m)
    l = jnp.sum(p, axis=1, keepdims=True)
    acc = jnp.dot(p.astype(BF16), v, preferred_element_type=F32)
    return m, l, acc


def _softmax_update(carry, s, v):
    m, l, acc = carry
    m_new = jnp.maximum(m, jnp.max(s, axis=1, keepdims=True))
    alpha = jnp.exp(m - m_new)
    p = jnp.exp(s - m_new)
    l = alpha * l + jnp.sum(p, axis=1, keepdims=True)
    acc = alpha * acc + jnp.dot(p.astype(BF16), v, preferred_element_type=F32)
    return m_new, l, acc


def _topk_select(gate, n_top, axis):
    pos = lax.broadcasted_iota(jnp.int32, gate.shape, axis)
    sel = jnp.zeros(gate.shape, F32)
    g = gate
    for _ in range(n_top):
        mx = jnp.max(g, axis=axis, keepdims=True)
        idx = jnp.min(jnp.where(g == mx, pos, gate.shape[axis]), axis=axis, keepdims=True)
        hit = pos == idx
        sel = jnp.maximum(sel, jnp.where(hit, jnp.where(mx > NEG_INF, 1.0, 0.0), 0.0))
        g = jnp.where(hit, NEG_INF, g)
    return sel


def _split3(x):
    hi = x.astype(BF16).astype(F32)
    r = x - hi
    mid = r.astype(BF16).astype(F32)
    return hi, mid, r - mid


def _attn_prompt_kernel(qt_ref, kt_ref, vt_ref, x_ref, o_ref, k_ref, qa_ref, *sel_refs, fox):
    t = MOBA_BLOCK
    i = pl.program_id(1)
    own = pl.multiple_of(i * t, t)
    causal = (lax.broadcasted_iota(jnp.int32, (t, t), 0) <= lax.broadcasted_iota(jnp.int32, (t, t), 1))
    pad = jnp.zeros((HEAD_DIM - AUG, t), F32)
    one = jnp.ones((1, t), F32)
    zero = jnp.zeros((1, t), F32)
    heads = range(N_HEADS)

    @pl.when(i == 0)
    def _():
        def build(jt, _):
            st = pl.multiple_of(jt * t, t)
            for h in heads:
                if fox:
                    hi, mid, lo = _split3(x_ref[h:h + 1, pl.ds(st, t)] * LOG2E)
                    aug = jnp.concatenate([one, one, one, -hi, -mid, -lo, zero, zero], axis=0)
                else:
                    aug = jnp.zeros((AUG, t), F32)
                kh = jnp.concatenate([kt_ref[h * HEAD_DIM:(h + 1) * HEAD_DIM, pl.ds(st, t)], aug, pad], axis=0)
                k_ref[h, pl.ds(st, t), :] = kh.T.astype(BF16)
            return 0

        lax.fori_loop(0, kt_ref.shape[1] // t, build, 0)

    for h in heads:
        qh = qt_ref[h * HEAD_DIM:(h + 1) * HEAD_DIM, :]
        if fox:
            aug = jnp.concatenate(_split3(x_ref[h:h + 1, pl.ds(own, t)] * LOG2E) + (one, one, one, zero, zero),
                                  axis=0)
        else:
            aug = jnp.zeros((AUG, t), F32)
            nb = x_ref.shape[1]
            gate = jnp.dot(x_ref[h], qh, precision=lax.Precision.HIGHEST, preferred_element_type=F32)
            blk = lax.broadcasted_iota(jnp.int32, (nb, t), 0)
            sel_refs[0][h] = _topk_select(jnp.where(blk < i, gate, NEG_INF), min(MOBA_TOPK, nb), 0)
        qa_ref[h] = jnp.concatenate([qh * (SCALE * LOG2E), aug, pad], axis=0).astype(BF16)

    def attend(carry, start, n, mask):
        ss = [mask(h, jnp.dot(k_ref[h, pl.ds(start, n * t), :], qa_ref[h], preferred_element_type=F32))
              for h in heads]
        ms = [jnp.max(s, axis=0, keepdims=True) for s in ss]
        if carry is not None:
            ms = [jnp.maximum(carry[3 * h], ms[h]) for h in heads]
        ps = [jnp.exp2(ss[h] - ms[h]) for h in heads]
        pvs = [jnp.dot(vt_ref[h * HEAD_DIM:(h + 1) * HEAD_DIM, pl.ds(start, n * t)], ps[h].astype(BF16),
                       preferred_element_type=F32) for h in heads]
        out = []
        for h in heads:
            l = jnp.sum(ps[h], axis=0, keepdims=True)
            if carry is None:
                out += [ms[h], l, pvs[h]]
            else:
                alpha = jnp.exp2(carry[3 * h] - ms[h])
                out += [ms[h], alpha * carry[3 * h + 1] + l, alpha * carry[3 * h + 2] + pvs[h]]
        return tuple(out)

    def past(n):
        def body(j, carry):
            def mask(h, s):
                if fox:
                    return s
                keep = jnp.concatenate([jnp.broadcast_to(sel_refs[0][h, pl.ds(j * n + k, 1), :], (t, t))
                                        for k in range(n)], axis=0)
                return jnp.where(keep > 0.5, s, NEG_INF)

            return attend(carry, pl.multiple_of(j * (n * t), n * t), n, mask)

        return body

    carry = attend(None, own, 1, lambda h, s: jnp.where(causal, s, NEG_INF))
    n, done = KV_BLOCKS_PER_TRIP, 0
    while n >= 1:
        trips = (i - done) // n
        carry = lax.fori_loop(done // n, done // n + trips, past(n), carry)
        done = done + trips * n
        n //= 2
    ot = jnp.concatenate([carry[3 * h + 2] / carry[3 * h + 1] for h in heads], axis=0)
    o_ref[...] = ot.T


def _attn_prompt(qt, k_all, layer, vtb, which, x, fox):
    b, _, _, l = qt.shape
    nb = l // MOBA_BLOCK
    t = MOBA_BLOCK
    if fox:
        x_spec = pl.BlockSpec((None, N_HEADS, l), lambda bi, i: (bi, 0, 0))
        scratch = []
    else:
        x_spec = pl.BlockSpec((None, N_HEADS, nb, HEAD_DIM), lambda bi, i: (bi, 0, 0, 0))
        scratch = [pltpu.VMEM((N_HEADS, nb, t), F32)]
    return pl.pallas_call(
        functools.partial(_attn_prompt_kernel, fox=fox),
        out_shape=jax.ShapeDtypeStruct((b * l, MIX), F32),
        grid=(b, nb),
        in_specs=[pl.BlockSpec((None, None, MIX, t), lambda bi, i: (bi, which, 0, i)),
                  pl.BlockSpec((None, None, MIX, l), lambda bi, i: (layer, bi, 0, 0)),
                  pl.BlockSpec((None, None, MIX, l), lambda bi, i: (bi, which, 0, 0)),
                  x_spec],
        out_specs=pl.BlockSpec((t, MIX), lambda bi, i: (bi * nb + i, 0)),
        scratch_shapes=[pltpu.VMEM((N_HEADS, l, 2 * HEAD_DIM), BF16),
                        pltpu.VMEM((N_HEADS, 2 * HEAD_DIM, t), BF16)] + scratch,
        compiler_params=_params("parallel", "arbitrary"),
        name="fox_prompt" if fox else "moba_prompt",
    )(qt, k_all, vtb, x)


def _gmlp_kernel(u_ref, v_ref, w_ref, b_ref, o_ref, *, c, cps):
    grp = lax.broadcasted_iota(jnp.int32, (c, MIX), 1) // HEAD_DIM
    row = lax.broadcasted_iota(jnp.int32, (c, c), 0)
    col = lax.broadcasted_iota(jnp.int32, (c, c), 1)
    w_cat = jnp.concatenate([jnp.where(col <= row, w_ref[g], 0.0) for g in range(N_HEADS)], axis=1).astype(BF16)
    bias = b_ref[...]
    for ci in range(cps):
        rows = slice(ci * c, (ci + 1) * c)
        v = v_ref[rows, :]
        v_cat = jnp.concatenate([jnp.where(grp == g, v, 0.0) for g in range(N_HEADS)], axis=0)
        mixed = jnp.dot(w_cat, v_cat.astype(BF16), preferred_element_type=F32)
        o_ref[rows, :] = u_ref[rows, :] * (mixed + bias)


def _gmlp(proj, u_col, v_col, ws, bs, c):
    n = proj.shape[0]
    bias = jnp.repeat(bs.T, HEAD_DIM, axis=1)
    cps = 8 if n % (8 * c) == 0 else 1
    col = lambda g: pl.BlockSpec((cps * c, MIX), lambda i: (i, g))
    return pl.pallas_call(
        functools.partial(_gmlp_kernel, c=c, cps=cps),
        out_shape=jax.ShapeDtypeStruct((n, MIX), F32),
        grid=(n // (cps * c),),
        in_specs=[col(u_col), col(v_col), pl.BlockSpec((N_HEADS, c, c), lambda i: (0, 0, 0)),
                  pl.BlockSpec((c, MIX), lambda i: (0, 0))],
        out_specs=col(0),
        compiler_params=_params("parallel"),
        name="gmlp",
    )(proj, proj, ws, bias)


def _retention_kernel(q_ref, kt_ref, v_ref, g_ref, s0_ref, dm_ref, qd_ref, kd_ref, cd_ref, gn_ref,
                      o_ref, s_ref, state, *, c, cps):
    step = pl.program_id(1)

    @pl.when(step == 0)
    def _():
        state[...] = s0_ref[...]

    for h in range(N_HEADS):
        s = state[h]
        for ci in range(cps):
            rows = slice(ci * c, (ci + 1) * c)
            qc = q_ref[h, rows, :].astype(BF16)
            vc = v_ref[h, rows, :].astype(BF16)
            kt = kt_ref[h, :, rows]
            att = jnp.dot(qc, kt.astype(BF16), preferred_element_type=F32) * dm_ref[h]
            o = (jnp.dot(att.astype(BF16), vc, preferred_element_type=F32)
                 + jnp.dot(qc, s.astype(BF16), preferred_element_type=F32) * qd_ref[h])
            s = s * cd_ref[h] + jnp.dot((kt * kd_ref[h]).astype(BF16), vc, preferred_element_type=F32)
            mu = jnp.mean(o, axis=-1, keepdims=True)
            oc = o - mu
            var = jnp.mean(oc * oc, axis=-1, keepdims=True)
            o_ref[h, rows, :] = jax.nn.silu(g_ref[h, rows, :]) * (oc * lax.rsqrt(var + EPS) * gn_ref[h])
        state[h] = s

        @pl.when(step == pl.num_programs(1) - 1)
        def _():
            s_ref[h] = s


def _retention(q, kt, v, g, s0, gn_g, c, cps):
    b, h, l, dh = q.shape
    lg = jnp.log(1.0 - 2.0 ** (-5.0 - jnp.arange(h, dtype=F32)))
    j = jnp.arange(c, dtype=F32)
    diff = j[:, None] - j[None, :]
    dmat = jnp.where(diff[None] >= 0, jnp.exp(jnp.maximum(diff, 0.0)[None] * lg[:, None, None]), 0.0)
    q_dec = jnp.broadcast_to(jnp.exp((j[None, :] + 1.0) * lg[:, None])[:, :, None], (h, c, dh))
    k_dec = jnp.broadcast_to(jnp.exp((c - 1.0 - j)[None, :] * lg[:, None])[:, None, :], (h, dh, c))
    chunk_dec = jnp.broadcast_to(jnp.exp(c * lg)[:, None, None], (h, dh, dh))
    r = c * cps
    rows = pl.BlockSpec((None, h, r, dh), lambda bi, i: (bi, 0, i, 0))
    per_b = pl.BlockSpec((None, h, dh, dh), lambda bi, i: (bi, 0, 0, 0))
    table = lambda s1, s2: pl.BlockSpec((h, s1, s2), lambda bi, i: (0, 0, 0))
    return pl.pallas_call(
        functools.partial(_retention_kernel, c=c, cps=cps),
        out_shape=(jax.ShapeDtypeStruct((b, h, l, dh), F32), jax.ShapeDtypeStruct((b, h, dh, dh), F32)),
        grid=(b, l // r),
        in_specs=[rows, pl.BlockSpec((None, h, dh, r), lambda bi, i: (bi, 0, 0, i)), rows, rows,
                  per_b, table(c, c), table(c, dh), table(dh, c), table(dh, dh), table(1, dh)],
        out_specs=(rows, per_b),
        scratch_shapes=[pltpu.VMEM((h, dh, dh), F32)],
        compiler_params=_params("parallel", "arbitrary"),
        name="retention",
    )(q, kt, v, g, s0, dmat, q_dec, k_dec, chunk_dec, gn_g.reshape(h, 1, dh))


def _retention_rows_kernel(q_ref, k_ref, v_ref, g_ref, s0_ref, dm_ref, qd_ref, kd_ref, cd_ref, bd_ref, gn_ref,
                           o_ref, s_ref, state, *, c, cps):
    step = pl.program_id(1)

    @pl.when(step == 0)
    def _():
        state[...] = s0_ref[...]

    heads = range(N_HEADS)
    grp = lax.broadcasted_iota(jnp.int32, (c, MIX), 1) // HEAD_DIM
    bd = bd_ref[...]
    avg = (bd * (1.0 / HEAD_DIM)).astype(BF16)
    s = state[...]
    for ci in range(cps):
        rows = slice(ci * c, (ci + 1) * c)
        q, k, v = q_ref[rows, :], k_ref[rows, :], v_ref[rows, :]
        kb, vb = k.astype(BF16), v.astype(BF16)
        att = jnp.concatenate([_nt_dot(jnp.where(grp == h, q, 0.0).astype(BF16), kb) * dm_ref[h] for h in heads],
                              axis=1)
        v_cat = jnp.concatenate([jnp.where(grp == h, v, 0.0).astype(BF16) for h in heads], axis=0)
        o = (jnp.dot(att.astype(BF16), v_cat, preferred_element_type=F32)
             + jnp.dot(q.astype(BF16), s.astype(BF16), preferred_element_type=F32) * qd_ref[...])
        s = s * cd_ref[...] + bd * jnp.dot((k * kd_ref[...]).T.astype(BF16), vb, preferred_element_type=F32)
        mu = jnp.dot(o.astype(BF16), avg, preferred_element_type=F32)
        oc = o - mu
        var = jnp.dot((oc * oc).astype(BF16), avg, preferred_element_type=F32)
        o_ref[rows, :] = jax.nn.silu(g_ref[rows, :]) * (oc * lax.rsqrt(var + EPS) * gn_ref[...])
    state[...] = s

    @pl.when(step == pl.num_programs(1) - 1)
    def _():
        s_ref[...] = s


def _retention_rows(proj, cols, b, l, gn_g, c, cps):
    h, dh = N_HEADS, HEAD_DIM
    lg = jnp.log(1.0 - 2.0 ** (-5.0 - jnp.arange(h, dtype=F32)))
    j = jnp.arange(c, dtype=F32)
    diff = j[:, None] - j[None, :]
    dmat = jnp.where(diff[None] >= 0, jnp.exp(jnp.maximum(diff, 0.0)[None] * lg[:, None, None]), 0.0)
    lanes = lambda x: jnp.repeat(x, dh, axis=-1)
    q_dec = lanes(jnp.exp((j[:, None] + 1.0) * lg[None, :]))
    k_dec = lanes(jnp.exp((c - 1.0 - j)[:, None] * lg[None, :]))
    chunk_dec = jnp.broadcast_to(lanes(jnp.exp(c * lg)[None, :]), (MIX, MIX))
    head_of = jnp.arange(MIX) // dh
    bd = (head_of[:, None] == head_of[None, :]).astype(F32)
    r = c * cps
    per = l // r
    col = lambda g: pl.BlockSpec((r, MIX), lambda bi, i: (bi * per + i, g))
    const = lambda *shape: pl.BlockSpec(shape, lambda bi, i: (0,) * len(shape))
    state = pl.BlockSpec((None, MIX, MIX), lambda bi, i: (bi, 0, 0))
    od, s_bd = pl.pallas_call(
        functools.partial(_retention_rows_kernel, c=c, cps=cps),
        out_shape=(jax.ShapeDtypeStruct((b * l, MIX), F32), jax.ShapeDtypeStruct((b, MIX, MIX), F32)),
        grid=(b, per),
        in_specs=[col(cols[0]), col(cols[1]), col(cols[2]), col(cols[3]), state, const(h, c, c),
                  const(c, MIX), const(c, MIX), const(MIX, MIX), const(MIX, MIX), const(1, MIX)],
        out_specs=(col(0), state),
        scratch_shapes=[pltpu.VMEM((MIX, MIX), F32)],
        compiler_params=_params("parallel", "arbitrary"),
        name="retention_rows",
    )(proj, proj, proj, proj, jnp.zeros((b, MIX, MIX), F32), dmat, q_dec, k_dec, chunk_dec, bd, gn_g[None])
    s5 = s_bd.reshape(b, h, dh, h, dh)
    return od, jnp.stack([s5[:, i, :, i, :] for i in range(h)], axis=1)


def _merge_kernel(h_ref, g_ref, oa_ref, ob_ref, oc_ref, od_ref, wm_ref, wb_ref, wo_ref, o_ref):
    h = h_ref[...]
    d = h.shape[1]
    xn = _rms(h, g_ref[...]).astype(BF16)
    z = None
    for n, br_ref in enumerate((oa_ref, ob_ref, oc_ref, od_ref)):
        gate = jax.nn.sigmoid(jnp.dot(xn, wm_ref[:, n * d:(n + 1) * d], preferred_element_type=F32))
        br = jnp.dot(br_ref[...].astype(BF16), wb_ref[n], preferred_element_type=F32)
        z = gate * br if z is None else z + gate * br
    o_ref[...] = h + jnp.dot(z.astype(BF16), wo_ref[...], preferred_element_type=F32)


def _merge(h, g, branches, w_merge, w_branch, w_out, tm):
    n, d = h.shape
    rows = lambda w: pl.BlockSpec((tm, w), lambda i: (i, 0))
    return pl.pallas_call(
        _merge_kernel,
        out_shape=jax.ShapeDtypeStruct((n, d), F32),
        grid=(n // tm,),
        in_specs=[rows(d), pl.BlockSpec((1, d), lambda i: (0, 0)), rows(MIX), rows(MIX), rows(MIX), rows(MIX),
                  pl.BlockSpec(w_merge.shape, lambda i: (0, 0)),
                  pl.BlockSpec(w_branch.shape, lambda i: (0, 0, 0)),
                  pl.BlockSpec(w_out.shape, lambda i: (0, 0))],
        out_specs=rows(d),
        compiler_params=_params("parallel"),
        name="merge",
    )(h, g, *branches, w_merge, w_branch, w_out)


def _ffn_kernel(h_ref, g_ref, wg_ref, wu_ref, wo_ref, o_ref, xn_ref, acc_ref):
    k = pl.program_id(1)

    @pl.when(k == 0)
    def _():
        xn_ref[...] = _rms(h_ref[...], g_ref[...]).astype(BF16)
        acc_ref[...] = h_ref[...]

    xn = xn_ref[...]
    tf = wg_ref.shape[1]
    slab = FFN_SLAB if tf % FFN_SLAB == 0 else tf
    part = None
    for lo in range(0, tf, slab):
        gate = jnp.dot(xn, wg_ref[:, lo:lo + slab], preferred_element_type=F32)
        up = jnp.dot(xn, wu_ref[:, lo:lo + slab], preferred_element_type=F32)
        act = (jax.nn.silu(gate) * up).astype(BF16)
        out = jnp.dot(act, wo_ref[lo:lo + slab, :], preferred_element_type=F32)
        part = out if part is None else part + out
    acc_ref[...] += part

    @pl.when(k == pl.num_programs(1) - 1)
    def _():
        o_ref[...] = acc_ref[...]


def _ffn(h, g, w_in, w_out, tm, tf):
    n, d = h.shape
    d_ff = w_out.shape[0]
    nk = d_ff // tf
    return pl.pallas_call(
        _ffn_kernel,
        out_shape=jax.ShapeDtypeStruct((n, d), F32),
        grid=(n // tm, nk),
        in_specs=[pl.BlockSpec((tm, d), lambda i, k: (i, 0)), pl.BlockSpec((1, d), lambda i, k: (0, 0)),
                  pl.BlockSpec((d, tf), lambda i, k: (0, k)), pl.BlockSpec((d, tf), lambda i, k: (0, nk + k)),
                  pl.BlockSpec((tf, d), lambda i, k: (k, 0))],
        out_specs=pl.BlockSpec((tm, d), lambda i, k: (i, 0)),
        scratch_shapes=[pltpu.VMEM((tm, d), BF16), pltpu.VMEM((tm, d), F32)],
        compiler_params=_params("parallel", "arbitrary"),
        name="ffn",
    )(h, g, w_in, w_in, w_out)


def _ple_kernel(h_ref, g_ref, p_ref, wg_ref, wp_ref, gf_ref, o_ref, *, final):
    h = h_ref[...]
    xn = _rms(h, g_ref[...]).astype(BF16)
    gate = jax.nn.sigmoid(jnp.dot(xn, wg_ref[...], preferred_element_type=F32))
    emb = jnp.dot(p_ref[...].astype(BF16), wp_ref[...], preferred_element_type=F32)
    out = h + gate * emb
    o_ref[...] = _rms(out, gf_ref[...]) if final else out


def _ple(h, g, p, w_gate, w_proj, g_final, tm, final):
    n, d = h.shape
    p, layer = p
    dp = p.shape[2]
    vec = pl.BlockSpec((1, d), lambda i: (0, 0))
    return pl.pallas_call(
        functools.partial(_ple_kernel, final=final),
        out_shape=jax.ShapeDtypeStruct((n, d), F32),
        grid=(n // tm,),
        in_specs=[pl.BlockSpec((tm, d), lambda i: (i, 0)), vec, pl.BlockSpec((None, tm, dp), lambda i: (layer, i, 0)),
                  pl.BlockSpec((d, d), lambda i: (0, 0)), pl.BlockSpec((dp, d), lambda i: (0, 0)), vec],
        out_specs=pl.BlockSpec((tm, d), lambda i: (i, 0)),
        compiler_params=_params("parallel"),
        name="ple",
    )(h, g, p, w_gate, w_proj, g_final)


def _stream_pages(pt_ref, streams, sems, layer, pp, n_chunks, n_seq, reverse, step):
    total = n_seq * n_chunks
    assert total % 2 == 0

    def locate(g):
        b = g // n_chunks
        c = g - b * n_chunks
        return b, (n_chunks - 1 - c if reverse else c)

    def copies(page_of, slot):
        return [pltpu.make_async_copy(hbm.at[layer, page_of(k)], buf.at[slot, k], sems.at[n, slot])
                for k in range(pp) for n, (hbm, buf) in enumerate(streams)]

    def fetch(g, slot):
        b, c = locate(g)
        for cp in copies(lambda k: pt_ref[b, c * pp + k], slot):
            cp.start()

    def pair(i, _):
        for slot in (0, 1):
            g = 2 * i + slot

            @pl.when(g + 1 < total)
            def _():
                fetch(g + 1, 1 - slot)

            for cp in copies(lambda k: 0, slot):
                cp.wait()
            step(g, *locate(g), slot)
        return 0

    fetch(0, 0)
    lax.fori_loop(0, total // 2, pair, 0)


def _paged_call(body, page_table, caches, dense, out_shape, scratch, name, page_shapes):
    whole = lambda x: pl.BlockSpec(x.shape, lambda i, pt, n=len(x.shape): (0,) * n)
    grid_spec = pltpu.PrefetchScalarGridSpec(
        num_scalar_prefetch=1, grid=(1,),
        in_specs=[pl.BlockSpec(memory_space=pl.ANY)] * len(caches) + [whole(x) for x in dense],
        out_specs=whole(out_shape),
        scratch_shapes=[pltpu.VMEM((2, PAGES_PER_STEP) + shape, F32) for shape in page_shapes]
        + [pltpu.SemaphoreType.DMA((len(caches), 2))] + scratch)
    return pl.pallas_call(body, out_shape=out_shape, grid_spec=grid_spec,
                          compiler_params=_params("arbitrary"), name=name)(page_table, *caches, *dense)


def _kmean_sample_kernel(pt_ref, k_hbm, o_ref, kbuf, sems, *, layer, pp, n_chunks):
    per_block = MOBA_BLOCK // PAGE_SIZE
    nblk = pp // per_block
    lane = lax.broadcasted_iota(jnp.int32, (MIX, LANES), 1)

    def step(g, b, c, slot):
        @pl.when(c == 0)
        def _():
            o_ref[b] = jnp.zeros((MIX, LANES), F32)

        acc = o_ref[b]
        for blk in range(nblk):
            tot = kbuf[slot, blk * per_block]
            for k in range(1, per_block):
                tot = tot + kbuf[slot, blk * per_block + k]
            mean = jnp.sum(tot, axis=1, keepdims=True) * (1.0 / MOBA_BLOCK)
            acc = jnp.where(lane == c * nblk + blk, mean, acc)
        o_ref[b] = acc

    _stream_pages(pt_ref, [(k_hbm, kbuf)], sems, layer, pp, n_chunks, o_ref.shape[0], False, step)


def _kmean_sample(cache_k, layer, page_table):
    db, n_pages = page_table.shape
    pp = PAGES_PER_STEP
    assert n_pages * PAGE_SIZE // MOBA_BLOCK <= LANES
    return _paged_call(
        functools.partial(_kmean_sample_kernel, layer=layer, pp=pp, n_chunks=n_pages // pp),
        page_table, [cache_k], [], jax.ShapeDtypeStruct((db, MIX, LANES), F32), [], "kmean_sample",
        [(MIX, PAGE_SIZE)])


def _lane_suffix_sum(x):
    n = x.shape[-1]
    lane = lax.broadcasted_iota(jnp.int32, x.shape, x.ndim - 1)
    s = 1
    while s < n:
        x = x + jnp.where(lane < n - s, pltpu.roll(x, n - s, x.ndim - 1), 0.0)
        s *= 2
    return x


def _expand_heads(x, t):
    return jnp.concatenate([jnp.broadcast_to(x[h:h + 1, :], (t, x.shape[1])) for h in range(N_HEADS)], axis=0)


def _decode_attn_kernel(pt_ref, *refs, layer, pp, n_chunks, t, fox, nb):
    if fox:
        (k_hbm, v_hbm, f_hbm, q_ref, kn_ref, vn_ref, fn_ref, o_ref, kbuf, vbuf, fbuf, sems,
         s_ref, m_ref, l_ref, acc_ref, cq_ref, tail_ref) = refs
        streams = [(k_hbm, kbuf), (v_hbm, vbuf), (f_hbm, fbuf)]
    else:
        (k_hbm, v_hbm, q_ref, kn_ref, vn_ref, km_ref, o_ref, kbuf, vbuf, sems,
         s_ref, m_ref, l_ref, acc_ref, sel_ref) = refs
        streams = [(k_hbm, kbuf), (v_hbm, vbuf)]
    r = N_HEADS * t

    def step(g, b, c, slot):
        first = c == (n_chunks - 1 if fox else 0)
        last = c == (0 if fox else n_chunks - 1)
        q = q_ref[b]
        qb = (q * SCALE).astype(BF16)

        @pl.when(first)
        def _():
            s = jnp.dot(qb, kn_ref[b].astype(BF16), preferred_element_type=F32)
            key = lax.broadcasted_iota(jnp.int32, (r, LANES), 1)
            qi = lax.broadcasted_iota(jnp.int32, (r, LANES), 0) % t
            if fox:
                cn = _expand_heads(_lane_cumsum(fn_ref[b]), t)
                cq = jnp.sum(jnp.where(key == qi, cn, 0.0), axis=1, keepdims=True)
                cq_ref[...] = jnp.broadcast_to(cq, cq_ref.shape)
                tail_ref[...] = jnp.zeros(tail_ref.shape, F32)
                s = s + cq - cn
            s = jnp.where(key <= qi, s, NEG_INF)
            m = jnp.max(s, axis=1, keepdims=True)
            p = jnp.exp(s - m)
            m_ref[...] = jnp.broadcast_to(m, m_ref.shape)
            l_ref[...] = jnp.broadcast_to(jnp.sum(p, axis=1, keepdims=True), l_ref.shape)
            acc_ref[...] = _nt_dot(p.astype(BF16), vn_ref[b].astype(BF16))
            if not fox:
                gate = jnp.dot(q, km_ref[b], precision=lax.Precision.HIGHEST, preferred_element_type=F32)
                blk = lax.broadcasted_iota(jnp.int32, gate.shape, 1)
                sel_ref[...] = _topk_select(jnp.where(blk < nb, gate, NEG_INF), min(MOBA_TOPK, nb), 1)

        for k in range(pp):
            s_ref[:, k * PAGE_SIZE:(k + 1) * PAGE_SIZE] = jnp.dot(qb, kbuf[slot, k].astype(BF16),
                                                                  preferred_element_type=F32)
        s = s_ref[...]
        if fox:
            f = jnp.concatenate([fbuf[slot, k] for k in range(pp)], axis=1)
            suffix = _lane_suffix_sum(f)
            tail = tail_ref[:, :1]
            s = s + jnp.concatenate([cq_ref[...]] * pp, axis=1) + _expand_heads(suffix - f + tail, t)
            tail_ref[...] = jnp.broadcast_to(tail + suffix[:, :1], tail_ref.shape)
        else:
            per_block = MOBA_BLOCK // PAGE_SIZE
            nblk = pp // per_block
            sel = sel_ref[...]
            blk = lax.broadcasted_iota(jnp.int32, sel.shape, 1)
            cols = []
            for j in range(nblk):
                chosen = jnp.sum(jnp.where(blk == c * nblk + j, sel, 0.0), axis=1, keepdims=True)
                cols.append(jnp.broadcast_to(chosen, (r, MOBA_BLOCK)))
            s = jnp.where(jnp.concatenate(cols, axis=1) > 0.5, s, NEG_INF)
        m = m_ref[:, :1]
        m_new = jnp.maximum(m, jnp.max(s, axis=1, keepdims=True))
        alpha = jnp.exp(m - m_new)
        p = jnp.exp(s - m_new).astype(BF16)
        l_new = alpha * l_ref[:, :1] + jnp.sum(p.astype(F32), axis=1, keepdims=True)
        acc = alpha * acc_ref[...]
        for k in range(pp):
            acc = acc + _nt_dot(p[:, k * PAGE_SIZE:(k + 1) * PAGE_SIZE], vbuf[slot, k].astype(BF16))
        m_ref[...] = jnp.broadcast_to(m_new, m_ref.shape)
        l_ref[...] = jnp.broadcast_to(l_new, l_ref.shape)
        acc_ref[...] = acc

        @pl.when(last)
        def _():
            o = acc / l_new
            grp = lax.broadcasted_iota(jnp.int32, (t, MIX), 1) // HEAD_DIM
            out = jnp.zeros((t, MIX), F32)
            for h in range(N_HEADS):
                out = out + jnp.where(grp == h, o[h * t:(h + 1) * t, :], 0.0)
            o_ref[b] = out

    _stream_pages(pt_ref, streams, sems, layer, pp, n_chunks, q_ref.shape[0], fox, step)


def _decode_attn(cache_k, cache_v, layer, page_table, q_blk, k_new, v_new, extra, fox):
    db, n_pages = page_table.shape
    pp = PAGES_PER_STEP
    r = q_blk.shape[1]
    t = r // N_HEADS
    scratch = [pltpu.VMEM((r, pp * PAGE_SIZE), F32), pltpu.VMEM((r, LANES), F32), pltpu.VMEM((r, LANES), F32),
               pltpu.VMEM((r, MIX), F32), pltpu.VMEM((r, LANES), F32)]
    caches, pages = [cache_k, cache_v], [(MIX, PAGE_SIZE), (MIX, PAGE_SIZE)]
    if fox:
        caches, pages = caches + [extra[0]], pages + [(N_HEADS, PAGE_SIZE)]
        scratch += [pltpu.VMEM((N_HEADS, LANES), F32)]
    return _paged_call(
        functools.partial(_decode_attn_kernel, layer=layer, pp=pp, n_chunks=n_pages // pp, t=t, fox=fox,
                          nb=n_pages * PAGE_SIZE // MOBA_BLOCK),
        page_table, caches, [q_blk, k_new, v_new, extra[-1]], jax.ShapeDtypeStruct((db, t, MIX), F32), scratch,
        "fox_sample" if fox else "moba_sample", pages)


def _to_heads(x, b, l):
    return x.reshape(b, l, N_HEADS, HEAD_DIM).transpose(0, 2, 1, 3)


def _from_heads(x):
    b, h, l, dh = x.shape
    return x.transpose(0, 2, 1, 3).reshape(b * l, h * dh)


def _split_proj(proj):
    cols = [proj[:, g * MIX:(g + 1) * MIX] for g in range(N_PROJ_GROUPS)]
    logf = proj[:, N_PROJ_GROUPS * MIX:N_PROJ_GROUPS * MIX + N_HEADS]
    return cols, logf


def _tail(h, p, prm, branches, final, tm):
    h = _merge(h, prm["norm_mix"], branches, prm["w_merge"], prm["w_branch"], prm["w_out"], tm)
    d_ff = prm["w_ffn_out"].shape[0]
    tf = d_ff
    h = _ffn(h, prm["norm_ffn"], prm["w_ffn_in"], prm["w_ffn_out"], min(512, h.shape[0]), tf)
    return _ple(h, prm["norm_ple"], p, prm["w_ple_gate"], prm["w_ple_proj"], prm["norm_final"], tm, final)


def _prompt_layer(h, p, prm, tabs, kv_all, layer, b, l, final):
    tm = 256
    proj, qt, *kv_all, vtb = _proj(h, prm["norm_mix"], prm["w_in"], prm["b_forget"], prm["ln_g"], prm["ln_b"],
                                   tabs, tm, kv_all=kv_all, layer=layer)
    logf = proj[:, N_PROJ_GROUPS * MIX:N_PROJ_GROUPS * MIX + N_HEADS]
    kmean = _kmean_prompt(proj, 1, b, l).reshape(b, -1, N_HEADS, HEAD_DIM).transpose(0, 2, 1, 3)
    oa = _attn_prompt(qt, kv_all[0], layer, vtb, 0, kmean, fox=False)
    c = _cumsum_prompt(logf.reshape(b, l, N_HEADS).transpose(0, 2, 1).reshape(b * N_HEADS, l))
    ob = _attn_prompt(qt, kv_all[2], layer, vtb, 1, c.reshape(b, N_HEADS, l), fox=True)
    oc = _gmlp(proj, 6, 7, prm["gmlp_ws"], prm["gmlp_bs"], GMLP_CHUNK)
    cps = 8 if l % (8 * RET_CHUNK) == 0 else 1
    od, s_new = _retention_rows(proj, (8, 9, 10, 11), b, l, prm["ret_gn_g"], RET_CHUNK, cps)
    h = _tail(h, p, prm, (oa, ob, oc, od), final, tm)
    return h, kv_all, (logf.reshape(b, l, N_HEADS), s_new)


def _block_diag_q(q, db, t):
    grp = jnp.arange(MIX) // HEAD_DIM
    mask = (grp[None, None, :] == jnp.arange(N_HEADS)[:, None, None]).astype(F32)
    return (q.reshape(db, 1, t, MIX) * mask[None]).reshape(db, N_HEADS * t, MIX)


def _new_keys_t(x, db, t):
    return jnp.pad(x.reshape(db, t, MIX).transpose(0, 2, 1), ((0, 0), (0, 0), (0, LANES - t)))


def _sample_layer(h, p, prm, tabs, caches, layer, page_table, db, t, final):
    cache_k_a, cache_v_a, cache_k_b, cache_v_b, cache_logf_t, state = caches
    n = db * t
    (proj,) = _proj(h, prm["norm_mix"], prm["w_in"], prm["b_forget"], prm["ln_g"], prm["ln_b"], tabs, n)
    (qa, ka, va, qb, kb, vb, _, vn, qd, kd, vd, gd), logf = _split_proj(proj)
    kmean = _kmean_sample(cache_k_a, layer, page_table)
    oa = _decode_attn(cache_k_a, cache_v_a, layer, page_table, _block_diag_q(qa, db, t),
                      _new_keys_t(ka, db, t), _new_keys_t(va, db, t), (kmean,), fox=False)
    logf_new = jnp.pad(logf.reshape(db, t, N_HEADS).transpose(0, 2, 1), ((0, 0), (0, 0), (0, LANES - t)))
    ob = _decode_attn(cache_k_b, cache_v_b, layer, page_table, _block_diag_q(qb, db, t),
                      _new_keys_t(kb, db, t), _new_keys_t(vb, db, t), (cache_logf_t, logf_new), fox=True)
    oc = _gmlp(proj, 6, 7, prm["gmlp_ws"][:, :t, :t], prm["gmlp_bs"][:, :t], t)
    hd = lambda x: _to_heads(x, db, t)
    od, s_new = _retention(hd(qd), hd(kd).transpose(0, 1, 3, 2), hd(vd), hd(gd), state[layer],
                           prm["ret_gn_g"], t, 1)
    h = _tail(h, p, prm, (oa.reshape(n, MIX), ob.reshape(n, MIX), oc, _from_heads(od)), final, n)
    rows = (ka.reshape(db, t, N_HEADS, HEAD_DIM), va.reshape(db, t, N_HEADS, HEAD_DIM),
            kb.reshape(db, t, N_HEADS, HEAD_DIM), vb.reshape(db, t, N_HEADS, HEAD_DIM),
            logf.reshape(db, t, N_HEADS), s_new, vn.reshape(db, t, MIX))
    return h, rows


def kernel(x_prompt, x_sample, cache_k_a, cache_v_a, cache_k_b, cache_v_b, cache_logf_b, state_ret, page_table, p_prompt, p_sample, norm_mix, w_in, b_forget, gmlp_ln_g, gmlp_ln_b, gmlp_ws, gmlp_bs, ret_gn_g, w_branch, w_merge, w_out, norm_ffn, w_ffn_in, w_ffn_out, norm_ple, w_ple_gate, w_ple_proj, norm_final):
    b, l, d = x_prompt.shape
    db, t, _ = x_sample.shape
    depth = w_in.shape[0]
    n_pages = page_table.shape[1]
    past_len = n_pages * PAGE_SIZE
    assert d == N_HEADS * MIX and l % MOBA_BLOCK == 0
    assert past_len % MOBA_BLOCK == 0 and t <= LANES and t % 8 == 0 and t < MOBA_BLOCK
    assert n_pages % PAGES_PER_STEP == 0

    pos_p = jnp.arange(l, dtype=jnp.int32)
    pos_s = jnp.tile(past_len + jnp.arange(t, dtype=jnp.int32), db)
    tabs_p = _rope_tables(pos_p, ROT_DIMS, ROPE_THETA) + _rope_tables(pos_p, HEAD_DIM, RET_THETA)
    tabs_s = _rope_tables(pos_s, ROT_DIMS, ROPE_THETA) + _rope_tables(pos_s, HEAD_DIM, RET_THETA)

    pool = cache_k_a.shape[1]
    pages = lambda x: x.transpose(0, 1, 3, 4, 2).reshape(depth, pool, MIX, PAGE_SIZE)
    caches = (pages(cache_k_a), pages(cache_v_a), pages(cache_k_b), pages(cache_v_b),
              cache_logf_b.transpose(0, 1, 3, 2), state_ret)

    hp = x_prompt.reshape(b * l, d)
    hs = x_sample.reshape(db * t, d)
    rows_p, rows_s = [], []
    kv_all = [jnp.zeros((depth, b, MIX, l), F32) for _ in range(N_KV_OUT)]
    f_lo, f_hi = 6 * MIX, 6 * MIX + N_HEADS
    w_in_pad = jnp.concatenate(
        [w_in[:, :, :f_lo].astype(BF16), w_in[:, :, f_hi:].astype(BF16),
         jnp.pad(w_in[:, :, f_lo:f_hi].astype(BF16), ((0, 0), (0, 0), (0, LANES - N_HEADS)))], axis=2)
    for i in range(depth):
        prm = dict(
            norm_mix=norm_mix[i][None], w_in=w_in_pad[i],
            b_forget=jnp.pad(b_forget[i], (0, LANES - N_HEADS))[None],
            ln_g=gmlp_ln_g[i][None], ln_b=gmlp_ln_b[i][None], gmlp_ws=gmlp_ws[i], gmlp_bs=gmlp_bs[i],
            ret_gn_g=ret_gn_g[i], w_branch=w_branch[i].astype(BF16), w_merge=w_merge[i].astype(BF16),
            w_out=w_out[i].astype(BF16), norm_ffn=norm_ffn[i][None], w_ffn_in=w_ffn_in[i].astype(BF16),
            w_ffn_out=w_ffn_out[i].astype(BF16), norm_ple=norm_ple[i][None],
            w_ple_gate=w_ple_gate[i].astype(BF16), w_ple_proj=w_ple_proj[i].astype(BF16),
            norm_final=norm_final[None])
        final = i == depth - 1
        hp, kv_all, rp = _prompt_layer(hp, (p_prompt.reshape(depth, b * l, -1), i), prm, tabs_p, kv_all, i,
                                       b, l, final)
        hs, rs = _sample_layer(hs, (p_sample.reshape(depth, db * t, -1), i), prm, tabs_s, caches, i, page_table,
                               db, t, final)
        rows_p.append(rp)
        rows_s.append(rs)
    stack = lambda rows, j: jnp.stack([r[j] for r in rows], axis=0)
    kv_out = tuple(x.reshape(depth, b, N_HEADS, HEAD_DIM, l).transpose(0, 1, 4, 2, 3) for x in kv_all)
    return ((hp.reshape(b, l, d), hs.reshape(db, t, d)) + kv_out
            + tuple(stack(rows_p, j) for j in range(2)) + tuple(stack(rows_s, j) for j in range(7)))
```

```python
import functools

import jax
import jax.numpy as jnp
from jax import lax
from jax.experimental import pallas as pl
from jax.experimental.pallas import tpu as pltpu

F32 = jnp.float32
BF16 = jnp.bfloat16

N_HEADS = 4
HEAD_DIM = 64
MIX = N_HEADS * HEAD_DIM
ROT_DIMS = HEAD_DIM // 4
ROPE_THETA = 500000.0
RET_THETA = 10000.0
MOBA_BLOCK = 256
MOBA_TOPK = 3
GMLP_CHUNK = 128
RET_CHUNK = 128
PAGE_SIZE = 128
EPS = 1e-6
SCALE = HEAD_DIM ** -0.5
NEG_INF = float("-inf")

LANES = 128
VMEM_LIMIT = 56 * 1024 * 1024
PAGES_PER_STEP = 32
N_PROJ_GROUPS = 12
N_ATTN_GROUPS = 6
N_KV_OUT = 4
W_IN_PAD = N_PROJ_GROUPS * MIX + LANES
LOG2E = 1.4426950408889634
AUG = 8
FFN_SLAB = 256
KV_BLOCKS_PER_TRIP = 4


def _params(*sem):
    return pltpu.CompilerParams(dimension_semantics=sem, vmem_limit_bytes=VMEM_LIMIT)


def _rms(x, g):
    return x * lax.rsqrt(jnp.mean(x * x, axis=-1, keepdims=True) + EPS) * g


def _nt_dot(a, b, precision=None):
    return lax.dot_general(a, b, (((1,), (1,)), ((), ())), precision=precision,
                           preferred_element_type=F32)


def _log_sigmoid(x):
    return jnp.minimum(x, 0.0) - jnp.log(1.0 + jnp.exp(-jnp.abs(x)))


def _proj_kernel(h_ref, g_ref, w_ref, bf_ref, lng_ref, lnb_ref, ca_ref, sa_ref, cd_ref, sd_ref, *refs):
    o_ref = refs[0] if len(refs) == 1 else refs[N_KV_OUT]
    t_refs = refs[N_KV_OUT + 1:]
    xn = _rms(h_ref[...], g_ref[...]).astype(BF16)
    lane = lax.broadcasted_iota(jnp.int32, (1, MIX), 1) & (HEAD_DIM - 1)

    def rope(y, c_ref, s_ref, half):
        c = jnp.concatenate([c_ref[...], c_ref[...]], axis=1)
        s = jnp.concatenate([s_ref[...], s_ref[...]], axis=1)
        rot = jnp.where(lane < half, pltpu.roll(y, MIX - half, 1), pltpu.roll(y, half, 1))
        return y * c + rot * s

    for grp in range(N_PROJ_GROUPS):
        y = jnp.dot(xn, w_ref[:, grp * MIX:(grp + 1) * MIX], preferred_element_type=F32)
        if grp in (0, 1):
            y = rope(y, ca_ref, sa_ref, ROT_DIMS // 2)
        elif grp == 6:
            y = jax.nn.gelu(y)
        elif grp == 7:
            y = jax.nn.gelu(y)
            mu = jnp.mean(y, axis=-1, keepdims=True)
            yc = y - mu
            var = jnp.mean(yc * yc, axis=-1, keepdims=True)
            y = yc * lax.rsqrt(var + EPS) * lng_ref[...] + lnb_ref[...]
        elif grp == 8:
            y = rope(y, cd_ref, sd_ref, HEAD_DIM // 2)
        elif grp == 9:
            y = rope(y, cd_ref, sd_ref, HEAD_DIM // 2) * SCALE
        o_ref[:, grp * MIX:(grp + 1) * MIX] = y
        if t_refs and grp < N_ATTN_GROUPS:
            qt_ref, kv_refs, vtb_ref = t_refs[0], t_refs[1:1 + N_KV_OUT], t_refs[1 + N_KV_OUT]
            yt = y.T
            which, kind = divmod(grp, 3)
            if kind == 0:
                qt_ref[which] = yt
            else:
                kv_refs[2 * which + kind - 1][...] = yt
            if kind == 2:
                vtb_ref[which] = yt.astype(BF16)
    f = jnp.dot(xn, w_ref[:, N_PROJ_GROUPS * MIX:], preferred_element_type=F32) + bf_ref[...]
    o_ref[:, N_PROJ_GROUPS * MIX:] = _log_sigmoid(f)


def _proj(h, g, w_pad, bf_pad, ln_g, ln_b, tabs, tm, kv_all=None, layer=None):
    n, d = h.shape
    ca, sa, cd, sd = tabs
    tab_blocks = ca.shape[0] // tm
    full = lambda shape: pl.BlockSpec(shape, lambda i: (0, 0))
    tab = pl.BlockSpec((tm, LANES), lambda i: (i % tab_blocks, 0))
    in_specs = [pl.BlockSpec((tm, d), lambda i: (i, 0)), full((1, d)), full((d, W_IN_PAD)),
                full((1, LANES)), full((1, MIX)), full((1, MIX)), tab, tab, tab, tab]
    out_shape = [jax.ShapeDtypeStruct((n, W_IN_PAD), F32)]
    out_specs = [pl.BlockSpec((tm, W_IN_PAD), lambda i: (i, 0))]
    aliases, extra = {}, ()
    if kv_all is not None:
        _, batch, _, l = kv_all[0].shape
        per_seq = l // tm
        pair = pl.BlockSpec((None, 2, MIX, tm), lambda i: (i // per_seq, 0, 0, i % per_seq))
        aliases = {len(in_specs) + k: 1 + k + 1 for k in range(N_KV_OUT)}
        in_specs += [pl.BlockSpec(memory_space=pl.ANY)] * N_KV_OUT
        out_shape += ([jax.ShapeDtypeStruct((batch, 2, MIX, l), F32)]
                      + [jax.ShapeDtypeStruct(x.shape, F32) for x in kv_all]
                      + [jax.ShapeDtypeStruct((batch, 2, MIX, l), BF16)])
        out_specs += ([pair] + [pl.BlockSpec((None, None, MIX, tm),
                                             lambda i: (layer, i // per_seq, 0, i % per_seq))] * N_KV_OUT + [pair])
        extra = tuple(kv_all)
    return pl.pallas_call(
        _proj_kernel,
        out_shape=out_shape,
        grid=(n // tm,),
        in_specs=in_specs,
        out_specs=out_specs,
        input_output_aliases=aliases,
        compiler_params=_params("parallel"),
        name="proj",
    )(h, g, w_pad, bf_pad, ln_g, ln_b, ca, sa, cd, sd, *extra)


def _rope_tables(pos, n_rot, theta):
    half = n_rot // 2
    inv = theta ** (-jnp.arange(half, dtype=F32) / half)
    ang = pos.astype(F32)[:, None] * inv[None, :]
    cos, sin = jnp.cos(ang), jnp.sin(ang)
    rest = HEAD_DIM - n_rot
    ones = jnp.ones((pos.shape[0], rest), F32)
    zeros = jnp.zeros((pos.shape[0], rest), F32)
    c = jnp.concatenate([cos, cos, ones], axis=1)
    s = jnp.concatenate([-sin, sin, zeros], axis=1)
    reps = LANES // HEAD_DIM
    return jnp.tile(c, (1, reps)), jnp.tile(s, (1, reps))


def _kmean_kernel(k_ref, o_ref, *, nblk):
    x = k_ref[...].reshape(nblk, MOBA_BLOCK, MIX)
    o_ref[...] = jnp.sum(x, axis=1) * (1.0 / MOBA_BLOCK)


def _kmean_prompt(proj, col, b, l):
    nblk = 8
    assert l % (nblk * MOBA_BLOCK) == 0
    per_seq = l // (nblk * MOBA_BLOCK)
    return pl.pallas_call(
        functools.partial(_kmean_kernel, nblk=nblk),
        out_shape=jax.ShapeDtypeStruct((b, l // MOBA_BLOCK, MIX), F32),
        grid=(b, per_seq),
        in_specs=[pl.BlockSpec((nblk * MOBA_BLOCK, MIX), lambda i, j: (i * per_seq + j, col))],
        out_specs=pl.BlockSpec((None, nblk, MIX), lambda i, j: (i, j, 0)),
        compiler_params=_params("parallel", "parallel"),
        name="kmean_prompt",
    )(proj)


def _lane_cumsum(x):
    n = x.shape[-1]
    lane = lax.broadcasted_iota(jnp.int32, x.shape, x.ndim - 1)
    s = 1
    while s < n:
        x = x + jnp.where(lane >= s, pltpu.roll(x, s, x.ndim - 1), 0.0)
        s *= 2
    return x


def _cumsum_kernel(x_ref, o_ref):
    o_ref[...] = _lane_cumsum(x_ref[...])


def _cumsum_prompt(logf_t):
    r, l = logf_t.shape
    return pl.pallas_call(
        _cumsum_kernel,
        out_shape=jax.ShapeDtypeStruct((r, l), F32),
        grid=(1,),
        in_specs=[pl.BlockSpec((r, l), lambda i: (0, 0))],
        out_specs=pl.BlockSpec((r, l), lambda i: (0, 0)),
        compiler_params=_params("arbitrary"),
        name="cumsum_prompt",
    )(logf_t)


def _softmax_first(s, v):
    m = jnp.max(s, axis=1, keepdims=True)
    p = jnp.exp(s - m)
    l = jnp.sum(p, axis=1, keepdims=True)
    acc = jnp.dot(p.astype(BF16), v, preferred_element_type=F32)
    return m, l, acc


def _softmax_update(carry, s, v):
    m, l, acc = carry
    m_new = jnp.maximum(m, jnp.max(s, axis=1, keepdims=True))
    alpha = jnp.exp(m - m_new)
    p = jnp.exp(s - m_new)
    l = alpha * l + jnp.sum(p, axis=1, keepdims=True)
    acc = alpha * acc + jnp.dot(p.astype(BF16), v, preferred_element_type=F32)
    return m_new, l, acc


def _topk_select(gate, n_top, axis):
    pos = lax.broadcasted_iota(jnp.int32, gate.shape, axis)
    sel = jnp.zeros(gate.shape, F32)
    g = gate
    for _ in range(n_top):
        mx = jnp.max(g, axis=axis, keepdims=True)
        idx = jnp.min(jnp.where(g == mx, pos, gate.shape[axis]), axis=axis, keepdims=True)
        hit = pos == idx
        sel = jnp.maximum(sel, jnp.where(hit, jnp.where(mx > NEG_INF, 1.0, 0.0), 0.0))
        g = jnp.where(hit, NEG_INF, g)
    return sel


def _split3(x):
    hi = x.astype(BF16).astype(F32)
    r = x - hi
    mid = r.astype(BF16).astype(F32)
    return hi, mid, r - mid


def _attn_prompt_kernel(qt_ref, kt_ref, vt_ref, x_ref, o_ref, k_ref, qa_ref, *sel_refs, fox):
    t = MOBA_BLOCK
    i = pl.program_id(1)
    own = pl.multiple_of(i * t, t)
    causal = (lax.broadcasted_iota(jnp.int32, (t, t), 0) <= lax.broadcasted_iota(jnp.int32, (t, t), 1))
    pad = jnp.zeros((HEAD_DIM - AUG, t), F32)
    one = jnp.ones((1, t), F32)
    zero = jnp.zeros((1, t), F32)
    heads = range(N_HEADS)

    @pl.when(i == 0)
    def _():
        def build(jt, _):
            st = pl.multiple_of(jt * t, t)
            for h in heads:
                if fox:
                    hi, mid, lo = _split3(x_ref[h:h + 1, pl.ds(st, t)] * LOG2E)
                    aug = jnp.concatenate([one, one, one, -hi, -mid, -lo, zero, zero], axis=0)
                else:
                    aug = jnp.zeros((AUG, t), F32)
                kh = jnp.concatenate([kt_ref[h * HEAD_DIM:(h + 1) * HEAD_DIM, pl.ds(st, t)], aug, pad], axis=0)
                k_ref[h, pl.ds(st, t), :] = kh.T.astype(BF16)
            return 0

        lax.fori_loop(0, kt_ref.shape[1] // t, build, 0)

    for h in heads:
        qh = qt_ref[h * HEAD_DIM:(h + 1) * HEAD_DIM, :]
        if fox:
            aug = jnp.concatenate(_split3(x_ref[h:h + 1, pl.ds(own, t)] * LOG2E) + (one, one, one, zero, zero),
                                  axis=0)
        else:
            aug = jnp.zeros((AUG, t), F32)
            nb = x_ref.shape[1]
            gate = jnp.dot(x_ref[h], qh, precision=lax.Precision.HIGHEST, preferred_element_type=F32)
            blk = lax.broadcasted_iota(jnp.int32, (nb, t), 0)
            sel_refs[0][h] = _topk_select(jnp.where(blk < i, gate, NEG_INF), min(MOBA_TOPK, nb), 0)
        qa_ref[h] = jnp.concatenate([qh * (SCALE * LOG2E), aug, pad], axis=0).astype(BF16)

    def attend(carry, start, n, mask):
        ss = [mask(h, jnp.dot(k_ref[h, pl.ds(start, n * t), :], qa_ref[h], preferred_element_type=F32))
              for h in heads]
        ms = [jnp.max(s, axis=0, keepdims=True) for s in ss]
        if carry is not None:
            ms = [jnp.maximum(carry[3 * h], ms[h]) for h in heads]
        ps = [jnp.exp2(ss[h] - ms[h]) for h in heads]
        pvs = [jnp.dot(vt_ref[h * HEAD_DIM:(h + 1) * HEAD_DIM, pl.ds(start, n * t)], ps[h].astype(BF16),
                       preferred_element_type=F32) for h in heads]
        out = []
        for h in heads:
            l = jnp.sum(ps[h], axis=0, keepdims=True)
            if carry is None:
                out += [ms[h], l, pvs[h]]
            else:
                alpha = jnp.exp2(carry[3 * h] - ms[h])
                out += [ms[h], alpha * carry[3 * h + 1] + l, alpha * carry[3 * h + 2] + pvs[h]]
        return tuple(out)

    def past(n):
        def body(j, carry):
            def mask(h, s):
                if fox:
                    return s
                keep = jnp.concatenate([jnp.broadcast_to(sel_refs[0][h, pl.ds(j * n + k, 1), :], (t, t))
                                        for k in range(n)], axis=0)
                return jnp.where(keep > 0.5, s, NEG_INF)

            return attend(carry, pl.multiple_of(j * (n * t), n * t), n, mask)

        return body

    carry = attend(None, own, 1, lambda h, s: jnp.where(causal, s, NEG_INF))
    n, done = KV_BLOCKS_PER_TRIP, 0
    while n >= 1:
        trips = (i - done) // n
        carry = lax.fori_loop(done // n, done // n + trips, past(n), carry)
        done = done + trips * n
        n //= 2
    ot = jnp.concatenate([carry[3 * h + 2] / carry[3 * h + 1] for h in heads], axis=0)
    o_ref[...] = ot.T


def _attn_prompt(qt, k_all, layer, vtb, which, x, fox):
    b, _, _, l = qt.shape
    nb = l // MOBA_BLOCK
    t = MOBA_BLOCK
    if fox:
        x_spec = pl.BlockSpec((None, N_HEADS, l), lambda bi, i: (bi, 0, 0))
        scratch = []
    else:
        x_spec = pl.BlockSpec((None, N_HEADS, nb, HEAD_DIM), lambda bi, i: (bi, 0, 0, 0))
        scratch = [pltpu.VMEM((N_HEADS, nb, t), F32)]
    return pl.pallas_call(
        functools.partial(_attn_prompt_kernel, fox=fox),
        out_shape=jax.ShapeDtypeStruct((b * l, MIX), F32),
        grid=(b, nb),
        in_specs=[pl.BlockSpec((None, None, MIX, t), lambda bi, i: (bi, which, 0, i)),
                  pl.BlockSpec((None, None, MIX, l), lambda bi, i: (layer, bi, 0, 0)),
                  pl.BlockSpec((None, None, MIX, l), lambda bi, i: (bi, which, 0, 0)),
                  x_spec],
        out_specs=pl.BlockSpec((t, MIX), lambda bi, i: (bi * nb + i, 0)),
        scratch_shapes=[pltpu.VMEM((N_HEADS, l, 2 * HEAD_DIM), BF16),
                        pltpu.VMEM((N_HEADS, 2 * HEAD_DIM, t), BF16)] + scratch,
        compiler_params=_params("parallel", "arbitrary"),
        name="fox_prompt" if fox else "moba_prompt",
    )(qt, k_all, vtb, x)


def _gmlp_kernel(u_ref, v_ref, w_ref, b_ref, o_ref, *, c, cps):
    grp = lax.broadcasted_iota(jnp.int32, (c, MIX), 1) // HEAD_DIM
    row = lax.broadcasted_iota(jnp.int32, (c, c), 0)
    col = lax.broadcasted_iota(jnp.int32, (c, c), 1)
    w_cat = jnp.concatenate([jnp.where(col <= row, w_ref[g], 0.0) for g in range(N_HEADS)], axis=1).astype(BF16)
    bias = b_ref[...]
    for ci in range(cps):
        rows = slice(ci * c, (ci + 1) * c)
        v = v_ref[rows, :]
        v_cat = jnp.concatenate([jnp.where(grp == g, v, 0.0) for g in range(N_HEADS)], axis=0)
        mixed = jnp.dot(w_cat, v_cat.astype(BF16), preferred_element_type=F32)
        o_ref[rows, :] = u_ref[rows, :] * (mixed + bias)


def _gmlp(proj, u_col, v_col, ws, bs, c):
    n = proj.shape[0]
    bias = jnp.repeat(bs.T, HEAD_DIM, axis=1)
    cps = 8 if n % (8 * c) == 0 else 1
    col = lambda g: pl.BlockSpec((cps * c, MIX), lambda i: (i, g))
    return pl.pallas_call(
        functools.partial(_gmlp_kernel, c=c, cps=cps),
        out_shape=jax.ShapeDtypeStruct((n, MIX), F32),
        grid=(n // (cps * c),),
        in_specs=[col(u_col), col(v_col), pl.BlockSpec((N_HEADS, c, c), lambda i: (0, 0, 0)),
                  pl.BlockSpec((c, MIX), lambda i: (0, 0))],
        out_specs=col(0),
        compiler_params=_params("parallel"),
        name="gmlp",
    )(proj, proj, ws, bias)


def _retention_kernel(q_ref, kt_ref, v_ref, g_ref, s0_ref, dm_ref, qd_ref, kd_ref, cd_ref, gn_ref,
                      o_ref, s_ref, state, *, c, cps):
    step = pl.program_id(1)

    @pl.when(step == 0)
    def _():
        state[...] = s0_ref[...]

    for h in range(N_HEADS):
        s = state[h]
        for ci in range(cps):
            rows = slice(ci * c, (ci + 1) * c)
            qc = q_ref[h, rows, :].astype(BF16)
            vc = v_ref[h, rows, :].astype(BF16)
            kt = kt_ref[h, :, rows]
            att = jnp.dot(qc, kt.astype(BF16), preferred_element_type=F32) * dm_ref[h]
            o = (jnp.dot(att.astype(BF16), vc, preferred_element_type=F32)
                 + jnp.dot(qc, s.astype(BF16), preferred_element_type=F32) * qd_ref[h])
            s = s * cd_ref[h] + jnp.dot((kt * kd_ref[h]).astype(BF16), vc, preferred_element_type=F32)
            mu = jnp.mean(o, axis=-1, keepdims=True)
            oc = o - mu
            var = jnp.mean(oc * oc, axis=-1, keepdims=True)
            o_ref[h, rows, :] = jax.nn.silu(g_ref[h, rows, :]) * (oc * lax.rsqrt(var + EPS) * gn_ref[h])
        state[h] = s

        @pl.when(step == pl.num_programs(1) - 1)
        def _():
            s_ref[h] = s


def _retention(q, kt, v, g, s0, gn_g, c, cps):
    b, h, l, dh = q.shape
    lg = jnp.log(1.0 - 2.0 ** (-5.0 - jnp.arange(h, dtype=F32)))
    j = jnp.arange(c, dtype=F32)
    diff = j[:, None] - j[None, :]
    dmat = jnp.where(diff[None] >= 0, jnp.exp(jnp.maximum(diff, 0.0)[None] * lg[:, None, None]), 0.0)
    q_dec = jnp.broadcast_to(jnp.exp((j[None, :] + 1.0) * lg[:, None])[:, :, None], (h, c, dh))
    k_dec = jnp.broadcast_to(jnp.exp((c - 1.0 - j)[None, :] * lg[:, None])[:, None, :], (h, dh, c))
    chunk_dec = jnp.broadcast_to(jnp.exp(c * lg)[:, None, None], (h, dh, dh))
    r = c * cps
    rows = pl.BlockSpec((None, h, r, dh), lambda bi, i: (bi, 0, i, 0))
    per_b = pl.BlockSpec((None, h, dh, dh), lambda bi, i: (bi, 0, 0, 0))
    table = lambda s1, s2: pl.BlockSpec((h, s1, s2), lambda bi, i: (0, 0, 0))
    return pl.pallas_call(
        functools.partial(_retention_kernel, c=c, cps=cps),
        out_shape=(jax.ShapeDtypeStruct((b, h, l, dh), F32), jax.ShapeDtypeStruct((b, h, dh, dh), F32)),
        grid=(b, l // r),
        in_specs=[rows, pl.BlockSpec((None, h, dh, r), lambda bi, i: (bi, 0, 0, i)), rows, rows,
                  per_b, table(c, c), table(c, dh), table(dh, c), table(dh, dh), table(1, dh)],
        out_specs=(rows, per_b),
        scratch_shapes=[pltpu.VMEM((h, dh, dh), F32)],
        compiler_params=_params("parallel", "arbitrary"),
        name="retention",
    )(q, kt, v, g, s0, dmat, q_dec, k_dec, chunk_dec, gn_g.reshape(h, 1, dh))


def _retention_rows_kernel(q_ref, k_ref, v_ref, g_ref, s0_ref, dm_ref, qd_ref, kd_ref, cd_ref, bd_ref, gn_ref,
                           o_ref, s_ref, state, *, c, cps):
    step = pl.program_id(1)

    @pl.when(step == 0)
    def _():
        state[...] = s0_ref[...]

    heads = range(N_HEADS)
    grp = lax.broadcasted_iota(jnp.int32, (c, MIX), 1) // HEAD_DIM
    bd = bd_ref[...]
    avg = (bd * (1.0 / HEAD_DIM)).astype(BF16)
    s = state[...]
    for ci in range(cps):
        rows = slice(ci * c, (ci + 1) * c)
        q, k, v = q_ref[rows, :], k_ref[rows, :], v_ref[rows, :]
        kb, vb = k.astype(BF16), v.astype(BF16)
        att = jnp.concatenate([_nt_dot(jnp.where(grp == h, q, 0.0).astype(BF16), kb) * dm_ref[h] for h in heads],
                              axis=1)
        v_cat = jnp.concatenate([jnp.where(grp == h, v, 0.0).astype(BF16) for h in heads], axis=0)
        o = (jnp.dot(att.astype(BF16), v_cat, preferred_element_type=F32)
             + jnp.dot(q.astype(BF16), s.astype(BF16), preferred_element_type=F32) * qd_ref[...])
        s = s * cd_ref[...] + bd * jnp.dot((k * kd_ref[...]).T.astype(BF16), vb, preferred_element_type=F32)
        mu = jnp.dot(o.astype(BF16), avg, preferred_element_type=F32)
        oc = o - mu
        var = jnp.dot((oc * oc).astype(BF16), avg, preferred_element_type=F32)
        o_ref[rows, :] = jax.nn.silu(g_ref[rows, :]) * (oc * lax.rsqrt(var + EPS) * gn_ref[...])
    state[...] = s

    @pl.when(step == pl.num_programs(1) - 1)
    def _():
        s_ref[...] = s


def _retention_rows(proj, cols, b, l, gn_g, c, cps):
    h, dh = N_HEADS, HEAD_DIM
    lg = jnp.log(1.0 - 2.0 ** (-5.0 - jnp.arange(h, dtype=F32)))
    j = jnp.arange(c, dtype=F32)
    diff = j[:, None] - j[None, :]
    dmat = jnp.where(diff[None] >= 0, jnp.exp(jnp.maximum(diff, 0.0)[None] * lg[:, None, None]), 0.0)
    lanes = lambda x: jnp.repeat(x, dh, axis=-1)
    q_dec = lanes(jnp.exp((j[:, None] + 1.0) * lg[None, :]))
    k_dec = lanes(jnp.exp((c - 1.0 - j)[:, None] * lg[None, :]))
    chunk_dec = jnp.broadcast_to(lanes(jnp.exp(c * lg)[None, :]), (MIX, MIX))
    head_of = jnp.arange(MIX) // dh
    bd = (head_of[:, None] == head_of[None, :]).astype(F32)
    r = c * cps
    per = l // r
    col = lambda g: pl.BlockSpec((r, MIX), lambda bi, i: (bi * per + i, g))
    const = lambda *shape: pl.BlockSpec(shape, lambda bi, i: (0,) * len(shape))
    state = pl.BlockSpec((None, MIX, MIX), lambda bi, i: (bi, 0, 0))
    od, s_bd = pl.pallas_call(
        functools.partial(_retention_rows_kernel, c=c, cps=cps),
        out_shape=(jax.ShapeDtypeStruct((b * l, MIX), F32), jax.ShapeDtypeStruct((b, MIX, MIX), F32)),
        grid=(b, per),
        in_specs=[col(cols[0]), col(cols[1]), col(cols[2]), col(cols[3]), state, const(h, c, c),
                  const(c, MIX), const(c, MIX), const(MIX, MIX), const(MIX, MIX), const(1, MIX)],
        out_specs=(col(0), state),
        scratch_shapes=[pltpu.VMEM((MIX, MIX), F32)],
        compiler_params=_params("parallel", "arbitrary"),
        name="retention_rows",
    )(proj, proj, proj, proj, jnp.zeros((b, MIX, MIX), F32), dmat, q_dec, k_dec, chunk_dec, bd, gn_g[None])
    s5 = s_bd.reshape(b, h, dh, h, dh)
    return od, jnp.stack([s5[:, i, :, i, :] for i in range(h)], axis=1)


def _merge_kernel(h_ref, g_ref, oa_ref, ob_ref, oc_ref, od_ref, wm_ref, wb_ref, wo_ref, o_ref):
    h = h_ref[...]
    d = h.shape[1]
    xn = _rms(h, g_ref[...]).astype(BF16)
    z = None
    for n, br_ref in enumerate((oa_ref, ob_ref, oc_ref, od_ref)):
        gate = jax.nn.sigmoid(jnp.dot(xn, wm_ref[:, n * d:(n + 1) * d], preferred_element_type=F32))
        br = jnp.dot(br_ref[...].astype(BF16), wb_ref[n], preferred_element_type=F32)
        z = gate * br if z is None else z + gate * br
    o_ref[...] = h + jnp.dot(z.astype(BF16), wo_ref[...], preferred_element_type=F32)


def _merge(h, g, branches, w_merge, w_branch, w_out, tm):
    n, d = h.shape
    rows = lambda w: pl.BlockSpec((tm, w), lambda i: (i, 0))
    return pl.pallas_call(
        _merge_kernel,
        out_shape=jax.ShapeDtypeStruct((n, d), F32),
        grid=(n // tm,),
        in_specs=[rows(d), pl.BlockSpec((1, d), lambda i: (0, 0)), rows(MIX), rows(MIX), rows(MIX), rows(MIX),
                  pl.BlockSpec(w_merge.shape, lambda i: (0, 0)),
                  pl.BlockSpec(w_branch.shape, lambda i: (0, 0, 0)),
                  pl.BlockSpec(w_out.shape, lambda i: (0, 0))],
        out_specs=rows(d),
        compiler_params=_params("parallel"),
        name="merge",
    )(h, g, *branches, w_merge, w_branch, w_out)


def _ffn_kernel(h_ref, g_ref, wg_ref, wu_ref, wo_ref, o_ref, xn_ref, acc_ref):
    k = pl.program_id(1)

    @pl.when(k == 0)
    def _():
        xn_ref[...] = _rms(h_ref[...], g_ref[...]).astype(BF16)
        acc_ref[...] = h_ref[...]

    xn = xn_ref[...]
    tf = wg_ref.shape[1]
    slab = FFN_SLAB if tf % FFN_SLAB == 0 else tf
    part = None
    for lo in range(0, tf, slab):
        gate = jnp.dot(xn, wg_ref[:, lo:lo + slab], preferred_element_type=F32)
        up = jnp.dot(xn, wu_ref[:, lo:lo + slab], preferred_element_type=F32)
        act = (jax.nn.silu(gate) * up).astype(BF16)
        out = jnp.dot(act, wo_ref[lo:lo + slab, :], preferred_element_type=F32)
        part = out if part is None else part + out
    acc_ref[...] += part

    @pl.when(k == pl.num_programs(1) - 1)
    def _():
        o_ref[...] = acc_ref[...]


def _ffn(h, g, w_in, w_out, tm, tf):
    n, d = h.shape
    d_ff = w_out.shape[0]
    nk = d_ff // tf
    return pl.pallas_call(
        _ffn_kernel,
        out_shape=jax.ShapeDtypeStruct((n, d), F32),
        grid=(n // tm, nk),
        in_specs=[pl.BlockSpec((tm, d), lambda i, k: (i, 0)), pl.BlockSpec((1, d), lambda i, k: (0, 0)),
                  pl.BlockSpec((d, tf), lambda i, k: (0, k)), pl.BlockSpec((d, tf), lambda i, k: (0, nk + k)),
                  pl.BlockSpec((tf, d), lambda i, k: (k, 0))],
        out_specs=pl.BlockSpec((tm, d), lambda i, k: (i, 0)),
        scratch_shapes=[pltpu.VMEM((tm, d), BF16), pltpu.VMEM((tm, d), F32)],
        compiler_params=_params("parallel", "arbitrary"),
        name="ffn",
    )(h, g, w_in, w_in, w_out)


def _ple_kernel(h_ref, g_ref, p_ref, wg_ref, wp_ref, gf_ref, o_ref, *, final):
    h = h_ref[...]
    xn = _rms(h, g_ref[...]).astype(BF16)
    gate = jax.nn.sigmoid(jnp.dot(xn, wg_ref[...], preferred_element_type=F32))
    emb = jnp.dot(p_ref[...].astype(BF16), wp_ref[...], preferred_element_type=F32)
    out = h + gate * emb
    o_ref[...] = _rms(out, gf_ref[...]) if final else out


def _ple(h, g, p, w_gate, w_proj, g_final, tm, final):
    n, d = h.shape
    p, layer = p
    dp = p.shape[2]
    vec = pl.BlockSpec((1, d), lambda i: (0, 0))
    return pl.pallas_call(
        functools.partial(_ple_kernel, final=final),
        out_shape=jax.ShapeDtypeStruct((n, d), F32),
        grid=(n // tm,),
        in_specs=[pl.BlockSpec((tm, d), lambda i: (i, 0)), vec, pl.BlockSpec((None, tm, dp), lambda i: (layer, i, 0)),
                  pl.BlockSpec((d, d), lambda i: (0, 0)), pl.BlockSpec((dp, d), lambda i: (0, 0)), vec],
        out_specs=pl.BlockSpec((tm, d), lambda i: (i, 0)),
        compiler_params=_params("parallel"),
        name="ple",
    )(h, g, p, w_gate, w_proj, g_final)


def _stream_pages(pt_ref, streams, sems, layer, pp, n_chunks, n_seq, reverse, step):
    total = n_seq * n_chunks
    assert total % 2 == 0

    def locate(g):
        b = g // n_chunks
        c = g - b * n_chunks
        return b, (n_chunks - 1 - c if reverse else c)

    def copies(page_of, slot):
        return [pltpu.make_async_copy(hbm.at[layer, page_of(k)], buf.at[slot, k], sems.at[n, slot])
                for k in range(pp) for n, (hbm, buf) in enumerate(streams)]

    def fetch(g, slot):
        b, c = locate(g)
        for cp in copies(lambda k: pt_ref[b, c * pp + k], slot):
            cp.start()

    def pair(i, _):
        for slot in (0, 1):
            g = 2 * i + slot

            @pl.when(g + 1 < total)
            def _():
                fetch(g + 1, 1 - slot)

            for cp in copies(lambda k: 0, slot):
                cp.wait()
            step(g, *locate(g), slot)
        return 0

    fetch(0, 0)
    lax.fori_loop(0, total // 2, pair, 0)


def _paged_call(body, page_table, caches, dense, out_shape, scratch, name, page_shapes):
    whole = lambda x: pl.BlockSpec(x.shape, lambda i, pt, n=len(x.shape): (0,) * n)
    grid_spec = pltpu.PrefetchScalarGridSpec(
        num_scalar_prefetch=1, grid=(1,),
        in_specs=[pl.BlockSpec(memory_space=pl.ANY)] * len(caches) + [whole(x) for x in dense],
        out_specs=whole(out_shape),
        scratch_shapes=[pltpu.VMEM((2, PAGES_PER_STEP) + shape, F32) for shape in page_shapes]
        + [pltpu.SemaphoreType.DMA((len(caches), 2))] + scratch)
    return pl.pallas_call(body, out_shape=out_shape, grid_spec=grid_spec,
                          compiler_params=_params("arbitrary"), name=name)(page_table, *caches, *dense)


def _kmean_sample_kernel(pt_ref, k_hbm, o_ref, kbuf, sems, *, layer, pp, n_chunks):
    per_block = MOBA_BLOCK // PAGE_SIZE
    nblk = pp // per_block
    lane = lax.broadcasted_iota(jnp.int32, (MIX, LANES), 1)

    def step(g, b, c, slot):
        @pl.when(c == 0)
        def _():
            o_ref[b] = jnp.zeros((MIX, LANES), F32)

        acc = o_ref[b]
        for blk in range(nblk):
            tot = kbuf[slot, blk * per_block]
            for k in range(1, per_block):
                tot = tot + kbuf[slot, blk * per_block + k]
            mean = jnp.sum(tot, axis=1, keepdims=True) * (1.0 / MOBA_BLOCK)
            acc = jnp.where(lane == c * nblk + blk, mean, acc)
        o_ref[b] = acc

    _stream_pages(pt_ref, [(k_hbm, kbuf)], sems, layer, pp, n_chunks, o_ref.shape[0], False, step)


def _kmean_sample(cache_k, layer, page_table):
    db, n_pages = page_table.shape
    pp = PAGES_PER_STEP
    assert n_pages * PAGE_SIZE // MOBA_BLOCK <= LANES
    return _paged_call(
        functools.partial(_kmean_sample_kernel, layer=layer, pp=pp, n_chunks=n_pages // pp),
        page_table, [cache_k], [], jax.ShapeDtypeStruct((db, MIX, LANES), F32), [], "kmean_sample",
        [(MIX, PAGE_SIZE)])


def _lane_suffix_sum(x):
    n = x.shape[-1]
    lane = lax.broadcasted_iota(jnp.int32, x.shape, x.ndim - 1)
    s = 1
    while s < n:
        x = x + jnp.where(lane < n - s, pltpu.roll(x, n - s, x.ndim - 1), 0.0)
        s *= 2
    return x


def _expand_heads(x, t):
    return jnp.concatenate([jnp.broadcast_to(x[h:h + 1, :], (t, x.shape[1])) for h in range(N_HEADS)], axis=0)


def _decode_attn_kernel(pt_ref, *refs, layer, pp, n_chunks, t, fox, nb):
    if fox:
        (k_hbm, v_hbm, f_hbm, q_ref, kn_ref, vn_ref, fn_ref, o_ref, kbuf, vbuf, fbuf, sems,
         s_ref, m_ref, l_ref, acc_ref, cq_ref, tail_ref) = refs
        streams = [(k_hbm, kbuf), (v_hbm, vbuf), (f_hbm, fbuf)]
    else:
        (k_hbm, v_hbm, q_ref, kn_ref, vn_ref, km_ref, o_ref, kbuf, vbuf, sems,
         s_ref, m_ref, l_ref, acc_ref, sel_ref) = refs
        streams = [(k_hbm, kbuf), (v_hbm, vbuf)]
    r = N_HEADS * t

    def step(g, b, c, slot):
        first = c == (n_chunks - 1 if fox else 0)
        last = c == (0 if fox else n_chunks - 1)
        q = q_ref[b]
        qb = (q * SCALE).astype(BF16)

        @pl.when(first)
        def _():
            s = jnp.dot(qb, kn_ref[b].astype(BF16), preferred_element_type=F32)
            key = lax.broadcasted_iota(jnp.int32, (r, LANES), 1)
            qi = lax.broadcasted_iota(jnp.int32, (r, LANES), 0) % t
            if fox:
                cn = _expand_heads(_lane_cumsum(fn_ref[b]), t)
                cq = jnp.sum(jnp.where(key == qi, cn, 0.0), axis=1, keepdims=True)
                cq_ref[...] = jnp.broadcast_to(cq, cq_ref.shape)
                tail_ref[...] = jnp.zeros(tail_ref.shape, F32)
                s = s + cq - cn
            s = jnp.where(key <= qi, s, NEG_INF)
            m = jnp.max(s, axis=1, keepdims=True)
            p = jnp.exp(s - m)
            m_ref[...] = jnp.broadcast_to(m, m_ref.shape)
            l_ref[...] = jnp.broadcast_to(jnp.sum(p, axis=1, keepdims=True), l_ref.shape)
            acc_ref[...] = _nt_dot(p.astype(BF16), vn_ref[b].astype(BF16))
            if not fox:
                gate = jnp.dot(q, km_ref[b], precision=lax.Precision.HIGHEST, preferred_element_type=F32)
                blk = lax.broadcasted_iota(jnp.int32, gate.shape, 1)
                sel_ref[...] = _topk_select(jnp.where(blk < nb, gate, NEG_INF), min(MOBA_TOPK, nb), 1)

        for k in range(pp):
            s_ref[:, k * PAGE_SIZE:(k + 1) * PAGE_SIZE] = jnp.dot(qb, kbuf[slot, k].astype(BF16),
                                                                  preferred_element_type=F32)
        s = s_ref[...]
        if fox:
            f = jnp.concatenate([fbuf[slot, k] for k in range(pp)], axis=1)
            suffix = _lane_suffix_sum(f)
            tail = tail_ref[:, :1]
            s = s + jnp.concatenate([cq_ref[...]] * pp, axis=1) + _expand_heads(suffix - f + tail, t)
            tail_ref[...] = jnp.broadcast_to(tail + suffix[:, :1], tail_ref.shape)
        else:
            per_block = MOBA_BLOCK // PAGE_SIZE
            nblk = pp // per_block
            sel = sel_ref[...]
            blk = lax.broadcasted_iota(jnp.int32, sel.shape, 1)
            cols = []
            for j in range(nblk):
                chosen = jnp.sum(jnp.where(blk == c * nblk + j, sel, 0.0), axis=1, keepdims=True)
                cols.append(jnp.broadcast_to(chosen, (r, MOBA_BLOCK)))
            s = jnp.where(jnp.concatenate(cols, axis=1) > 0.5, s, NEG_INF)
        m = m_ref[:, :1]
        m_new = jnp.maximum(m, jnp.max(s, axis=1, keepdims=True))
        alpha = jnp.exp(m - m_new)
        p = jnp.exp(s - m_new).astype(BF16)
        l_new = alpha * l_ref[:, :1] + jnp.sum(p.astype(F32), axis=1, keepdims=True)
        acc = alpha * acc_ref[...]
        for k in range(pp):
            acc = acc + _nt_dot(p[:, k * PAGE_SIZE:(k + 1) * PAGE_SIZE], vbuf[slot, k].astype(BF16))
        m_ref[...] = jnp.broadcast_to(m_new, m_ref.shape)
        l_ref[...] = jnp.broadcast_to(l_new, l_ref.shape)
        acc_ref[...] = acc

        @pl.when(last)
        def _():
            o = acc / l_new
            grp = lax.broadcasted_iota(jnp.int32, (t, MIX), 1) // HEAD_DIM
            out = jnp.zeros((t, MIX), F32)
            for h in range(N_HEADS):
                out = out + jnp.where(grp == h, o[h * t:(h + 1) * t, :], 0.0)
            o_ref[b] = out

    _stream_pages(pt_ref, streams, sems, layer, pp, n_chunks, q_ref.shape[0], fox, step)


def _decode_attn(cache_k, cache_v, layer, page_table, q_blk, k_new, v_new, extra, fox):
    db, n_pages = page_table.shape
    pp = PAGES_PER_STEP
    r = q_blk.shape[1]
    t = r // N_HEADS
    scratch = [pltpu.VMEM((r, pp * PAGE_SIZE), F32), pltpu.VMEM((r, LANES), F32), pltpu.VMEM((r, LANES), F32),
               pltpu.VMEM((r, MIX), F32), pltpu.VMEM((r, LANES), F32)]
    caches, pages = [cache_k, cache_v], [(MIX, PAGE_SIZE), (MIX, PAGE_SIZE)]
    if fox:
        caches, pages = caches + [extra[0]], pages + [(N_HEADS, PAGE_SIZE)]
        scratch += [pltpu.VMEM((N_HEADS, LANES), F32)]
    return _paged_call(
        functools.partial(_decode_attn_kernel, layer=layer, pp=pp, n_chunks=n_pages // pp, t=t, fox=fox,
                          nb=n_pages * PAGE_SIZE // MOBA_BLOCK),
        page_table, caches, [q_blk, k_new, v_new, extra[-1]], jax.ShapeDtypeStruct((db, t, MIX), F32), scratch,
        "fox_sample" if fox else "moba_sample", pages)


def _to_heads(x, b, l):
    return x.reshape(b, l, N_HEADS, HEAD_DIM).transpose(0, 2, 1, 3)


def _from_heads(x):
    b, h, l, dh = x.shape
    return x.transpose(0, 2, 1, 3).reshape(b * l, h * dh)


def _split_proj(proj):
    cols = [proj[:, g * MIX:(g + 1) * MIX] for g in range(N_PROJ_GROUPS)]
    logf = proj[:, N_PROJ_GROUPS * MIX:N_PROJ_GROUPS * MIX + N_HEADS]
    return cols, logf


def _tail(h, p, prm, branches, final, tm):
    h = _merge(h, prm["norm_mix"], branches, prm["w_merge"], prm["w_branch"], prm["w_out"], tm)
    d_ff = prm["w_ffn_out"].shape[0]
    tf = d_ff
    h = _ffn(h, prm["norm_ffn"], prm["w_ffn_in"], prm["w_ffn_out"], min(512, h.shape[0]), tf)
    return _ple(h, prm["norm_ple"], p, prm["w_ple_gate"], prm["w_ple_proj"], prm["norm_final"], tm, final)


def _prompt_layer(h, p, prm, tabs, kv_all, layer, b, l, final):
    tm = 512
    proj, qt, *kv_all, vtb = _proj(h, prm["norm_mix"], prm["w_in"], prm["b_forget"], prm["ln_g"], prm["ln_b"],
                                   tabs, tm, kv_all=kv_all, layer=layer)
    logf = proj[:, N_PROJ_GROUPS * MIX:N_PROJ_GROUPS * MIX + N_HEADS]
    kmean = _kmean_prompt(proj, 1, b, l).reshape(b, -1, N_HEADS, HEAD_DIM).transpose(0, 2, 1, 3)
    oa = _attn_prompt(qt, kv_all[0], layer, vtb, 0, kmean, fox=False)
    c = _cumsum_prompt(logf.reshape(b, l, N_HEADS).transpose(0, 2, 1).reshape(b * N_HEADS, l))
    ob = _attn_prompt(qt, kv_all[2], layer, vtb, 1, c.reshape(b, N_HEADS, l), fox=True)
    oc = _gmlp(proj, 6, 7, prm["gmlp_ws"], prm["gmlp_bs"], GMLP_CHUNK)
    cps = 8 if l % (8 * RET_CHUNK) == 0 else 1
    od, s_new = _retention_rows(proj, (8, 9, 10, 11), b, l, prm["ret_gn_g"], RET_CHUNK, cps)
    h = _tail(h, p, prm, (oa, ob, oc, od), final, tm)
    return h, kv_all, (logf.reshape(b, l, N_HEADS), s_new)


def _block_diag_q(q, db, t):
    grp = jnp.arange(MIX) // HEAD_DIM
    mask = (grp[None, None, :] == jnp.arange(N_HEADS)[:, None, None]).astype(F32)
    return (q.reshape(db, 1, t, MIX) * mask[None]).reshape(db, N_HEADS * t, MIX)


def _new_keys_t(x, db, t):
    return jnp.pad(x.reshape(db, t, MIX).transpose(0, 2, 1), ((0, 0), (0, 0), (0, LANES - t)))


def _sample_layer(h, p, prm, tabs, caches, layer, page_table, db, t, final):
    cache_k_a, cache_v_a, cache_k_b, cache_v_b, cache_logf_t, state = caches
    n = db * t
    (proj,) = _proj(h, prm["norm_mix"], prm["w_in"], prm["b_forget"], prm["ln_g"], prm["ln_b"], tabs, n)
    (qa, ka, va, qb, kb, vb, _, vn, qd, kd, vd, gd), logf = _split_proj(proj)
    kmean = _kmean_sample(cache_k_a, layer, page_table)
    oa = _decode_attn(cache_k_a, cache_v_a, layer, page_table, _block_diag_q(qa, db, t),
                      _new_keys_t(ka, db, t), _new_keys_t(va, db, t), (kmean,), fox=False)
    logf_new = jnp.pad(logf.reshape(db, t, N_HEADS).transpose(0, 2, 1), ((0, 0), (0, 0), (0, LANES - t)))
    ob = _decode_attn(cache_k_b, cache_v_b, layer, page_table, _block_diag_q(qb, db, t),
                      _new_keys_t(kb, db, t), _new_keys_t(vb, db, t), (cache_logf_t, logf_new), fox=True)
    oc = _gmlp(proj, 6, 7, prm["gmlp_ws"][:, :t, :t], prm["gmlp_bs"][:, :t], t)
    hd = lambda x: _to_heads(x, db, t)
    od, s_new = _retention(hd(qd), hd(kd).transpose(0, 1, 3, 2), hd(vd), hd(gd), state[layer],
                           prm["ret_gn_g"], t, 1)
    h = _tail(h, p, prm, (oa.reshape(n, MIX), ob.reshape(n, MIX), oc, _from_heads(od)), final, n)
    rows = (ka.reshape(db, t, N_HEADS, HEAD_DIM), va.reshape(db, t, N_HEADS, HEAD_DIM),
            kb.reshape(db, t, N_HEADS, HEAD_DIM), vb.reshape(db, t, N_HEADS, HEAD_DIM),
            logf.reshape(db, t, N_HEADS), s_new, vn.reshape(db, t, MIX))
    return h, rows


def kernel(x_prompt, x_sample, cache_k_a, cache_v_a, cache_k_b, cache_v_b, cache_logf_b, state_ret, page_table, p_prompt, p_sample, norm_mix, w_in, b_forget, gmlp_ln_g, gmlp_ln_b, gmlp_ws, gmlp_bs, ret_gn_g, w_branch, w_merge, w_out, norm_ffn, w_ffn_in, w_ffn_out, norm_ple, w_ple_gate, w_ple_proj, norm_final):
    b, l, d = x_prompt.shape
    db, t, _ = x_sample.shape
    depth = w_in.shape[0]
    n_pages = page_table.shape[1]
    past_len = n_pages * PAGE_SIZE
    assert d == N_HEADS * MIX and l % MOBA_BLOCK == 0
    assert past_len % MOBA_BLOCK == 0 and t <= LANES and t % 8 == 0 and t < MOBA_BLOCK
    assert n_pages % PAGES_PER_STEP == 0

    pos_p = jnp.arange(l, dtype=jnp.int32)
    pos_s = jnp.tile(past_len + jnp.arange(t, dtype=jnp.int32), db)
    tabs_p = _rope_tables(pos_p, ROT_DIMS, ROPE_THETA) + _rope_tables(pos_p, HEAD_DIM, RET_THETA)
    tabs_s = _rope_tables(pos_s, ROT_DIMS, ROPE_THETA) + _rope_tables(pos_s, HEAD_DIM, RET_THETA)

    pool = cache_k_a.shape[1]
    pages = lambda x: x.transpose(0, 1, 3, 4, 2).reshape(depth, pool, MIX, PAGE_SIZE)
    caches = (pages(cache_k_a), pages(cache_v_a), pages(cache_k_b), pages(cache_v_b),
              cache_logf_b.transpose(0, 1, 3, 2), state_ret)

    hp = x_prompt.reshape(b * l, d)
    hs = x_sample.reshape(db * t, d)
    rows_p, rows_s = [], []
    kv_all = [jnp.zeros((depth, b, MIX, l), F32) for _ in range(N_KV_OUT)]
    f_lo, f_hi = 6 * MIX, 6 * MIX + N_HEADS
    w_in_pad = jnp.concatenate(
        [w_in[:, :, :f_lo].astype(BF16), w_in[:, :, f_hi:].astype(BF16),
         jnp.pad(w_in[:, :, f_lo:f_hi].astype(BF16), ((0, 0), (0, 0), (0, LANES - N_HEADS)))], axis=2)
    for i in range(depth):
        prm = dict(
            norm_mix=norm_mix[i][None], w_in=w_in_pad[i],
            b_forget=jnp.pad(b_forget[i], (0, LANES - N_HEADS))[None],
            ln_g=gmlp_ln_g[i][None], ln_b=gmlp_ln_b[i][None], gmlp_ws=gmlp_ws[i], gmlp_bs=gmlp_bs[i],
            ret_gn_g=ret_gn_g[i], w_branch=w_branch[i].astype(BF16), w_merge=w_merge[i].astype(BF16),
            w_out=w_out[i].astype(BF16), norm_ffn=norm_ffn[i][None], w_ffn_in=w_ffn_in[i].astype(BF16),
            w_ffn_out=w_ffn_out[i].astype(BF16), norm_ple=norm_ple[i][None],
            w_ple_gate=w_ple_gate[i].astype(BF16), w_ple_proj=w_ple_proj[i].astype(BF16),
            norm_final=norm_final[None])
        final = i == depth - 1
        hp, kv_all, rp = _prompt_layer(hp, (p_prompt.reshape(depth, b * l, -1), i), prm, tabs_p, kv_all, i,
                                       b, l, final)
        hs, rs = _sample_layer(hs, (p_sample.reshape(depth, db * t, -1), i), prm, tabs_s, caches, i, page_table,
                               db, t, final)
        rows_p.append(rp)
        rows_s.append(rs)
    stack = lambda rows, j: jnp.stack([r[j] for r in rows], axis=0)
    kv_out = tuple(x.reshape(depth, b, N_HEADS, HEAD_DIM, l).transpose(0, 1, 4, 2, 3) for x in kv_all)
    return ((hp.reshape(b, l, d), hs.reshape(db, t, d)) + kv_out
            + tuple(stack(rows_p, j) for j in range(2)) + tuple(stack(rows_s, j) for j in range(7)))
```

```python
import functools

import jax
import jax.numpy as jnp
from jax import lax
from jax.experimental import pallas as pl
from jax.experimental.pallas import tpu as pltpu

F32 = jnp.float32
BF16 = jnp.bfloat16

N_HEADS = 4
HEAD_DIM = 64
MIX = N_HEADS * HEAD_DIM
ROT_DIMS = HEAD_DIM // 4
ROPE_THETA = 500000.0
RET_THETA = 10000.0
MOBA_BLOCK = 256
MOBA_TOPK = 3
GMLP_CHUNK = 128
RET_CHUNK = 128
PAGE_SIZE = 128
EPS = 1e-6
SCALE = HEAD_DIM ** -0.5
NEG_INF = float("-inf")

LANES = 128
VMEM_LIMIT = 56 * 1024 * 1024
PAGES_PER_STEP = 32
N_PROJ_GROUPS = 12
N_ATTN_GROUPS = 6
N_KV_OUT = 4
W_IN_PAD = N_PROJ_GROUPS * MIX + LANES
LOG2E = 1.4426950408889634
AUG = 8
FFN_SLAB = 256
KV_BLOCKS_PER_TRIP = 4


def _params(*sem):
    return pltpu.CompilerParams(dimension_semantics=sem, vmem_limit_bytes=VMEM_LIMIT)


def _rms(x, g):
    return x * lax.rsqrt(jnp.mean(x * x, axis=-1, keepdims=True) + EPS) * g


def _nt_dot(a, b, precision=None):
    return lax.dot_general(a, b, (((1,), (1,)), ((), ())), precision=precision,
                           preferred_element_type=F32)


def _log_sigmoid(x):
    return jnp.minimum(x, 0.0) - jnp.log(1.0 + jnp.exp(-jnp.abs(x)))


def _proj_kernel(h_ref, g_ref, w_ref, bf_ref, lng_ref, lnb_ref, ca_ref, sa_ref, cd_ref, sd_ref, *refs):
    o_ref = refs[0] if len(refs) == 1 else refs[N_KV_OUT]
    t_refs = refs[N_KV_OUT + 1:]
    xn = _rms(h_ref[...], g_ref[...]).astype(BF16)
    lane = lax.broadcasted_iota(jnp.int32, (1, MIX), 1) & (HEAD_DIM - 1)

    def rope(y, c_ref, s_ref, half):
        c = jnp.concatenate([c_ref[...], c_ref[...]], axis=1)
        s = jnp.concatenate([s_ref[...], s_ref[...]], axis=1)
        rot = jnp.where(lane < half, pltpu.roll(y, MIX - half, 1), pltpu.roll(y, half, 1))
        return y * c + rot * s

    for grp in range(N_PROJ_GROUPS):
        y = jnp.dot(xn, w_ref[:, grp * MIX:(grp + 1) * MIX], preferred_element_type=F32)
        if grp in (0, 1):
            y = rope(y, ca_ref, sa_ref, ROT_DIMS // 2)
        elif grp == 6:
            y = jax.nn.gelu(y)
        elif grp == 7:
            y = jax.nn.gelu(y)
            mu = jnp.mean(y, axis=-1, keepdims=True)
            yc = y - mu
            var = jnp.mean(yc * yc, axis=-1, keepdims=True)
            y = yc * lax.rsqrt(var + EPS) * lng_ref[...] + lnb_ref[...]
        elif grp == 8:
            y = rope(y, cd_ref, sd_ref, HEAD_DIM // 2)
        elif grp == 9:
            y = rope(y, cd_ref, sd_ref, HEAD_DIM // 2) * SCALE
        o_ref[:, grp * MIX:(grp + 1) * MIX] = y
        if t_refs and grp < N_ATTN_GROUPS:
            qt_ref, kv_refs, vtb_ref = t_refs[0], t_refs[1:1 + N_KV_OUT], t_refs[1 + N_KV_OUT]
            yt = y.T
            which, kind = divmod(grp, 3)
            if kind == 0:
                qt_ref[which] = yt
            else:
                kv_refs[2 * which + kind - 1][...] = yt
            if kind == 2:
                vtb_ref[which] = yt.astype(BF16)
    f = jnp.dot(xn, w_ref[:, N_PROJ_GROUPS * MIX:], preferred_element_type=F32) + bf_ref[...]
    o_ref[:, N_PROJ_GROUPS * MIX:] = _log_sigmoid(f)


def _proj(h, g, w_pad, bf_pad, ln_g, ln_b, tabs, tm, kv_all=None, layer=None):
    n, d = h.shape
    ca, sa, cd, sd = tabs
    tab_blocks = ca.shape[0] // tm
    full = lambda shape: pl.BlockSpec(shape, lambda i: (0, 0))
    tab = pl.BlockSpec((tm, LANES), lambda i: (i % tab_blocks, 0))
    in_specs = [pl.BlockSpec((tm, d), lambda i: (i, 0)), full((1, d)), full((d, W_IN_PAD)),
                full((1, LANES)), full((1, MIX)), full((1, MIX)), tab, tab, tab, tab]
    out_shape = [jax.ShapeDtypeStruct((n, W_IN_PAD), F32)]
    out_specs = [pl.BlockSpec((tm, W_IN_PAD), lambda i: (i, 0))]
    aliases, extra = {}, ()
    if kv_all is not None:
        _, batch, _, l = kv_all[0].shape
        per_seq = l // tm
        pair = pl.BlockSpec((None, 2, MIX, tm), lambda i: (i // per_seq, 0, 0, i % per_seq))
        aliases = {len(in_specs) + k: 1 + k + 1 for k in range(N_KV_OUT)}
        in_specs += [pl.BlockSpec(memory_space=pl.ANY)] * N_KV_OUT
        out_shape += ([jax.ShapeDtypeStruct((batch, 2, MIX, l), F32)]
                      + [jax.ShapeDtypeStruct(x.shape, F32) for x in kv_all]
                      + [jax.ShapeDtypeStruct((batch, 2, MIX, l), BF16)])
        out_specs += ([pair] + [pl.BlockSpec((None, None, MIX, tm),
                                             lambda i: (layer, i // per_seq, 0, i % per_seq))] * N_KV_OUT + [pair])
        extra = tuple(kv_all)
    return pl.pallas_call(
        _proj_kernel,
        out_shape=out_shape,
        grid=(n // tm,),
        in_specs=in_specs,
        out_specs=out_specs,
        input_output_aliases=aliases,
        compiler_params=_params("parallel"),
        name="proj",
    )(h, g, w_pad, bf_pad, ln_g, ln_b, ca, sa, cd, sd, *extra)


def _rope_tables(pos, n_rot, theta):
    half = n_rot // 2
    inv = theta ** (-jnp.arange(half, dtype=F32) / half)
    ang = pos.astype(F32)[:, None] * inv[None, :]
    cos, sin = jnp.cos(ang), jnp.sin(ang)
    rest = HEAD_DIM - n_rot
    ones = jnp.ones((pos.shape[0], rest), F32)
    zeros = jnp.zeros((pos.shape[0], rest), F32)
    c = jnp.concatenate([cos, cos, ones], axis=1)
    s = jnp.concatenate([-sin, sin, zeros], axis=1)
    reps = LANES // HEAD_DIM
    return jnp.tile(c, (1, reps)), jnp.tile(s, (1, reps))


def _kmean_kernel(k_ref, o_ref, *, nblk):
    x = k_ref[...].reshape(nblk, MOBA_BLOCK, MIX)
    o_ref[...] = jnp.sum(x, axis=1) * (1.0 / MOBA_BLOCK)


def _kmean_prompt(proj, col, b, l):
    nblk = 8
    assert l % (nblk * MOBA_BLOCK) == 0
    per_seq = l // (nblk * MOBA_BLOCK)
    return pl.pallas_call(
        functools.partial(_kmean_kernel, nblk=nblk),
        out_shape=jax.ShapeDtypeStruct((b, l // MOBA_BLOCK, MIX), F32),
        grid=(b, per_seq),
        in_specs=[pl.BlockSpec((nblk * MOBA_BLOCK, MIX), lambda i, j: (i * per_seq + j, col))],
        out_specs=pl.BlockSpec((None, nblk, MIX), lambda i, j: (i, j, 0)),
        compiler_params=_params("parallel", "parallel"),
        name="kmean_prompt",
    )(proj)


def _lane_cumsum(x):
    n = x.shape[-1]
    lane = lax.broadcasted_iota(jnp.int32, x.shape, x.ndim - 1)
    s = 1
    while s < n:
        x = x + jnp.where(lane >= s, pltpu.roll(x, s, x.ndim - 1), 0.0)
        s *= 2
    return x


def _cumsum_kernel(x_ref, o_ref):
    o_ref[...] = _lane_cumsum(x_ref[...])


def _cumsum_prompt(logf_t):
    r, l = logf_t.shape
    return pl.pallas_call(
        _cumsum_kernel,
        out_shape=jax.ShapeDtypeStruct((r, l), F32),
        grid=(1,),
        in_specs=[pl.BlockSpec((r, l), lambda i: (0, 0))],
        out_specs=pl.BlockSpec((r, l), lambda i: (0, 0)),
        compiler_params=_params("arbitrary"),
        name="cumsum_prompt",
    )(logf_t)


def _softmax_first(s, v):
    m = jnp.max(s, axis=1, keepdims=True)
    p = jnp.exp(s - m)
    l = jnp.sum(p, axis=1, keepdims=True)
    acc = jnp.dot(p.astype(BF16), v, preferred_element_type=F32)
    return m, l, acc


def _softmax_update(carry, s, v):
    m, l, acc = carry
    m_new = jnp.maximum(m, jnp.max(s, axis=1, keepdims=True))
    alpha = jnp.exp(m - m_new)
    p = jnp.exp(s - m_new)
    l = alpha * l + jnp.sum(p, axis=1, keepdims=True)
    acc = alpha * acc + jnp.dot(p.astype(BF16), v, preferred_element_type=F32)
    return m_new, l, acc


def _topk_select(gate, n_top, axis):
    pos = lax.broadcasted_iota(jnp.int32, gate.shape, axis)
    sel = jnp.zeros(gate.shape, F32)
    g = gate
    for _ in range(n_top):
        mx = jnp.max(g, axis=axis, keepdims=True)
        idx = jnp.min(jnp.where(g == mx, pos, gate.shape[axis]), axis=axis, keepdims=True)
        hit = pos == idx
        sel = jnp.maximum(sel, jnp.where(hit, jnp.where(mx > NEG_INF, 1.0, 0.0), 0.0))
        g = jnp.where(hit, NEG_INF, g)
    return sel


def _split3(x):
    hi = x.astype(BF16).astype(F32)
    r = x - hi
    mid = r.astype(BF16).astype(F32)
    return hi, mid, r - mid


def _attn_prompt_kernel(qt_ref, kt_ref, vt_ref, x_ref, o_ref, k_ref, qa_ref, *sel_refs, fox):
    t = MOBA_BLOCK
    i = pl.program_id(1)
    own = pl.multiple_of(i * t, t)
    causal = (lax.broadcasted_iota(jnp.int32, (t, t), 0) <= lax.broadcasted_iota(jnp.int32, (t, t), 1))
    pad = jnp.zeros((HEAD_DIM - AUG, t), F32)
    one = jnp.ones((1, t), F32)
    zero = jnp.zeros((1, t), F32)
    heads = range(N_HEADS)

    @pl.when(i == 0)
    def _():
        def build(jt, _):
            st = pl.multiple_of(jt * t, t)
            for h in heads:
                if fox:
                    hi, mid, lo = _split3(x_ref[h:h + 1, pl.ds(st, t)] * LOG2E)
                    aug = jnp.concatenate([one, one, one, -hi, -mid, -lo, zero, zero], axis=0)
                else:
                    aug = jnp.zeros((AUG, t), F32)
                kh = jnp.concatenate([kt_ref[h * HEAD_DIM:(h + 1) * HEAD_DIM, pl.ds(st, t)], aug, pad], axis=0)
                k_ref[h, pl.ds(st, t), :] = kh.T.astype(BF16)
            return 0

        lax.fori_loop(0, kt_ref.shape[1] // t, build, 0)

    for h in heads:
        qh = qt_ref[h * HEAD_DIM:(h + 1) * HEAD_DIM, :]
        if fox:
            aug = jnp.concatenate(_split3(x_ref[h:h + 1, pl.ds(own, t)] * LOG2E) + (one, one, one, zero, zero),
                                  axis=0)
        else:
            aug = jnp.zeros((AUG, t), F32)
            nb = x_ref.shape[1]
            gate = jnp.dot(x_ref[h], qh, precision=lax.Precision.HIGHEST, preferred_element_type=F32)
            blk = lax.broadcasted_iota(jnp.int32, (nb, t), 0)
            sel_refs[0][h] = _topk_select(jnp.where(blk < i, gate, NEG_INF), min(MOBA_TOPK, nb), 0)
        qa_ref[h] = jnp.concatenate([qh * (SCALE * LOG2E), aug, pad], axis=0).astype(BF16)

    def attend(carry, start, n, mask):
        ss = [mask(h, jnp.dot(k_ref[h, pl.ds(start, n * t), :], qa_ref[h], preferred_element_type=F32))
              for h in heads]
        ms = [jnp.max(s, axis=0, keepdims=True) for s in ss]
        if carry is not None:
            ms = [jnp.maximum(carry[3 * h], ms[h]) for h in heads]
        ps = [jnp.exp2(ss[h] - ms[h]) for h in heads]
        pvs = [jnp.dot(vt_ref[h * HEAD_DIM:(h + 1) * HEAD_DIM, pl.ds(start, n * t)], ps[h].astype(BF16),
                       preferred_element_type=F32) for h in heads]
        out = []
        for h in heads:
            l = jnp.sum(ps[h], axis=0, keepdims=True)
            if carry is None:
                out += [ms[h], l, pvs[h]]
            else:
                alpha = jnp.exp2(carry[3 * h] - ms[h])
                out += [ms[h], alpha * carry[3 * h + 1] + l, alpha * carry[3 * h + 2] + pvs[h]]
        return tuple(out)

    def past(n):
        def body(j, carry):
            def mask(h, s):
                if fox:
                    return s
                keep = jnp.concatenate([jnp.broadcast_to(sel_refs[0][h, pl.ds(j * n + k, 1), :], (t, t))
                                        for k in range(n)], axis=0)
                return jnp.where(keep > 0.5, s, NEG_INF)

            return attend(carry, pl.multiple_of(j * (n * t), n * t), n, mask)

        return body

    carry = attend(None, own, 1, lambda h, s: jnp.where(causal, s, NEG_INF))
    n, done = KV_BLOCKS_PER_TRIP, 0
    while n >= 1:
        trips = (i - done) // n
        carry = lax.fori_loop(done // n, done // n + trips, past(n), carry)
        done = done + trips * n
        n //= 2
    ot = jnp.concatenate([carry[3 * h + 2] / carry[3 * h + 1] for h in heads], axis=0)
    o_ref[...] = ot.T


def _attn_prompt(qt, k_all, layer, vtb, which, x, fox):
    b, _, _, l = qt.shape
    nb = l // MOBA_BLOCK
    t = MOBA_BLOCK
    if fox:
        x_spec = pl.BlockSpec((None, N_HEADS, l), lambda bi, i: (bi, 0, 0))
        scratch = []
    else:
        x_spec = pl.BlockSpec((None, N_HEADS, nb, HEAD_DIM), lambda bi, i: (bi, 0, 0, 0))
        scratch = [pltpu.VMEM((N_HEADS, nb, t), F32)]
    return pl.pallas_call(
        functools.partial(_attn_prompt_kernel, fox=fox),
        out_shape=jax.ShapeDtypeStruct((b * l, MIX), F32),
        grid=(b, nb),
        in_specs=[pl.BlockSpec((None, None, MIX, t), lambda bi, i: (bi, which, 0, i)),
                  pl.BlockSpec((None, None, MIX, l), lambda bi, i: (layer, bi, 0, 0)),
                  pl.BlockSpec((None, None, MIX, l), lambda bi, i: (bi, which, 0, 0)),
                  x_spec],
        out_specs=pl.BlockSpec((t, MIX), lambda bi, i: (bi * nb + i, 0)),
        scratch_shapes=[pltpu.VMEM((N_HEADS, l, 2 * HEAD_DIM), BF16),
                        pltpu.VMEM((N_HEADS, 2 * HEAD_DIM, t), BF16)] + scratch,
        compiler_params=_params("parallel", "arbitrary"),
        name="fox_prompt" if fox else "moba_prompt",
    )(qt, k_all, vtb, x)


def _gmlp_kernel(u_ref, v_ref, w_ref, b_ref, o_ref, *, c, cps):
    grp = lax.broadcasted_iota(jnp.int32, (c, MIX), 1) // HEAD_DIM
    row = lax.broadcasted_iota(jnp.int32, (c, c), 0)
    col = lax.broadcasted_iota(jnp.int32, (c, c), 1)
    w_cat = jnp.concatenate([jnp.where(col <= row, w_ref[g], 0.0) for g in range(N_HEADS)], axis=1).astype(BF16)
    bias = b_ref[...]
    for ci in range(cps):
        rows = slice(ci * c, (ci + 1) * c)
        v = v_ref[rows, :]
        v_cat = jnp.concatenate([jnp.where(grp == g, v, 0.0) for g in range(N_HEADS)], axis=0)
        mixed = jnp.dot(w_cat, v_cat.astype(BF16), preferred_element_type=F32)
        o_ref[rows, :] = u_ref[rows, :] * (mixed + bias)


def _gmlp(proj, u_col, v_col, ws, bs, c):
    n = proj.shape[0]
    bias = jnp.repeat(bs.T, HEAD_DIM, axis=1)
    cps = 8 if n % (8 * c) == 0 else 1
    col = lambda g: pl.BlockSpec((cps * c, MIX), lambda i: (i, g))
    return pl.pallas_call(
        functools.partial(_gmlp_kernel, c=c, cps=cps),
        out_shape=jax.ShapeDtypeStruct((n, MIX), F32),
        grid=(n // (cps * c),),
        in_specs=[col(u_col), col(v_col), pl.BlockSpec((N_HEADS, c, c), lambda i: (0, 0, 0)),
                  pl.BlockSpec((c, MIX), lambda i: (0, 0))],
        out_specs=col(0),
        compiler_params=_params("parallel"),
        name="gmlp",
    )(proj, proj, ws, bias)


def _retention_kernel(q_ref, kt_ref, v_ref, g_ref, s0_ref, dm_ref, qd_ref, kd_ref, cd_ref, gn_ref,
                      o_ref, s_ref, state, *, c, cps):
    step = pl.program_id(1)

    @pl.when(step == 0)
    def _():
        state[...] = s0_ref[...]

    for h in range(N_HEADS):
        s = state[h]
        for ci in range(cps):
            rows = slice(ci * c, (ci + 1) * c)
            qc = q_ref[h, rows, :].astype(BF16)
            vc = v_ref[h, rows, :].astype(BF16)
            kt = kt_ref[h, :, rows]
            att = jnp.dot(qc, kt.astype(BF16), preferred_element_type=F32) * dm_ref[h]
            o = (jnp.dot(att.astype(BF16), vc, preferred_element_type=F32)
                 + jnp.dot(qc, s.astype(BF16), preferred_element_type=F32) * qd_ref[h])
            s = s * cd_ref[h] + jnp.dot((kt * kd_ref[h]).astype(BF16), vc, preferred_element_type=F32)
            mu = jnp.mean(o, axis=-1, keepdims=True)
            oc = o - mu
            var = jnp.mean(oc * oc, axis=-1, keepdims=True)
            o_ref[h, rows, :] = jax.nn.silu(g_ref[h, rows, :]) * (oc * lax.rsqrt(var + EPS) * gn_ref[h])
        state[h] = s

        @pl.when(step == pl.num_programs(1) - 1)
        def _():
            s_ref[h] = s


def _retention(q, kt, v, g, s0, gn_g, c, cps):
    b, h, l, dh = q.shape
    lg = jnp.log(1.0 - 2.0 ** (-5.0 - jnp.arange(h, dtype=F32)))
    j = jnp.arange(c, dtype=F32)
    diff = j[:, None] - j[None, :]
    dmat = jnp.where(diff[None] >= 0, jnp.exp(jnp.maximum(diff, 0.0)[None] * lg[:, None, None]), 0.0)
    q_dec = jnp.broadcast_to(jnp.exp((j[None, :] + 1.0) * lg[:, None])[:, :, None], (h, c, dh))
    k_dec = jnp.broadcast_to(jnp.exp((c - 1.0 - j)[None, :] * lg[:, None])[:, None, :], (h, dh, c))
    chunk_dec = jnp.broadcast_to(jnp.exp(c * lg)[:, None, None], (h, dh, dh))
    r = c * cps
    rows = pl.BlockSpec((None, h, r, dh), lambda bi, i: (bi, 0, i, 0))
    per_b = pl.BlockSpec((None, h, dh, dh), lambda bi, i: (bi, 0, 0, 0))
    table = lambda s1, s2: pl.BlockSpec((h, s1, s2), lambda bi, i: (0, 0, 0))
    return pl.pallas_call(
        functools.partial(_retention_kernel, c=c, cps=cps),
        out_shape=(jax.ShapeDtypeStruct((b, h, l, dh), F32), jax.ShapeDtypeStruct((b, h, dh, dh), F32)),
        grid=(b, l // r),
        in_specs=[rows, pl.BlockSpec((None, h, dh, r), lambda bi, i: (bi, 0, 0, i)), rows, rows,
                  per_b, table(c, c), table(c, dh), table(dh, c), table(dh, dh), table(1, dh)],
        out_specs=(rows, per_b),
        scratch_shapes=[pltpu.VMEM((h, dh, dh), F32)],
        compiler_params=_params("parallel", "arbitrary"),
        name="retention",
    )(q, kt, v, g, s0, dmat, q_dec, k_dec, chunk_dec, gn_g.reshape(h, 1, dh))


def _retention_rows_kernel(q_ref, k_ref, v_ref, g_ref, s0_ref, dm_ref, qd_ref, kd_ref, cd_ref, bd_ref, gn_ref,
                           o_ref, s_ref, state, *, c, cps):
    step = pl.program_id(1)

    @pl.when(step == 0)
    def _():
        state[...] = s0_ref[...]

    heads = range(N_HEADS)
    grp = lax.broadcasted_iota(jnp.int32, (c, MIX), 1) // HEAD_DIM
    bd = bd_ref[...]
    avg = (bd * (1.0 / HEAD_DIM)).astype(BF16)
    s = state[...]
    for ci in range(cps):
        rows = slice(ci * c, (ci + 1) * c)
        q, k, v = q_ref[rows, :], k_ref[rows, :], v_ref[rows, :]
        kb, vb = k.astype(BF16), v.astype(BF16)
        att = jnp.concatenate([_nt_dot(jnp.where(grp == h, q, 0.0).astype(BF16), kb) * dm_ref[h] for h in heads],
                              axis=1)
        v_cat = jnp.concatenate([jnp.where(grp == h, v, 0.0).astype(BF16) for h in heads], axis=0)
        o = (jnp.dot(att.astype(BF16), v_cat, preferred_element_type=F32)
             + jnp.dot(q.astype(BF16), s.astype(BF16), preferred_element_type=F32) * qd_ref[...])
        s = s * cd_ref[...] + bd * jnp.dot((k * kd_ref[...]).T.astype(BF16), vb, preferred_element_type=F32)
        mu = jnp.dot(o.astype(BF16), avg, preferred_element_type=F32)
        oc = o - mu
        var = jnp.dot((oc * oc).astype(BF16), avg, preferred_element_type=F32)
        o_ref[rows, :] = jax.nn.silu(g_ref[rows, :]) * (oc * lax.rsqrt(var + EPS) * gn_ref[...])
    state[...] = s

    @pl.when(step == pl.num_programs(1) - 1)
    def _():
        s_ref[...] = s


def _retention_rows(proj, cols, b, l, gn_g, c, cps):
    h, dh = N_HEADS, HEAD_DIM
    lg = jnp.log(1.0 - 2.0 ** (-5.0 - jnp.arange(h, dtype=F32)))
    j = jnp.arange(c, dtype=F32)
    diff = j[:, None] - j[None, :]
    dmat = jnp.where(diff[None] >= 0, jnp.exp(jnp.maximum(diff, 0.0)[None] * lg[:, None, None]), 0.0)
    lanes = lambda x: jnp.repeat(x, dh, axis=-1)
    q_dec = lanes(jnp.exp((j[:, None] + 1.0) * lg[None, :]))
    k_dec = lanes(jnp.exp((c - 1.0 - j)[:, None] * lg[None, :]))
    chunk_dec = jnp.broadcast_to(lanes(jnp.exp(c * lg)[None, :]), (MIX, MIX))
    head_of = jnp.arange(MIX) // dh
    bd = (head_of[:, None] == head_of[None, :]).astype(F32)
    r = c * cps
    per = l // r
    col = lambda g: pl.BlockSpec((r, MIX), lambda bi, i: (bi * per + i, g))
    const = lambda *shape: pl.BlockSpec(shape, lambda bi, i: (0,) * len(shape))
    state = pl.BlockSpec((None, MIX, MIX), lambda bi, i: (bi, 0, 0))
    od, s_bd = pl.pallas_call(
        functools.partial(_retention_rows_kernel, c=c, cps=cps),
        out_shape=(jax.ShapeDtypeStruct((b * l, MIX), F32), jax.ShapeDtypeStruct((b, MIX, MIX), F32)),
        grid=(b, per),
        in_specs=[col(cols[0]), col(cols[1]), col(cols[2]), col(cols[3]), state, const(h, c, c),
                  const(c, MIX), const(c, MIX), const(MIX, MIX), const(MIX, MIX), const(1, MIX)],
        out_specs=(col(0), state),
        scratch_shapes=[pltpu.VMEM((MIX, MIX), F32)],
        compiler_params=_params("parallel", "arbitrary"),
        name="retention_rows",
    )(proj, proj, proj, proj, jnp.zeros((b, MIX, MIX), F32), dmat, q_dec, k_dec, chunk_dec, bd, gn_g[None])
    s5 = s_bd.reshape(b, h, dh, h, dh)
    return od, jnp.stack([s5[:, i, :, i, :] for i in range(h)], axis=1)


def _merge_kernel(h_ref, g_ref, oa_ref, ob_ref, oc_ref, od_ref, wm_ref, wb_ref, wo_ref, o_ref):
    h = h_ref[...]
    d = h.shape[1]
    xn = _rms(h, g_ref[...]).astype(BF16)
    z = None
    for n, br_ref in enumerate((oa_ref, ob_ref, oc_ref, od_ref)):
        gate = jax.nn.sigmoid(jnp.dot(xn, wm_ref[:, n * d:(n + 1) * d], preferred_element_type=F32))
        br = jnp.dot(br_ref[...].astype(BF16), wb_ref[n], preferred_element_type=F32)
        z = gate * br if z is None else z + gate * br
    o_ref[...] = h + jnp.dot(z.astype(BF16), wo_ref[...], preferred_element_type=F32)


def _merge(h, g, branches, w_merge, w_branch, w_out, tm):
    n, d = h.shape
    rows = lambda w: pl.BlockSpec((tm, w), lambda i: (i, 0))
    return pl.pallas_call(
        _merge_kernel,
        out_shape=jax.ShapeDtypeStruct((n, d), F32),
        grid=(n // tm,),
        in_specs=[rows(d), pl.BlockSpec((1, d), lambda i: (0, 0)), rows(MIX), rows(MIX), rows(MIX), rows(MIX),
                  pl.BlockSpec(w_merge.shape, lambda i: (0, 0)),
                  pl.BlockSpec(w_branch.shape, lambda i: (0, 0, 0)),
                  pl.BlockSpec(w_out.shape, lambda i: (0, 0))],
        out_specs=rows(d),
        compiler_params=_params("parallel"),
        name="merge",
    )(h, g, *branches, w_merge, w_branch, w_out)


def _ffn_kernel(h_ref, g_ref, wg_ref, wu_ref, wo_ref, o_ref, xn_ref, acc_ref):
    k = pl.program_id(1)

    @pl.when(k == 0)
    def _():
        xn_ref[...] = _rms(h_ref[...], g_ref[...]).astype(BF16)
        acc_ref[...] = h_ref[...]

    xn = xn_ref[...]
    tf = wg_ref.shape[1]
    slab = FFN_SLAB if tf % FFN_SLAB == 0 else tf
    part = None
    for lo in range(0, tf, slab):
        gate = jnp.dot(xn, wg_ref[:, lo:lo + slab], preferred_element_type=F32)
        up = jnp.dot(xn, wu_ref[:, lo:lo + slab], preferred_element_type=F32)
        act = (jax.nn.silu(gate) * up).astype(BF16)
        out = jnp.dot(act, wo_ref[lo:lo + slab, :], preferred_element_type=F32)
        part = out if part is None else part + out
    acc_ref[...] += part

    @pl.when(k == pl.num_programs(1) - 1)
    def _():
        o_ref[...] = acc_ref[...]


def _ffn(h, g, w_in, w_out, tm, tf):
    n, d = h.shape
    d_ff = w_out.shape[0]
    nk = d_ff // tf
    return pl.pallas_call(
        _ffn_kernel,
        out_shape=jax.ShapeDtypeStruct((n, d), F32),
        grid=(n // tm, nk),
        in_specs=[pl.BlockSpec((tm, d), lambda i, k: (i, 0)), pl.BlockSpec((1, d), lambda i, k: (0, 0)),
                  pl.BlockSpec((d, tf), lambda i, k: (0, k)), pl.BlockSpec((d, tf), lambda i, k: (0, nk + k)),
                  pl.BlockSpec((tf, d), lambda i, k: (k, 0))],
        out_specs=pl.BlockSpec((tm, d), lambda i, k: (i, 0)),
        scratch_shapes=[pltpu.VMEM((tm, d), BF16), pltpu.VMEM((tm, d), F32)],
        compiler_params=_params("parallel", "arbitrary"),
        name="ffn",
    )(h, g, w_in, w_in, w_out)


def _ple_kernel(h_ref, g_ref, p_ref, wg_ref, wp_ref, gf_ref, o_ref, *, final):
    h = h_ref[...]
    xn = _rms(h, g_ref[...]).astype(BF16)
    gate = jax.nn.sigmoid(jnp.dot(xn, wg_ref[...], preferred_element_type=F32))
    emb = jnp.dot(p_ref[...].astype(BF16), wp_ref[...], preferred_element_type=F32)
    out = h + gate * emb
    o_ref[...] = _rms(out, gf_ref[...]) if final else out


def _ple(h, g, p, w_gate, w_proj, g_final, tm, final):
    n, d = h.shape
    p, layer = p
    dp = p.shape[2]
    vec = pl.BlockSpec((1, d), lambda i: (0, 0))
    return pl.pallas_call(
        functools.partial(_ple_kernel, final=final),
        out_shape=jax.ShapeDtypeStruct((n, d), F32),
        grid=(n // tm,),
        in_specs=[pl.BlockSpec((tm, d), lambda i: (i, 0)), vec, pl.BlockSpec((None, tm, dp), lambda i: (layer, i, 0)),
                  pl.BlockSpec((d, d), lambda i: (0, 0)), pl.BlockSpec((dp, d), lambda i: (0, 0)), vec],
        out_specs=pl.BlockSpec((tm, d), lambda i: (i, 0)),
        compiler_params=_params("parallel"),
        name="ple",
    )(h, g, p, w_gate, w_proj, g_final)


def _stream_pages(pt_ref, streams, sems, layer, pp, n_chunks, n_seq, reverse, step):
    total = n_seq * n_chunks
    assert total % 2 == 0

    def locate(g):
        b = g // n_chunks
        c = g - b * n_chunks
        return b, (n_chunks - 1 - c if reverse else c)

    def copies(page_of, slot):
        return [pltpu.make_async_copy(hbm.at[layer, page_of(k)], buf.at[slot, k], sems.at[n, slot])
                for k in range(pp) for n, (hbm, buf) in enumerate(streams)]

    def fetch(g, slot):
        b, c = locate(g)
        for n, cp in enumerate(copies(lambda k: pt_ref[b, c * pp + k], slot)):
            cp.start(priority=n % 2)

    def pair(i, _):
        for slot in (0, 1):
            g = 2 * i + slot

            @pl.when(g + 1 < total)
            def _():
                fetch(g + 1, 1 - slot)

            for cp in copies(lambda k: 0, slot):
                cp.wait()
            step(g, *locate(g), slot)
        return 0

    fetch(0, 0)
    lax.fori_loop(0, total // 2, pair, 0)


def _paged_call(body, page_table, caches, dense, out_shape, scratch, name, page_shapes):
    whole = lambda x: pl.BlockSpec(x.shape, lambda i, pt, n=len(x.shape): (0,) * n)
    grid_spec = pltpu.PrefetchScalarGridSpec(
        num_scalar_prefetch=1, grid=(1,),
        in_specs=[pl.BlockSpec(memory_space=pl.ANY)] * len(caches) + [whole(x) for x in dense],
        out_specs=whole(out_shape),
        scratch_shapes=[pltpu.VMEM((2, PAGES_PER_STEP) + shape, F32) for shape in page_shapes]
        + [pltpu.SemaphoreType.DMA((len(caches), 2))] + scratch)
    return pl.pallas_call(body, out_shape=out_shape, grid_spec=grid_spec,
                          compiler_params=_params("arbitrary"), name=name)(page_table, *caches, *dense)


def _kmean_sample_kernel(pt_ref, k_hbm, o_ref, kbuf, sems, *, layer, pp, n_chunks):
    per_block = MOBA_BLOCK // PAGE_SIZE
    nblk = pp // per_block
    lane = lax.broadcasted_iota(jnp.int32, (MIX, LANES), 1)

    def step(g, b, c, slot):
        @pl.when(c == 0)
        def _():
            o_ref[b] = jnp.zeros((MIX, LANES), F32)

        acc = o_ref[b]
        for blk in range(nblk):
            tot = kbuf[slot, blk * per_block]
            for k in range(1, per_block):
                tot = tot + kbuf[slot, blk * per_block + k]
            mean = jnp.sum(tot, axis=1, keepdims=True) * (1.0 / MOBA_BLOCK)
            acc = jnp.where(lane == c * nblk + blk, mean, acc)
        o_ref[b] = acc

    _stream_pages(pt_ref, [(k_hbm, kbuf)], sems, layer, pp, n_chunks, o_ref.shape[0], False, step)


def _kmean_sample(cache_k, layer, page_table):
    db, n_pages = page_table.shape
    pp = PAGES_PER_STEP
    assert n_pages * PAGE_SIZE // MOBA_BLOCK <= LANES
    return _paged_call(
        functools.partial(_kmean_sample_kernel, layer=layer, pp=pp, n_chunks=n_pages // pp),
        page_table, [cache_k], [], jax.ShapeDtypeStruct((db, MIX, LANES), F32), [], "kmean_sample",
        [(MIX, PAGE_SIZE)])


def _lane_suffix_sum(x):
    n = x.shape[-1]
    lane = lax.broadcasted_iota(jnp.int32, x.shape, x.ndim - 1)
    s = 1
    while s < n:
        x = x + jnp.where(lane < n - s, pltpu.roll(x, n - s, x.ndim - 1), 0.0)
        s *= 2
    return x


def _expand_heads(x, t):
    return jnp.concatenate([jnp.broadcast_to(x[h:h + 1, :], (t, x.shape[1])) for h in range(N_HEADS)], axis=0)


def _decode_attn_kernel(pt_ref, *refs, layer, pp, n_chunks, t, fox, nb):
    if fox:
        (k_hbm, v_hbm, f_hbm, q_ref, kn_ref, vn_ref, fn_ref, o_ref, kbuf, vbuf, fbuf, sems,
         s_ref, m_ref, l_ref, acc_ref, cq_ref, tail_ref) = refs
        streams = [(k_hbm, kbuf), (v_hbm, vbuf), (f_hbm, fbuf)]
    else:
        (k_hbm, v_hbm, q_ref, kn_ref, vn_ref, km_ref, o_ref, kbuf, vbuf, sems,
         s_ref, m_ref, l_ref, acc_ref, sel_ref) = refs
        streams = [(k_hbm, kbuf), (v_hbm, vbuf)]
    r = N_HEADS * t

    def step(g, b, c, slot):
        first = c == (n_chunks - 1 if fox else 0)
        last = c == (0 if fox else n_chunks - 1)
        q = q_ref[b]
        qb = (q * SCALE).astype(BF16)

        @pl.when(first)
        def _():
            s = jnp.dot(qb, kn_ref[b].astype(BF16), preferred_element_type=F32)
            key = lax.broadcasted_iota(jnp.int32, (r, LANES), 1)
            qi = lax.broadcasted_iota(jnp.int32, (r, LANES), 0) % t
            if fox:
                cn = _expand_heads(_lane_cumsum(fn_ref[b]), t)
                cq = jnp.sum(jnp.where(key == qi, cn, 0.0), axis=1, keepdims=True)
                cq_ref[...] = jnp.broadcast_to(cq, cq_ref.shape)
                tail_ref[...] = jnp.zeros(tail_ref.shape, F32)
                s = s + cq - cn
            s = jnp.where(key <= qi, s, NEG_INF)
            m = jnp.max(s, axis=1, keepdims=True)
            p = jnp.exp(s - m)
            m_ref[...] = jnp.broadcast_to(m, m_ref.shape)
            l_ref[...] = jnp.broadcast_to(jnp.sum(p, axis=1, keepdims=True), l_ref.shape)
            acc_ref[...] = _nt_dot(p.astype(BF16), vn_ref[b].astype(BF16))
            if not fox:
                gate = jnp.dot(q, km_ref[b], precision=lax.Precision.HIGHEST, preferred_element_type=F32)
                blk = lax.broadcasted_iota(jnp.int32, gate.shape, 1)
                sel_ref[...] = _topk_select(jnp.where(blk < nb, gate, NEG_INF), min(MOBA_TOPK, nb), 1)

        for k in range(pp):
            s_ref[:, k * PAGE_SIZE:(k + 1) * PAGE_SIZE] = jnp.dot(qb, kbuf[slot, k].astype(BF16),
                                                                  preferred_element_type=F32)
        s = s_ref[...]
        if fox:
            f = jnp.concatenate([fbuf[slot, k] for k in range(pp)], axis=1)
            suffix = _lane_suffix_sum(f)
            tail = tail_ref[:, :1]
            s = s + jnp.concatenate([cq_ref[...]] * pp, axis=1) + _expand_heads(suffix - f + tail, t)
            tail_ref[...] = jnp.broadcast_to(tail + suffix[:, :1], tail_ref.shape)
        else:
            per_block = MOBA_BLOCK // PAGE_SIZE
            nblk = pp // per_block
            sel = sel_ref[...]
            blk = lax.broadcasted_iota(jnp.int32, sel.shape, 1)
            cols = []
            for j in range(nblk):
                chosen = jnp.sum(jnp.where(blk == c * nblk + j, sel, 0.0), axis=1, keepdims=True)
                cols.append(jnp.broadcast_to(chosen, (r, MOBA_BLOCK)))
            s = jnp.where(jnp.concatenate(cols, axis=1) > 0.5, s, NEG_INF)
        m = m_ref[:, :1]
        m_new = jnp.maximum(m, jnp.max(s, axis=1, keepdims=True))
        alpha = jnp.exp(m - m_new)
        p = jnp.exp(s - m_new).astype(BF16)
        l_new = alpha * l_ref[:, :1] + jnp.sum(p.astype(F32), axis=1, keepdims=True)
        acc = alpha * acc_ref[...]
        for k in range(pp):
            acc = acc + _nt_dot(p[:, k * PAGE_SIZE:(k + 1) * PAGE_SIZE], vbuf[slot, k].astype(BF16))
        m_ref[...] = jnp.broadcast_to(m_new, m_ref.shape)
        l_ref[...] = jnp.broadcast_to(l_new, l_ref.shape)
        acc_ref[...] = acc

        @pl.when(last)
        def _():
            o = acc / l_new
            grp = lax.broadcasted_iota(jnp.int32, (t, MIX), 1) // HEAD_DIM
            out = jnp.zeros((t, MIX), F32)
            for h in range(N_HEADS):
                out = out + jnp.where(grp == h, o[h * t:(h + 1) * t, :], 0.0)
            o_ref[b] = out

    _stream_pages(pt_ref, streams, sems, layer, pp, n_chunks, q_ref.shape[0], fox, step)


def _decode_attn(cache_k, cache_v, layer, page_table, q_blk, k_new, v_new, extra, fox):
    db, n_pages = page_table.shape
    pp = PAGES_PER_STEP
    r = q_blk.shape[1]
    t = r // N_HEADS
    scratch = [pltpu.VMEM((r, pp * PAGE_SIZE), F32), pltpu.VMEM((r, LANES), F32), pltpu.VMEM((r, LANES), F32),
               pltpu.VMEM((r, MIX), F32), pltpu.VMEM((r, LANES), F32)]
    caches, pages = [cache_k, cache_v], [(MIX, PAGE_SIZE), (MIX, PAGE_SIZE)]
    if fox:
        caches, pages = caches + [extra[0]], pages + [(N_HEADS, PAGE_SIZE)]
        scratch += [pltpu.VMEM((N_HEADS, LANES), F32)]
    return _paged_call(
        functools.partial(_decode_attn_kernel, layer=layer, pp=pp, n_chunks=n_pages // pp, t=t, fox=fox,
                          nb=n_pages * PAGE_SIZE // MOBA_BLOCK),
        page_table, caches, [q_blk, k_new, v_new, extra[-1]], jax.ShapeDtypeStruct((db, t, MIX), F32), scratch,
        "fox_sample" if fox else "moba_sample", pages)


def _to_heads(x, b, l):
    return x.reshape(b, l, N_HEADS, HEAD_DIM).transpose(0, 2, 1, 3)


def _from_heads(x):
    b, h, l, dh = x.shape
    return x.transpose(0, 2, 1, 3).reshape(b * l, h * dh)


def _split_proj(proj):
    cols = [proj[:, g * MIX:(g + 1) * MIX] for g in range(N_PROJ_GROUPS)]
    logf = proj[:, N_PROJ_GROUPS * MIX:N_PROJ_GROUPS * MIX + N_HEADS]
    return cols, logf


def _tail(h, p, prm, branches, final, tm):
    h = _merge(h, prm["norm_mix"], branches, prm["w_merge"], prm["w_branch"], prm["w_out"], tm)
    d_ff = prm["w_ffn_out"].shape[0]
    tf = d_ff
    h = _ffn(h, prm["norm_ffn"], prm["w_ffn_in"], prm["w_ffn_out"], min(512, h.shape[0]), tf)
    return _ple(h, prm["norm_ple"], p, prm["w_ple_gate"], prm["w_ple_proj"], prm["norm_final"], tm, final)


def _prompt_layer(h, p, prm, tabs, kv_all, layer, b, l, final):
    tm = 512
    proj, qt, *kv_all, vtb = _proj(h, prm["norm_mix"], prm["w_in"], prm["b_forget"], prm["ln_g"], prm["ln_b"],
                                   tabs, tm, kv_all=kv_all, layer=layer)
    logf = proj[:, N_PROJ_GROUPS * MIX:N_PROJ_GROUPS * MIX + N_HEADS]
    kmean = _kmean_prompt(proj, 1, b, l).reshape(b, -1, N_HEADS, HEAD_DIM).transpose(0, 2, 1, 3)
    oa = _attn_prompt(qt, kv_all[0], layer, vtb, 0, kmean, fox=False)
    c = _cumsum_prompt(logf.reshape(b, l, N_HEADS).transpose(0, 2, 1).reshape(b * N_HEADS, l))
    ob = _attn_prompt(qt, kv_all[2], layer, vtb, 1, c.reshape(b, N_HEADS, l), fox=True)
    oc = _gmlp(proj, 6, 7, prm["gmlp_ws"], prm["gmlp_bs"], GMLP_CHUNK)
    cps = 8 if l % (8 * RET_CHUNK) == 0 else 1
    od, s_new = _retention_rows(proj, (8, 9, 10, 11), b, l, prm["ret_gn_g"], RET_CHUNK, cps)
    h = _tail(h, p, prm, (oa, ob, oc, od), final, tm)
    return h, kv_all, (logf.reshape(b, l, N_HEADS), s_new)


def _block_diag_q(q, db, t):
    grp = jnp.arange(MIX) // HEAD_DIM
    mask = (grp[None, None, :] == jnp.arange(N_HEADS)[:, None, None]).astype(F32)
    return (q.reshape(db, 1, t, MIX) * mask[None]).reshape(db, N_HEADS * t, MIX)


def _new_keys_t(x, db, t):
    return jnp.pad(x.reshape(db, t, MIX).transpose(0, 2, 1), ((0, 0), (0, 0), (0, LANES - t)))


def _sample_layer(h, p, prm, tabs, caches, layer, page_table, db, t, final):
    cache_k_a, cache_v_a, cache_k_b, cache_v_b, cache_logf_t, state = caches
    n = db * t
    (proj,) = _proj(h, prm["norm_mix"], prm["w_in"], prm["b_forget"], prm["ln_g"], prm["ln_b"], tabs, n)
    (qa, ka, va, qb, kb, vb, _, vn, qd, kd, vd, gd), logf = _split_proj(proj)
    kmean = _kmean_sample(cache_k_a, layer, page_table)
    oa = _decode_attn(cache_k_a, cache_v_a, layer, page_table, _block_diag_q(qa, db, t),
                      _new_keys_t(ka, db, t), _new_keys_t(va, db, t), (kmean,), fox=False)
    logf_new = jnp.pad(logf.reshape(db, t, N_HEADS).transpose(0, 2, 1), ((0, 0), (0, 0), (0, LANES - t)))
    ob = _decode_attn(cache_k_b, cache_v_b, layer, page_table, _block_diag_q(qb, db, t),
                      _new_keys_t(kb, db, t), _new_keys_t(vb, db, t), (cache_logf_t, logf_new), fox=True)
    oc = _gmlp(proj, 6, 7, prm["gmlp_ws"][:, :t, :t], prm["gmlp_bs"][:, :t], t)
    hd = lambda x: _to_heads(x, db, t)
    od, s_new = _retention(hd(qd), hd(kd).transpose(0, 1, 3, 2), hd(vd), hd(gd), state[layer],
                           prm["ret_gn_g"], t, 1)
    h = _tail(h, p, prm, (oa.reshape(n, MIX), ob.reshape(n, MIX), oc, _from_heads(od)), final, n)
    rows = (ka.reshape(db, t, N_HEADS, HEAD_DIM), va.reshape(db, t, N_HEADS, HEAD_DIM),
            kb.reshape(db, t, N_HEADS, HEAD_DIM), vb.reshape(db, t, N_HEADS, HEAD_DIM),
            logf.reshape(db, t, N_HEADS), s_new, vn.reshape(db, t, MIX))
    return h, rows


def kernel(x_prompt, x_sample, cache_k_a, cache_v_a, cache_k_b, cache_v_b, cache_logf_b, state_ret, page_table, p_prompt, p_sample, norm_mix, w_in, b_forget, gmlp_ln_g, gmlp_ln_b, gmlp_ws, gmlp_bs, ret_gn_g, w_branch, w_merge, w_out, norm_ffn, w_ffn_in, w_ffn_out, norm_ple, w_ple_gate, w_ple_proj, norm_final):
    b, l, d = x_prompt.shape
    db, t, _ = x_sample.shape
    depth = w_in.shape[0]
    n_pages = page_table.shape[1]
    past_len = n_pages * PAGE_SIZE
    assert d == N_HEADS * MIX and l % MOBA_BLOCK == 0
    assert past_len % MOBA_BLOCK == 0 and t <= LANES and t % 8 == 0 and t < MOBA_BLOCK
    assert n_pages % PAGES_PER_STEP == 0

    pos_p = jnp.arange(l, dtype=jnp.int32)
    pos_s = jnp.tile(past_len + jnp.arange(t, dtype=jnp.int32), db)
    tabs_p = _rope_tables(pos_p, ROT_DIMS, ROPE_THETA) + _rope_tables(pos_p, HEAD_DIM, RET_THETA)
    tabs_s = _rope_tables(pos_s, ROT_DIMS, ROPE_THETA) + _rope_tables(pos_s, HEAD_DIM, RET_THETA)

    pool = cache_k_a.shape[1]
    pages = lambda x: x.transpose(0, 1, 3, 4, 2).reshape(depth, pool, MIX, PAGE_SIZE)
    caches = (pages(cache_k_a), pages(cache_v_a), pages(cache_k_b), pages(cache_v_b),
              cache_logf_b.transpose(0, 1, 3, 2), state_ret)

    hp = x_prompt.reshape(b * l, d)
    hs = x_sample.reshape(db * t, d)
    rows_p, rows_s = [], []
    kv_all = [jnp.zeros((depth, b, MIX, l), F32) for _ in range(N_KV_OUT)]
    f_lo, f_hi = 6 * MIX, 6 * MIX + N_HEADS
    w_in_pad = jnp.concatenate(
        [w_in[:, :, :f_lo].astype(BF16), w_in[:, :, f_hi:].astype(BF16),
         jnp.pad(w_in[:, :, f_lo:f_hi].astype(BF16), ((0, 0), (0, 0), (0, LANES - N_HEADS)))], axis=2)
    for i in range(depth):
        prm = dict(
            norm_mix=norm_mix[i][None], w_in=w_in_pad[i],
            b_forget=jnp.pad(b_forget[i], (0, LANES - N_HEADS))[None],
            ln_g=gmlp_ln_g[i][None], ln_b=gmlp_ln_b[i][None], gmlp_ws=gmlp_ws[i], gmlp_bs=gmlp_bs[i],
            ret_gn_g=ret_gn_g[i], w_branch=w_branch[i].astype(BF16), w_merge=w_merge[i].astype(BF16),
            w_out=w_out[i].astype(BF16), norm_ffn=norm_ffn[i][None], w_ffn_in=w_ffn_in[i].astype(BF16),
            w_ffn_out=w_ffn_out[i].astype(BF16), norm_ple=norm_ple[i][None],
            w_ple_gate=w_ple_gate[i].astype(BF16), w_ple_proj=w_ple_proj[i].astype(BF16),
            norm_final=norm_final[None])
        final = i == depth - 1
        hp, kv_all, rp = _prompt_layer(hp, (p_prompt.reshape(depth, b * l, -1), i), prm, tabs_p, kv_all, i,
                                       b, l, final)
        hs, rs = _sample_layer(hs, (p_sample.reshape(depth, db * t, -1), i), prm, tabs_s, caches, i, page_table,
                               db, t, final)
        rows_p.append(rp)
        rows_s.append(rs)
    stack = lambda rows, j: jnp.stack([r[j] for r in rows], axis=0)
    kv_out = tuple(x.reshape(depth, b, N_HEADS, HEAD_DIM, l).transpose(0, 1, 4, 2, 3) for x in kv_all)
    return ((hp.reshape(b, l, d), hs.reshape(db, t, d)) + kv_out
            + tuple(stack(rows_p, j) for j in range(2)) + tuple(stack(rows_s, j) for j in range(7)))
```
